```python
import math
import jax
import jax.numpy as jnp
from jax import lax
import numpy as np

D_MODEL = 2048
BATCH = 4
SEQ = 4096
DEPTH = 2

CHUNK = 64
N_MIXERS = 2
GMLP_BLOCK = 128
GMLP_HEADS = 16
GMLP_DIM = D_MODEL
GMLP_HEAD_DIM = GMLP_DIM // GMLP_HEADS
HGRN_HEAD_DIM = 128
HGRN_HEADS = D_MODEL // HGRN_HEAD_DIM
HGRN_DIM = HGRN_HEADS * HGRN_HEAD_DIM
N_EXPERTS = 64
TOP_K = 8
N_GROUPS = 8
TOPK_GROUPS = 4
EXPERT_DIM = D_MODEL // 4
SHARED_DIM = EXPERT_DIM
ROUTED_SCALE = 2.5
EXPERT_BLOCK = 128
LN_EPS = 1e-5
DEEPNORM_ALPHA = (2 * DEPTH) ** 0.25
DEEPNORM_BETA = (8 * DEPTH) ** -0.25

kernel_name = 'hybrid_gmlp_hgrn2_moe_deepnorm'


def layer_norm(x, g, b):
    xf = x.astype(jnp.float32)
    mu = jnp.mean(xf, axis=-1, keepdims=True)
    xc = xf - mu
    var = jnp.mean(xc * xc, axis=-1, keepdims=True)
    return (xc * lax.rsqrt(var + LN_EPS) * g.astype(jnp.float32) + b.astype(jnp.float32)).astype(x.dtype)


def gmlp_mixer(x, w_in, v_ln_g, v_ln_b, w_sp, b_sp, w_out):
    bsz, seq, _ = x.shape
    z = jax.nn.gelu(x @ w_in, approximate=False)
    u, v = jnp.split(z, 2, axis=-1)
    v = layer_norm(v, v_ln_g, v_ln_b)
    nblk = seq // GMLP_BLOCK
    v = v.reshape(bsz, nblk, GMLP_BLOCK, GMLP_HEADS, GMLP_HEAD_DIM)
    chunk_id = jnp.arange(GMLP_BLOCK) // CHUNK
    mask = chunk_id[:, None] >= chunk_id[None, :]
    w = jnp.where(mask[None], w_sp, 0.0)
    sv = jnp.einsum('hij,bnjhc->bnihc', w, v) + b_sp.T[None, None, :, :, None]
    y = u * sv.reshape(bsz, seq, GMLP_DIM)
    return y @ w_out


def chunk_gated_recurrence(q, k, v, log_f):
    bsz, seq, nh, dk = q.shape
    dv = v.shape[-1]
    n_chunks = seq // CHUNK

    def to_chunks(t):
        return t.reshape(bsz, n_chunks, CHUNK, nh, t.shape[-1]).transpose(1, 0, 3, 2, 4)

    causal = jnp.tril(jnp.ones((CHUNK, CHUNK), dtype=bool))[:, :, None]

    def step(state, inp):
        qc, kc, vc, lfc = inp
        b = jnp.cumsum(lfc, axis=2)
        b_last = b[:, :, -1]
        o_inter = jnp.einsum('bhtk,bhkv->bhtv', qc * jnp.exp(b), state)
        rel = jnp.where(causal, b[:, :, :, None, :] - b[:, :, None, :, :], -jnp.inf)
        scores = jnp.einsum('bhtk,bhtsk,bhsk->bhts', qc, jnp.exp(rel), kc)
        o_intra = jnp.einsum('bhts,bhsv->bhtv', scores, vc)
        k_dec = kc * jnp.exp(b_last[:, :, None, :] - b)
        new_state = jnp.exp(b_last)[..., None] * state + jnp.einsum('bhsk,bhsv->bhkv', k_dec, vc)
        return new_state, o_inter + o_intra

    state0 = jnp.zeros((bsz, nh, dk, dv), jnp.float32)
    _, o = lax.scan(step, state0, (to_chunks(q), to_chunks(k), to_chunks(v), to_chunks(log_f)))
    return o.transpose(1, 0, 3, 2, 4).reshape(bsz, seq, nh, dv)


def hgrn2_mixer(x, w_in, o_norm_g, w_out, lower_bound):
    bsz, seq, _ = x.shape
    q, f, i, g = jnp.split(x @ w_in, 4, axis=-1)
    hs = (bsz, seq, HGRN_HEADS, HGRN_HEAD_DIM)
    q = jax.nn.silu(q.astype(jnp.float32)).reshape(hs)
    fz = f.astype(jnp.float32).reshape(hs)
    lb = lower_bound.reshape(HGRN_HEADS, HGRN_HEAD_DIM)
    log_f = jnp.logaddexp(jnp.log(lb), jnp.log1p(-lb) + jax.nn.log_sigmoid(fz))
    k = (1.0 - lb) * jax.nn.sigmoid(-fz)
    o = chunk_gated_recurrence(q, k, i.astype(jnp.float32).reshape(hs), log_f)
    o = o * lax.rsqrt(jnp.mean(o * o, axis=-1, keepdims=True) + LN_EPS) * o_norm_g.astype(jnp.float32)
    o = o.reshape(bsz, seq, HGRN_DIM) * jax.nn.silu(g.astype(jnp.float32))
    return o.astype(x.dtype) @ w_out


def swiglu(x, w_gate, w_up, w_down):
    return (jax.nn.silu(x @ w_gate) * (x @ w_up)) @ w_down


def route(xt, w_router, e_bias):
    n_tok = xt.shape[0]
    scores = jax.nn.sigmoid((xt @ w_router).astype(jnp.float32))
    choice = scores + e_bias.astype(jnp.float32)
    grp = choice.reshape(n_tok, N_GROUPS, N_EXPERTS // N_GROUPS)
    grp_score = lax.top_k(grp, 2)[0].sum(-1)
    _, grp_idx = lax.top_k(grp_score, TOPK_GROUPS)
    grp_mask = jnp.any(grp_idx[:, :, None] == jnp.arange(N_GROUPS)[None, None, :], axis=1)
    allowed = jnp.repeat(grp_mask, N_EXPERTS // N_GROUPS, axis=1)
    _, eidx = lax.top_k(jnp.where(allowed, choice, -jnp.inf), TOP_K)
    gw = jnp.take_along_axis(scores, eidx, axis=1)
    gw = gw / jnp.sum(gw, axis=-1, keepdims=True) * ROUTED_SCALE
    return eidx, gw


def moe_ffn(x, w_router, e_bias, w_gate, w_up, w_down, ws_gate, ws_up, ws_down):
    bsz, seq, d = x.shape
    xt = x.reshape(-1, d)
    n_tok = xt.shape[0]
    eidx, gw = route(xt, w_router, e_bias)
    n_assign = n_tok * TOP_K
    e_flat = eidx.reshape(-1)
    tok_flat = jnp.arange(n_assign, dtype=jnp.int32) // TOP_K
    order = jnp.argsort(e_flat)
    e_sorted = e_flat[order]
    counts = jnp.bincount(e_flat, length=N_EXPERTS)
    padded = (counts + EXPERT_BLOCK - 1) // EXPERT_BLOCK * EXPERT_BLOCK
    pad_end = jnp.cumsum(padded)
    pad_start = pad_end - padded
    sort_start = jnp.cumsum(counts) - counts
    dest = pad_start[e_sorted] + jnp.arange(n_assign, dtype=jnp.int32) - sort_start[e_sorted]
    n_blocks = -(-n_assign // EXPERT_BLOCK) + N_EXPERTS
    n_rows = n_blocks * EXPERT_BLOCK
    row_tok = jnp.full((n_rows,), n_tok, jnp.int32).at[dest].set(tok_flat[order])
    row_w = jnp.zeros((n_rows,), jnp.float32).at[dest].set(gw.reshape(-1)[order])
    blk_start = jnp.arange(n_blocks, dtype=jnp.int32) * EXPERT_BLOCK
    blk_expert = jnp.minimum(jnp.searchsorted(pad_end, blk_start, side='right'), N_EXPERTS - 1)
    x_ext = jnp.concatenate([xt, jnp.zeros((1, d), xt.dtype)], axis=0)

    def expert_block(args):
        tok, wt, e = args
        xb = x_ext[tok]
        return swiglu(xb, w_gate[e], w_up[e], w_down[e]) * wt[:, None].astype(xb.dtype)

    rows = lax.map(expert_block, (row_tok.reshape(n_blocks, EXPERT_BLOCK),
                                  row_w.reshape(n_blocks, EXPERT_BLOCK), blk_expert))
    routed = jax.ops.segment_sum(rows.reshape(n_rows, d), row_tok, num_segments=n_tok + 1)[:n_tok]
    return (routed + swiglu(xt, ws_gate, ws_up, ws_down)).reshape(bsz, seq, d)


def setup_inputs(seed: int = 0) -> dict:
    key = jax.random.key(seed)
    ks = jax.random.split(key, 24)
    n_a = len(range(0, DEPTH, N_MIXERS))
    n_b = len(range(1, DEPTH, N_MIXERS))
    beta = DEEPNORM_BETA

    def nrm(k, shape, scale):
        return jax.random.normal(k, shape, jnp.float32) * scale

    return {
        'x': nrm(ks[0], (BATCH, SEQ, D_MODEL), 1.0),
        'ln_mix_g': 1.0 + nrm(ks[1], (DEPTH, D_MODEL), 0.02),
        'ln_mix_b': nrm(ks[2], (DEPTH, D_MODEL), 0.02),
        'ln_ffn_g': 1.0 + nrm(ks[3], (DEPTH, D_MODEL), 0.02),
        'ln_ffn_b': nrm(ks[4], (DEPTH, D_MODEL), 0.02),
        'gmlp_w_in': nrm(ks[5], (n_a, D_MODEL, 2 * GMLP_DIM), D_MODEL ** -0.5),
        'gmlp_v_ln_g': 1.0 + nrm(ks[6], (n_a, GMLP_DIM), 0.02),
        'gmlp_v_ln_b': nrm(ks[7], (n_a, GMLP_DIM), 0.02),
        'gmlp_w_sp': nrm(ks[8], (n_a, GMLP_HEADS, GMLP_BLOCK, GMLP_BLOCK), GMLP_BLOCK ** -0.5),
        'gmlp_b_sp': 1.0 + nrm(ks[9], (n_a, GMLP_HEADS, GMLP_BLOCK), 0.02),
        'gmlp_w_out': nrm(ks[10], (n_a, GMLP_DIM, D_MODEL), beta * GMLP_DIM ** -0.5),
        'hgrn_w_in': nrm(ks[11], (n_b, D_MODEL, 4 * HGRN_DIM), D_MODEL ** -0.5),
        'hgrn_o_norm_g': 1.0 + nrm(ks[12], (n_b, HGRN_HEAD_DIM), 0.02),
        'hgrn_w_out': nrm(ks[13], (n_b, HGRN_DIM, D_MODEL), beta * HGRN_DIM ** -0.5),
        'hgrn_lower_bounds': 1.0 + nrm(ks[14], (DEPTH, HGRN_DIM), 0.5),
        'moe_w_router': nrm(ks[15], (DEPTH, D_MODEL, N_EXPERTS), D_MODEL ** -0.5),
        'moe_e_bias': nrm(ks[16], (DEPTH, N_EXPERTS), 0.01),
        'moe_w_gate': nrm(ks[17], (DEPTH, N_EXPERTS, D_MODEL, EXPERT_DIM), D_MODEL ** -0.5),
        'moe_w_up': nrm(ks[18], (DEPTH, N_EXPERTS, D_MODEL, EXPERT_DIM), D_MODEL ** -0.5),
        'moe_w_down': nrm(ks[19], (DEPTH, N_EXPERTS, EXPERT_DIM, D_MODEL), beta * EXPERT_DIM ** -0.5),
        'moe_ws_gate': nrm(ks[20], (DEPTH, D_MODEL, SHARED_DIM), D_MODEL ** -0.5),
        'moe_ws_up': nrm(ks[21], (DEPTH, D_MODEL, SHARED_DIM), D_MODEL ** -0.5),
        'moe_ws_down': nrm(ks[22], (DEPTH, SHARED_DIM, D_MODEL), beta * SHARED_DIM ** -0.5),
    }


def reference(x, ln_mix_g, ln_mix_b, ln_ffn_g, ln_ffn_b,
              gmlp_w_in, gmlp_v_ln_g, gmlp_v_ln_b, gmlp_w_sp, gmlp_b_sp, gmlp_w_out,
              hgrn_w_in, hgrn_o_norm_g, hgrn_w_out, hgrn_lower_bounds,
              moe_w_router, moe_e_bias, moe_w_gate, moe_w_up, moe_w_down,
              moe_ws_gate, moe_ws_up, moe_ws_down):
    lb_soft = jax.nn.softmax(hgrn_lower_bounds.astype(jnp.float32), axis=0)
    lb_all = jnp.cumsum(lb_soft, axis=0) - lb_soft[0]
    for layer in range(DEPTH):
        j = layer // N_MIXERS
        if layer % N_MIXERS == 0:
            h = gmlp_mixer(x, gmlp_w_in[j], gmlp_v_ln_g[j], gmlp_v_ln_b[j],
                           gmlp_w_sp[j], gmlp_b_sp[j], gmlp_w_out[j])
        else:
            h = hgrn2_mixer(x, hgrn_w_in[j], hgrn_o_norm_g[j], hgrn_w_out[j], lb_all[layer])
        x = layer_norm(DEEPNORM_ALPHA * x + h, ln_mix_g[layer], ln_mix_b[layer])
        h = moe_ffn(x, moe_w_router[layer], moe_e_bias[layer], moe_w_gate[layer],
                    moe_w_up[layer], moe_w_down[layer], moe_ws_gate[layer],
                    moe_ws_up[layer], moe_ws_down[layer])
        x = layer_norm(DEEPNORM_ALPHA * x + h, ln_ffn_g[layer], ln_ffn_b[layer])
    return x
```

```python
import functools
import math

import numpy as np
import jax
import jax.numpy as jnp
from jax import lax
from jax.experimental import pallas as pl
from jax.experimental.pallas import tpu as pltpu

D_MODEL = 2048
N_HEADS = 16
HEAD_DIM = 128
GMLP_BLOCK = 128
STREAM_CHUNK = 64
N_EXPERTS = 64
TOP_K = 8
N_GROUPS = 8
GROUP_SIZE = N_EXPERTS // N_GROUPS
TOPK_GROUPS = 4
EXPERT_DIM = 512
ROUTED_SCALE = 2.5
LN_EPS = 1e-5
DEPTH = 2
ALPHA = (2 * DEPTH) ** 0.25

ROW_BLOCK = 128
REC_CHUNK = 128
REC_LEVELS = int(math.log2(REC_CHUNK))
VMEM_LIMIT = 56 * 1024 * 1024

_BF = jnp.bfloat16
_F32 = jnp.float32


def _cparams(sem):
    return pltpu.CompilerParams(dimension_semantics=sem, vmem_limit_bytes=VMEM_LIMIT)


def _sigmoid(x):
    return 1.0 / (1.0 + jnp.exp(-x))


def _silu(x):
    return x * _sigmoid(x)


def _layer_norm(x, g, b):
    mu = jnp.mean(x, axis=-1, keepdims=True)
    xc = x - mu
    var = jnp.mean(xc * xc, axis=-1, keepdims=True)
    return xc * lax.rsqrt(var + LN_EPS) * g + b


def _proj_body(epi, n_vec, n_out, x_ref, w_ref, *refs):
    vecs = [r[...] for r in refs[:n_vec]]
    outs = refs[n_vec:n_vec + n_out]
    acc = jnp.dot(x_ref[...], w_ref[...], preferred_element_type=_F32)
    res = epi(acc, *vecs)
    for o_ref, r in zip(outs, res):
        o_ref[...] = r.astype(o_ref.dtype)


def _proj(x, w, epi, out_dtypes, vecs=(), tm=512, tn=1024, name="proj"):
    m, k = x.shape
    n = w.shape[1]
    grid = (m // tm, n // tn)
    in_specs = [pl.BlockSpec((tm, k), lambda i, j: (i, 0)),
                pl.BlockSpec((k, tn), lambda i, j: (0, j))]
    in_specs += [pl.BlockSpec((1, tn), lambda i, j: (0, j)) for _ in vecs]
    out_specs = [pl.BlockSpec((tm, tn), lambda i, j: (i, j)) for _ in out_dtypes]
    out_shape = [jax.ShapeDtypeStruct((m, n), dt) for dt in out_dtypes]
    return pl.pallas_call(
        functools.partial(_proj_body, epi, len(vecs), len(out_dtypes)),
        grid=grid, in_specs=in_specs, out_specs=out_specs, out_shape=out_shape,
        compiler_params=_cparams(("parallel", "arbitrary")), name=name,
    )(x, w, *vecs)


def _epi_gelu(acc):
    return (0.5 * acc * (1.0 + lax.erf(acc * (1.0 / math.sqrt(2.0)))),)


def _epi_silu(acc):
    return (_silu(acc),)


def _epi_id(acc):
    return (acc,)


def _epi_forget(acc, log_lb, log1m_lb, one_m_lb):
    ls = jnp.minimum(acc, 0.0) - jnp.log1p(jnp.exp(-jnp.abs(acc)))
    c = log1m_lb + ls
    hi = jnp.maximum(log_lb, c)
    log_f = hi + jnp.log1p(jnp.exp(-jnp.abs(log_lb - c)))
    k = one_m_lb * _sigmoid(-acc)
    return log_f, k


def _sgu_body(u_ref, v_ref, g_ref, b_ref, wsp_ref, bsp_ref, y_ref, *, n_blk):
    vn = _layer_norm(v_ref[...].astype(_F32), g_ref[...], b_ref[...]).astype(_BF)
    ri = lax.broadcasted_iota(jnp.int32, (GMLP_BLOCK, GMLP_BLOCK), 0) // STREAM_CHUNK
    ci = lax.broadcasted_iota(jnp.int32, (GMLP_BLOCK, GMLP_BLOCK), 1) // STREAM_CHUNK
    causal = ri >= ci
    for h in range(N_HEADS):
        w = jnp.where(causal, wsp_ref[h], 0.0).astype(_BF)
        bias = bsp_ref[:, h:h + 1]
        cs = slice(h * HEAD_DIM, (h + 1) * HEAD_DIM)
        for n in range(n_blk):
            rs = slice(n * GMLP_BLOCK, (n + 1) * GMLP_BLOCK)
            sv = jnp.dot(w, vn[rs, cs], preferred_element_type=_F32) + bias
            y_ref[rs, cs] = (u_ref[rs, cs].astype(_F32) * sv).astype(_BF)


def _sgu(z, g, b, w_sp, bsp_t, tm=256):
    t = z.shape[0]
    return pl.pallas_call(
        functools.partial(_sgu_body, n_blk=tm // GMLP_BLOCK),
        grid=(t // tm,),
        in_specs=[pl.BlockSpec((tm, D_MODEL), lambda i: (i, 0)),
                  pl.BlockSpec((tm, D_MODEL), lambda i: (i, 1)),
                  pl.BlockSpec((1, D_MODEL), lambda i: (0, 0)),
                  pl.BlockSpec((1, D_MODEL), lambda i: (0, 0)),
                  pl.BlockSpec((N_HEADS, GMLP_BLOCK, GMLP_BLOCK), lambda i: (0, 0, 0)),
                  pl.BlockSpec((GMLP_BLOCK, N_HEADS), lambda i: (0, 0))],
        out_specs=pl.BlockSpec((tm, D_MODEL), lambda i: (i, 0)),
        out_shape=jax.ShapeDtypeStruct((t, D_MODEL), _BF),
        compiler_params=_cparams(("parallel",)), name="sgu",
    )(z, z, g, b, w_sp, bsp_t)


def _out_ln_body(y_ref, w_ref, x_ref, g_ref, b_ref, o_ref, ob_ref):
    h = jnp.dot(y_ref[...], w_ref[...], preferred_element_type=_F32)
    r = _layer_norm(ALPHA * x_ref[...] + h, g_ref[...], b_ref[...])
    o_ref[...] = r
    ob_ref[...] = r.astype(_BF)


def _out_ln(y, w, x_res, g, b, tm=256):
    t = y.shape[0]
    row = lambda i: (i, 0)
    fixed = lambda i: (0, 0)
    return pl.pallas_call(
        _out_ln_body,
        grid=(t // tm,),
        in_specs=[pl.BlockSpec((tm, D_MODEL), row),
                  pl.BlockSpec((D_MODEL, D_MODEL), fixed),
                  pl.BlockSpec((tm, D_MODEL), row),
                  pl.BlockSpec((1, D_MODEL), fixed),
                  pl.BlockSpec((1, D_MODEL), fixed)],
        out_specs=[pl.BlockSpec((tm, D_MODEL), row), pl.BlockSpec((tm, D_MODEL), row)],
        out_shape=[jax.ShapeDtypeStruct((t, D_MODEL), _F32),
                   jax.ShapeDtypeStruct((t, D_MODEL), _BF)],
        compiler_params=_cparams(("parallel",)), name="out_ln",
    )(y, w, x_res, g, b)


def _rec_constants():
    c = REC_CHUNK
    t = np.arange(c)[:, None]
    j = np.arange(c)[None, :]
    seg = [(j <= t)]
    sec = []
    blk = []
    for lv in range(REC_LEVELS):
        m = c >> (lv + 1)
        start = (t // (2 * m)) * (2 * m)
        mid = start + m - 1
        second = t > mid
        a = np.where(second, (j > mid) & (j <= t), (j > t) & (j <= mid))
        seg.append(a)
        sec.append(np.broadcast_to(second, (c, HEAD_DIM)))
        blk.append((t // (2 * m)) == (j // (2 * m)))
    seg = np.concatenate(seg, 0).astype(np.float32)
    sec = np.concatenate(sec, 0).astype(np.float32)
    blk = np.concatenate(blk, 0).astype(np.float32)
    return seg, sec, blk


def _rec_body(seg_ref, sec_ref, blk_ref, q_ref, lf_ref, k_ref, v_ref, gs_ref, gn_ref,
              o_ref, st_ref, *, hpb):
    c = REC_CHUNK

    @pl.when(pl.program_id(2) == 0)
    def _():
        st_ref[...] = jnp.zeros_like(st_ref)

    seg = seg_ref[...]
    eye = (lax.broadcasted_iota(jnp.int32, (c, c), 0) ==
           lax.broadcasted_iota(jnp.int32, (c, c), 1))
    ones_c = jnp.ones((c, HEAD_DIM), _BF)
    for h in range(hpb):
        cs = slice(h * HEAD_DIM, (h + 1) * HEAD_DIM)
        lf = lf_ref[:, cs]
        q = q_ref[:, cs].astype(_F32)
        k = k_ref[:, cs].astype(_F32)
        v = v_ref[:, cs]
        l1 = lf.astype(_BF)
        r1 = lf - l1.astype(_F32)
        l2 = r1.astype(_BF)
        l3 = (r1 - l2.astype(_F32)).astype(_BF)
        lf3 = jnp.concatenate([l1, l2, l3], axis=1)
        dd = jnp.dot(seg, lf3, preferred_element_type=_F32)
        dd = dd[:, :HEAD_DIM] + dd[:, HEAD_DIM:2 * HEAD_DIM] + dd[:, 2 * HEAD_DIM:]
        b = dd[:c]
        b_last = b[c - 1:c, :]
        st = st_ref[h]
        o = lax.dot_general((q * jnp.exp(b)).astype(_BF), st.astype(_BF),
                            (((1,), (1,)), ((), ())), preferred_element_type=_F32)
        scores = jnp.where(eye, jnp.sum(q * k, axis=1, keepdims=True), 0.0)
        for lv in range(REC_LEVELS):
            e = jnp.exp(dd[(lv + 1) * c:(lv + 2) * c])
            sec = sec_ref[lv * c:(lv + 1) * c, :]
            qt = (q * (e * sec)).astype(_BF)
            kt = (k * (e * (1.0 - sec))).astype(_BF)
            s = lax.dot_general(qt, kt, (((1,), (1,)), ((), ())), preferred_element_type=_F32)
            if lv == 0:
                scores = scores + s
            else:
                scores = scores + s * blk_ref[lv * c:(lv + 1) * c, :]
        o = o + jnp.dot(scores.astype(_BF), v, preferred_element_type=_F32)
        k_dec = (k * jnp.exp(b_last - b)).astype(_BF)
        upd = lax.dot_general(v, k_dec, (((0,), (0,)), ((), ())), preferred_element_type=_F32)
        st_ref[h] = jnp.exp(b_last) * st + upd
        ms = jnp.mean(o * o, axis=1, keepdims=True)
        o = o * lax.rsqrt(ms + LN_EPS) * gn_ref[...]
        o_ref[:, cs] = (o * gs_ref[:, cs].astype(_F32)).astype(_BF)


def _recurrence(q_s, log_f, k, v, g_s, g_norm, bsz, seq, hpb=4):
    c = REC_CHUNK
    seg, sec, blk = _rec_constants()
    seg = jnp.asarray(seg, _BF)
    sec = jnp.asarray(sec, _F32)
    blk = jnp.asarray(blk, _F32)
    w = hpb * HEAD_DIM
    n_c = seq // c
    fixed = lambda b, h, s: (0, 0)
    tile = lambda b, h, s: (b * n_c + s, h)
    return pl.pallas_call(
        functools.partial(_rec_body, hpb=hpb),
        grid=(bsz, N_HEADS // hpb, n_c),
        in_specs=[pl.BlockSpec(seg.shape, fixed), pl.BlockSpec(sec.shape, fixed),
                  pl.BlockSpec(blk.shape, fixed),
                  pl.BlockSpec((c, w), tile), pl.BlockSpec((c, w), tile),
                  pl.BlockSpec((c, w), tile), pl.BlockSpec((c, w), tile),
                  pl.BlockSpec((c, w), tile), pl.BlockSpec((1, HEAD_DIM), fixed)],
        out_specs=pl.BlockSpec((c, w), tile),
        out_shape=jax.ShapeDtypeStruct((bsz * seq, D_MODEL), _BF),
        scratch_shapes=[pltpu.VMEM((hpb, HEAD_DIM, HEAD_DIM), _F32)],
        compiler_params=_cparams(("parallel", "parallel", "arbitrary")), name="hgrn_rec",
    )(seg, sec, blk, q_s, log_f, k, v, g_s, g_norm)


def _router_body(x_ref, wr_ref, bias_ref, eidx_ref, gw_ref, rank_ref, cnt_ref, carry_ref, *, tm):
    @pl.when(pl.program_id(0) == 0)
    def _():
        carry_ref[...] = jnp.zeros_like(carry_ref)

    neg = -jnp.inf
    logits = lax.dot_general(wr_ref[...], x_ref[...], (((1,), (1,)), ((), ())),
                             precision=lax.Precision.HIGHEST,
                             preferred_element_type=_F32)
    scores = _sigmoid(logits)
    choice = scores + bias_ref[...]
    c3 = choice.reshape(N_GROUPS, GROUP_SIZE, tm)
    i3 = lax.broadcasted_iota(jnp.int32, c3.shape, 1)
    m1 = jnp.max(c3, axis=1, keepdims=True)
    first = jnp.min(jnp.where(c3 == m1, i3, GROUP_SIZE), axis=1, keepdims=True)
    m2 = jnp.max(jnp.where(i3 == first, neg, c3), axis=1, keepdims=True)
    gs = (m1 + m2).reshape(N_GROUPS, tm)
    ig = lax.broadcasted_iota(jnp.int32, gs.shape, 0)
    gsel = jnp.zeros(gs.shape, jnp.bool_)
    for _ in range(TOPK_GROUPS):
        m = jnp.max(gs, axis=0, keepdims=True)
        gi = jnp.min(jnp.where(gs == m, ig, N_GROUPS), axis=0, keepdims=True)
        hit = ig == gi
        gsel = gsel | hit
        gs = jnp.where(hit, neg, gs)
    allowed = jnp.broadcast_to(gsel.reshape(N_GROUPS, 1, tm), c3.shape).reshape(N_EXPERTS, tm)
    masked = jnp.where(allowed, choice, neg)
    ie = lax.broadcasted_iota(jnp.int32, masked.shape, 0)
    picked = jnp.zeros(masked.shape, _F32)
    hits, e_rows, w_rows = [], [], []
    for _ in range(TOP_K):
        m = jnp.max(masked, axis=0, keepdims=True)
        ei = jnp.min(jnp.where(masked == m, ie, N_EXPERTS), axis=0, keepdims=True)
        hit = ie == ei
        hits.append(hit)
        e_rows.append(ei)
        w_rows.append(jnp.sum(jnp.where(hit, scores, 0.0), axis=0, keepdims=True))
        picked = picked + hit.astype(_F32)
        masked = jnp.where(hit, neg, masked)
    gw = jnp.concatenate(w_rows, axis=0)
    gw = gw / jnp.sum(gw, axis=0, keepdims=True) * ROUTED_SCALE
    before = (lax.broadcasted_iota(jnp.int32, (tm, tm), 0) <
              lax.broadcasted_iota(jnp.int32, (tm, tm), 1)).astype(_BF)
    cum = jnp.dot(picked.astype(_BF), before, preferred_element_type=_F32) + carry_ref[...]
    r_rows = [jnp.sum(jnp.where(h, cum, 0.0), axis=0, keepdims=True) for h in hits]
    carry = carry_ref[...] + jnp.sum(picked, axis=1, keepdims=True)
    carry_ref[...] = carry
    eidx_ref[...] = jnp.concatenate(e_rows, axis=0)
    gw_ref[...] = gw
    rank_ref[...] = jnp.concatenate(r_rows, axis=0).astype(jnp.int32)
    cnt_ref[...] = carry.astype(jnp.int32)


def _router(x, wr_t, bias_col, tm=512):
    t = x.shape[0]
    tok = lambda i: (0, i)
    fixed = lambda i: (0, 0)
    return pl.pallas_call(
        functools.partial(_router_body, tm=tm),
        grid=(t // tm,),
        in_specs=[pl.BlockSpec((tm, D_MODEL), lambda i: (i, 0)),
                  pl.BlockSpec((N_EXPERTS, D_MODEL), fixed),
                  pl.BlockSpec((N_EXPERTS, 1), fixed)],
        out_specs=[pl.BlockSpec((TOP_K, tm), tok), pl.BlockSpec((TOP_K, tm), tok),
                   pl.BlockSpec((TOP_K, tm), tok), pl.BlockSpec((N_EXPERTS, 1), fixed)],
        out_shape=[jax.ShapeDtypeStruct((TOP_K, t), jnp.int32),
                   jax.ShapeDtypeStruct((TOP_K, t), _F32),
                   jax.ShapeDtypeStruct((TOP_K, t), jnp.int32),
                   jax.ShapeDtypeStruct((N_EXPERTS, 1), jnp.int32)],
        scratch_shapes=[pltpu.VMEM((N_EXPERTS, 1), _F32)],
        compiler_params=_cparams(("arbitrary",)), name="router",
    )(x, wr_t, bias_col)


def _expert_body(be_ref, nv_ref, tok_ref, slot_ref, w_ref, x_hbm, wg_ref, wu_ref, wd_ref,
                 y_hbm, xbuf, ybuf, sem):
    n_slots = y_hbm.shape[0] - ROW_BLOCK

    @pl.when(pl.program_id(0) == 0)
    def _():
        ybuf[...] = jnp.zeros_like(ybuf)
        pad_rows = pltpu.make_async_copy(ybuf, y_hbm.at[pl.ds(n_slots, ROW_BLOCK), :], sem.at[1])
        pad_rows.start()
        pad_rows.wait()

    @pl.when(nv_ref[pl.program_id(0)] > 0)
    def _():
        def gather(r, carry):
            pltpu.make_async_copy(x_hbm.at[pl.ds(tok_ref[0, 0, r], 1), :],
                                  xbuf.at[pl.ds(r, 1), :], sem.at[0]).start()
            return carry
        lax.fori_loop(0, ROW_BLOCK, gather, 0)
        pltpu.make_async_copy(x_hbm.at[pl.ds(0, ROW_BLOCK), :], xbuf, sem.at[0]).wait()
        xb = xbuf[...].astype(_BF)
        hg = jnp.dot(xb, wg_ref[0], preferred_element_type=_F32)
        hu = jnp.dot(xb, wu_ref[0], preferred_element_type=_F32)
        hh = (_silu(hg) * hu).astype(_BF)
        ybuf[...] = jnp.dot(hh, wd_ref[0], preferred_element_type=_F32) * w_ref[...]

        def scatter(r, carry):
            pltpu.make_async_copy(ybuf.at[pl.ds(r, 1), :],
                                  y_hbm.at[pl.ds(slot_ref[0, 0, r], 1), :], sem.at[1]).start()
            return carry
        lax.fori_loop(0, ROW_BLOCK, scatter, 0)
        pltpu.make_async_copy(ybuf, y_hbm.at[pl.ds(0, ROW_BLOCK), :], sem.at[1]).wait()


def _experts(blk_expert, n_used, row_tok, row_slot, row_w, x, wg, wu, wd, n_slots):
    n_blocks = row_tok.shape[0]
    smem_blk = pl.BlockSpec((1, 1, ROW_BLOCK), lambda i, be, nu: (i, 0, 0),
                            memory_space=pltpu.SMEM)
    grid_spec = pltpu.PrefetchScalarGridSpec(
        num_scalar_prefetch=2, grid=(n_blocks,),
        in_specs=[smem_blk, smem_blk,
                  pl.BlockSpec((ROW_BLOCK, 1), lambda i, be, nu: (i, 0)),
                  pl.BlockSpec(memory_space=pl.ANY),
                  pl.BlockSpec((1, D_MODEL, EXPERT_DIM), lambda i, be, nu: (be[i], 0, 0)),
                  pl.BlockSpec((1, D_MODEL, EXPERT_DIM), lambda i, be, nu: (be[i], 0, 0)),
                  pl.BlockSpec((1, EXPERT_DIM, D_MODEL), lambda i, be, nu: (be[i], 0, 0))],
        out_specs=pl.BlockSpec(memory_space=pl.ANY),
        scratch_shapes=[pltpu.VMEM((ROW_BLOCK, D_MODEL), _F32),
                        pltpu.VMEM((ROW_BLOCK, D_MODEL), _F32),
                        pltpu.SemaphoreType.DMA((2,))])
    return pl.pallas_call(
        _expert_body, grid_spec=grid_spec,
        out_shape=jax.ShapeDtypeStruct((n_slots, D_MODEL), _F32),
        compiler_params=_cparams(("arbitrary",)), name="experts",
    )(blk_expert, n_used, row_tok, row_slot, row_w, x, wg, wu, wd)


def _combine_body(*refs):
    y_refs = refs[:TOP_K]
    x_ref, xb_ref, wg_ref, wu_ref, wd_ref, g_ref, b_ref, o_ref, ob_ref = refs[TOP_K:]
    xb = xb_ref[...]
    hg = jnp.dot(xb, wg_ref[...], preferred_element_type=_F32)
    hu = jnp.dot(xb, wu_ref[...], preferred_element_type=_F32)
    acc = jnp.dot((_silu(hg) * hu).astype(_BF), wd_ref[...], preferred_element_type=_F32)
    for y_ref in y_refs:
        acc = acc + y_ref[...]
    r = _layer_norm(ALPHA * x_ref[...] + acc, g_ref[...], b_ref[...])
    o_ref[...] = r
    ob_ref[...] = r.astype(_BF)


def _combine(y2, x, xb, wg, wu, wd, g, b, tm=256):
    t = x.shape[0]
    nt = t // tm
    row = lambda i: (i, 0)
    fixed = lambda i: (0, 0)
    y_specs = [pl.BlockSpec((tm, D_MODEL), functools.partial(lambda i, k: (k * nt + i, 0), k=k))
               for k in range(TOP_K)]
    return pl.pallas_call(
        _combine_body,
        grid=(nt,),
        in_specs=y_specs + [pl.BlockSpec((tm, D_MODEL), row), pl.BlockSpec((tm, D_MODEL), row),
                            pl.BlockSpec((D_MODEL, EXPERT_DIM), fixed),
                            pl.BlockSpec((D_MODEL, EXPERT_DIM), fixed),
                            pl.BlockSpec((EXPERT_DIM, D_MODEL), fixed),
                            pl.BlockSpec((1, D_MODEL), fixed), pl.BlockSpec((1, D_MODEL), fixed)],
        out_specs=[pl.BlockSpec((tm, D_MODEL), row), pl.BlockSpec((tm, D_MODEL), row)],
        out_shape=[jax.ShapeDtypeStruct((t, D_MODEL), _F32),
                   jax.ShapeDtypeStruct((t, D_MODEL), _BF)],
        compiler_params=_cparams(("parallel",)), name="combine",
    )(*([y2] * TOP_K), x, xb, wg, wu, wd, g, b)


def _moe(x, xb, w_router, e_bias, w_gate, w_up, w_down, ws_gate, ws_up, ws_down, g, b):
    t = x.shape[0]
    eidx, gw, rank, counts = _router(x, w_router.T, e_bias.reshape(N_EXPERTS, 1))
    counts = counts.reshape(N_EXPERTS)
    padded = (counts + ROW_BLOCK - 1) // ROW_BLOCK * ROW_BLOCK
    pad_end = jnp.cumsum(padded)
    pad_start = pad_end - padded
    n_blocks = t * TOP_K // ROW_BLOCK + N_EXPERTS
    n_rows = n_blocks * ROW_BLOCK
    dest = (pad_start[eidx] + rank).reshape(-1)
    tok = jnp.broadcast_to(jnp.arange(t, dtype=jnp.int32)[None, :], (TOP_K, t)).reshape(-1)
    slot = jnp.arange(TOP_K * t, dtype=jnp.int32)
    row_tok = jnp.zeros((n_rows,), jnp.int32).at[dest].set(tok)
    pad_slot = TOP_K * t + jnp.arange(n_rows, dtype=jnp.int32) % ROW_BLOCK
    row_slot = pad_slot.at[dest].set(slot)
    row_w = jnp.zeros((n_rows,), _F32).at[dest].set(gw.reshape(-1))
    blk_start = jnp.arange(n_blocks, dtype=jnp.int32) * ROW_BLOCK
    blk_expert = jnp.minimum(jnp.searchsorted(pad_end, blk_start, side="right"),
                             N_EXPERTS - 1).astype(jnp.int32)
    n_valid = jnp.clip((pad_start + counts)[blk_expert] - blk_start, 0, ROW_BLOCK)
    n_valid = jnp.where(blk_start < pad_end[-1], n_valid, 0).astype(jnp.int32)
    y2 = _experts(blk_expert, n_valid,
                  row_tok.reshape(n_blocks, 1, ROW_BLOCK), row_slot.reshape(n_blocks, 1, ROW_BLOCK),
                  row_w.reshape(n_rows, 1), x,
                  w_gate.astype(_BF), w_up.astype(_BF), w_down.astype(_BF),
                  TOP_K * t + ROW_BLOCK)
    return _combine(y2, x, xb, ws_gate.astype(_BF), ws_up.astype(_BF), ws_down.astype(_BF),
                    g.reshape(1, -1), b.reshape(1, -1))


def kernel(x, ln_mix_g, ln_mix_b, ln_ffn_g, ln_ffn_b, gmlp_w_in, gmlp_v_ln_g, gmlp_v_ln_b, gmlp_w_sp, gmlp_b_sp, gmlp_w_out, hgrn_w_in, hgrn_o_norm_g, hgrn_w_out, hgrn_lower_bounds, moe_w_router, moe_e_bias, moe_w_gate, moe_w_up, moe_w_down, moe_ws_gate, moe_ws_up, moe_ws_down):
    bsz, seq, d = x.shape
    t = bsz * seq
    row = lambda a: a.reshape(1, -1)
    xf = x.reshape(t, d)
    xb = xf.astype(_BF)

    z = _proj(xb, gmlp_w_in[0].astype(_BF), _epi_gelu, [_BF], name="gmlp_in")[0]
    y = _sgu(z, row(gmlp_v_ln_g[0]), row(gmlp_v_ln_b[0]), gmlp_w_sp[0], gmlp_b_sp[0].T)
    xf, xb = _out_ln(y, gmlp_w_out[0].astype(_BF), xf, row(ln_mix_g[0]), row(ln_mix_b[0]))
    xf, xb = _moe(xf, xb, moe_w_router[0], moe_e_bias[0], moe_w_gate[0], moe_w_up[0],
                  moe_w_down[0], moe_ws_gate[0], moe_ws_up[0], moe_ws_down[0],
                  ln_ffn_g[0], ln_ffn_b[0])

    lb_soft = jax.nn.softmax(hgrn_lower_bounds.astype(_F32), axis=0)
    lb = (jnp.cumsum(lb_soft, axis=0) - lb_soft[0])[1]
    w_in = hgrn_w_in[0].astype(_BF)
    wq, wf, wi, wgt = (w_in[:, j * d:(j + 1) * d] for j in range(4))
    q_s = _proj(xb, wq, _epi_silu, [_BF], name="hgrn_q")[0]
    log_f, kk = _proj(xb, wf, _epi_forget, [_F32, _BF],
                      vecs=(row(jnp.log(lb)), row(jnp.log1p(-lb)), row(1.0 - lb)), name="hgrn_f")
    vv = _proj(xb, wi, _epi_id, [_BF], name="hgrn_i")[0]
    g_s = _proj(xb, wgt, _epi_silu, [_BF], name="hgrn_g")[0]
    o = _recurrence(q_s, log_f, kk, vv, g_s, row(hgrn_o_norm_g[0]), bsz, seq)
    xf, xb = _out_ln(o, hgrn_w_out[0].astype(_BF), xf, row(ln_mix_g[1]), row(ln_mix_b[1]))
    xf, xb = _moe(xf, xb, moe_w_router[1], moe_e_bias[1], moe_w_gate[1], moe_w_up[1],
                  moe_w_down[1], moe_ws_gate[1], moe_ws_up[1], moe_ws_down[1],
                  ln_ffn_g[1], ln_ffn_b[1])
    return xf.reshape(bsz, seq, d)
```

```python
import functools
import math

import numpy as np
import jax
import jax.numpy as jnp
from jax import lax
from jax.experimental import pallas as pl
from jax.experimental.pallas import tpu as pltpu

D_MODEL = 2048
N_HEADS = 16
HEAD_DIM = 128
GMLP_BLOCK = 128
STREAM_CHUNK = 64
N_EXPERTS = 64
TOP_K = 8
N_GROUPS = 8
GROUP_SIZE = N_EXPERTS // N_GROUPS
TOPK_GROUPS = 4
EXPERT_DIM = 512
ROUTED_SCALE = 2.5
LN_EPS = 1e-5
DEPTH = 2
ALPHA = (2 * DEPTH) ** 0.25

ROW_BLOCK = 128
REC_CHUNK = 128
REC_LEVELS = int(math.log2(REC_CHUNK))
VMEM_LIMIT = 56 * 1024 * 1024

_BF = jnp.bfloat16
_F32 = jnp.float32


def _cparams(sem):
    return pltpu.CompilerParams(dimension_semantics=sem, vmem_limit_bytes=VMEM_LIMIT)


def _sigmoid(x):
    return 1.0 / (1.0 + jnp.exp(-x))


def _silu(x):
    return x * _sigmoid(x)


def _layer_norm(x, g, b):
    mu = jnp.mean(x, axis=-1, keepdims=True)
    xc = x - mu
    var = jnp.mean(xc * xc, axis=-1, keepdims=True)
    return xc * lax.rsqrt(var + LN_EPS) * g + b


CAST_ROWS = 256


def _cast_rows(src, dst):
    def step(i, carry):
        rows = pl.ds(pl.multiple_of(i * CAST_ROWS, CAST_ROWS), CAST_ROWS)
        dst[rows, :] = src[rows, :].astype(_BF)
        return carry
    lax.fori_loop(0, src.shape[0] // CAST_ROWS, step, 0)


def _proj_body(epi, n_vec, n_out, x_ref, w_ref, *refs):
    vecs = [r[...] for r in refs[:n_vec]]
    outs = refs[n_vec:n_vec + n_out]
    wb_ref = refs[n_vec + n_out]

    @pl.when(pl.program_id(1) == 0)
    def _():
        _cast_rows(w_ref, wb_ref)

    acc = jnp.dot(x_ref[...], wb_ref[...], preferred_element_type=_F32)
    res = epi(acc, *vecs)
    for o_ref, r in zip(outs, res):
        o_ref[...] = r.astype(o_ref.dtype)


def _proj(x, w, epi, out_dtypes, n, col0=0, vecs=(), tm=512, tn=1024, name="proj"):
    m, k = x.shape
    grid = (n // tn, m // tm)
    c0 = col0 // tn
    in_specs = [pl.BlockSpec((tm, k), lambda j, i: (i, 0)),
                pl.BlockSpec((k, tn), lambda j, i: (0, c0 + j))]
    in_specs += [pl.BlockSpec((1, tn), lambda j, i: (0, j)) for _ in vecs]
    out_specs = [pl.BlockSpec((tm, tn), lambda j, i: (i, j)) for _ in out_dtypes]
    out_shape = [jax.ShapeDtypeStruct((m, n), dt) for dt in out_dtypes]
    return pl.pallas_call(
        functools.partial(_proj_body, epi, len(vecs), len(out_dtypes)),
        grid=grid, in_specs=in_specs, out_specs=out_specs, out_shape=out_shape,
        scratch_shapes=[pltpu.VMEM((k, tn), _BF)],
        compiler_params=_cparams(("arbitrary", "arbitrary")), name=name,
    )(x, w, *vecs)


def _epi_gelu(acc):
    return (0.5 * acc * (1.0 + lax.erf(acc * (1.0 / math.sqrt(2.0)))),)


def _epi_silu(acc):
    return (_silu(acc),)


def _epi_id(acc):
    return (acc,)


def _epi_forget(acc, log_lb, log1m_lb, one_m_lb):
    ls = jnp.minimum(acc, 0.0) - jnp.log1p(jnp.exp(-jnp.abs(acc)))
    c = log1m_lb + ls
    hi = jnp.maximum(log_lb, c)
    log_f = hi + jnp.log1p(jnp.exp(-jnp.abs(log_lb - c)))
    k = one_m_lb * _sigmoid(-acc)
    return log_f, k


def _sgu_body(u_ref, v_ref, g_ref, b_ref, wsp_ref, bsp_ref, y_ref, *, n_blk):
    vn = _layer_norm(v_ref[...].astype(_F32), g_ref[...], b_ref[...]).astype(_BF)
    ri = lax.broadcasted_iota(jnp.int32, (GMLP_BLOCK, GMLP_BLOCK), 0) // STREAM_CHUNK
    ci = lax.broadcasted_iota(jnp.int32, (GMLP_BLOCK, GMLP_BLOCK), 1) // STREAM_CHUNK
    causal = ri >= ci
    for h in range(N_HEADS):
        w = jnp.where(causal, wsp_ref[h], 0.0).astype(_BF)
        bias = bsp_ref[:, h:h + 1]
        cs = slice(h * HEAD_DIM, (h + 1) * HEAD_DIM)
        for n in range(n_blk):
            rs = slice(n * GMLP_BLOCK, (n + 1) * GMLP_BLOCK)
            sv = jnp.dot(w, vn[rs, cs], preferred_element_type=_F32) + bias
            y_ref[rs, cs] = (u_ref[rs, cs].astype(_F32) * sv).astype(_BF)


def _sgu(z, g, b, w_sp, bsp_t, tm=256):
    t = z.shape[0]
    return pl.pallas_call(
        functools.partial(_sgu_body, n_blk=tm // GMLP_BLOCK),
        grid=(t // tm,),
        in_specs=[pl.BlockSpec((tm, D_MODEL), lambda i: (i, 0)),
                  pl.BlockSpec((tm, D_MODEL), lambda i: (i, 1)),
                  pl.BlockSpec((1, D_MODEL), lambda i: (0, 0)),
                  pl.BlockSpec((1, D_MODEL), lambda i: (0, 0)),
                  pl.BlockSpec((N_HEADS, GMLP_BLOCK, GMLP_BLOCK), lambda i: (0, 0, 0)),
                  pl.BlockSpec((GMLP_BLOCK, N_HEADS), lambda i: (0, 0))],
        out_specs=pl.BlockSpec((tm, D_MODEL), lambda i: (i, 0)),
        out_shape=jax.ShapeDtypeStruct((t, D_MODEL), _BF),
        compiler_params=_cparams(("parallel",)), name="sgu",
    )(z, z, g, b, w_sp, bsp_t)


def _out_ln_body(y_ref, w_ref, x_ref, g_ref, b_ref, o_ref, ob_ref):
    h = jnp.dot(y_ref[...], w_ref[...], preferred_element_type=_F32)
    r = _layer_norm(ALPHA * x_ref[...] + h, g_ref[...], b_ref[...])
    o_ref[...] = r
    ob_ref[...] = r.astype(_BF)


def _out_ln(y, w, x_res, g, b, tm=256):
    t = y.shape[0]
    row = lambda i: (i, 0)
    fixed = lambda i: (0, 0)
    return pl.pallas_call(
        _out_ln_body,
        grid=(t // tm,),
        in_specs=[pl.BlockSpec((tm, D_MODEL), row),
                  pl.BlockSpec((D_MODEL, D_MODEL), fixed),
                  pl.BlockSpec((tm, D_MODEL), row),
                  pl.BlockSpec((1, D_MODEL), fixed),
                  pl.BlockSpec((1, D_MODEL), fixed)],
        out_specs=[pl.BlockSpec((tm, D_MODEL), row), pl.BlockSpec((tm, D_MODEL), row)],
        out_shape=[jax.ShapeDtypeStruct((t, D_MODEL), _F32),
                   jax.ShapeDtypeStruct((t, D_MODEL), _BF)],
        compiler_params=_cparams(("parallel",)), name="out_ln",
    )(y, w, x_res, g, b)


def _rec_constants():
    c = REC_CHUNK
    t = np.arange(c)[:, None]
    j = np.arange(c)[None, :]
    seg = [(j <= t)]
    sec = []
    blk = []
    for lv in range(REC_LEVELS):
        m = c >> (lv + 1)
        start = (t // (2 * m)) * (2 * m)
        mid = start + m - 1
        second = t > mid
        a = np.where(second, (j > mid) & (j <= t), (j > t) & (j <= mid))
        seg.append(a)
        sec.append(np.broadcast_to(second, (c, HEAD_DIM)))
        blk.append((t // (2 * m)) == (j // (2 * m)))
    seg = np.concatenate(seg, 0).astype(np.float32)
    sec = np.concatenate(sec, 0).astype(np.float32)
    blk = np.concatenate(blk, 0).astype(np.float32)
    return seg, sec, blk


def _rec_body(seg_ref, sec_ref, blk_ref, q_ref, lf_ref, k_ref, v_ref, gs_ref, gn_ref,
              o_ref, st_ref, *, hpb):
    c = REC_CHUNK

    @pl.when(pl.program_id(2) == 0)
    def _():
        st_ref[...] = jnp.zeros_like(st_ref)

    seg = seg_ref[...]
    eye = (lax.broadcasted_iota(jnp.int32, (c, c), 0) ==
           lax.broadcasted_iota(jnp.int32, (c, c), 1))
    ones_c = jnp.ones((c, HEAD_DIM), _BF)
    for h in range(hpb):
        cs = slice(h * HEAD_DIM, (h + 1) * HEAD_DIM)
        lf = lf_ref[:, cs]
        q = q_ref[:, cs].astype(_F32)
        k = k_ref[:, cs].astype(_F32)
        v = v_ref[:, cs]
        l1 = lf.astype(_BF)
        r1 = lf - l1.astype(_F32)
        l2 = r1.astype(_BF)
        l3 = (r1 - l2.astype(_F32)).astype(_BF)
        lf3 = jnp.concatenate([l1, l2, l3], axis=1)
        dd = jnp.dot(seg, lf3, preferred_element_type=_F32)
        dd = dd[:, :HEAD_DIM] + dd[:, HEAD_DIM:2 * HEAD_DIM] + dd[:, 2 * HEAD_DIM:]
        b = dd[:c]
        b_last = b[c - 1:c, :]
        st = st_ref[h]
        o = lax.dot_general((q * jnp.exp(b)).astype(_BF), st.astype(_BF),
                            (((1,), (1,)), ((), ())), preferred_element_type=_F32)
        scores = jnp.where(eye, jnp.sum(q * k, axis=1, keepdims=True), 0.0)
        for lv in range(REC_LEVELS):
            e = jnp.exp(dd[(lv + 1) * c:(lv + 2) * c])
            sec = sec_ref[lv * c:(lv + 1) * c, :]
            qt = (q * (e * sec)).astype(_BF)
            kt = (k * (e * (1.0 - sec))).astype(_BF)
            s = lax.dot_general(qt, kt, (((1,), (1,)), ((), ())), preferred_element_type=_F32)
            if lv == 0:
                scores = scores + s
            else:
                scores = scores + s * blk_ref[lv * c:(lv + 1) * c, :]
        o = o + jnp.dot(scores.astype(_BF), v, preferred_element_type=_F32)
        k_dec = (k * jnp.exp(b_last - b)).astype(_BF)
        upd = lax.dot_general(v, k_dec, (((0,), (0,)), ((), ())), preferred_element_type=_F32)
        st_ref[h] = jnp.exp(b_last) * st + upd
        ms = jnp.mean(o * o, axis=1, keepdims=True)
        o = o * lax.rsqrt(ms + LN_EPS) * gn_ref[...]
        o_ref[:, cs] = (o * gs_ref[:, cs].astype(_F32)).astype(_BF)


def _recurrence(q_s, log_f, k, v, g_s, g_norm, bsz, seq, hpb=4):
    c = REC_CHUNK
    seg, sec, blk = _rec_constants()
    seg = jnp.asarray(seg, _BF)
    sec = jnp.asarray(sec, _F32)
    blk = jnp.asarray(blk, _F32)
    w = hpb * HEAD_DIM
    n_c = seq // c
    fixed = lambda b, h, s: (0, 0)
    tile = lambda b, h, s: (b * n_c + s, h)
    return pl.pallas_call(
        functools.partial(_rec_body, hpb=hpb),
        grid=(bsz, N_HEADS // hpb, n_c),
        in_specs=[pl.BlockSpec(seg.shape, fixed), pl.BlockSpec(sec.shape, fixed),
                  pl.BlockSpec(blk.shape, fixed),
                  pl.BlockSpec((c, w), tile), pl.BlockSpec((c, w), tile),
                  pl.BlockSpec((c, w), tile), pl.BlockSpec((c, w), tile),
                  pl.BlockSpec((c, w), tile), pl.BlockSpec((1, HEAD_DIM), fixed)],
        out_specs=pl.BlockSpec((c, w), tile),
        out_shape=jax.ShapeDtypeStruct((bsz * seq, D_MODEL), _BF),
        scratch_shapes=[pltpu.VMEM((hpb, HEAD_DIM, HEAD_DIM), _F32)],
        compiler_params=_cparams(("parallel", "parallel", "arbitrary")), name="hgrn_rec",
    )(seg, sec, blk, q_s, log_f, k, v, g_s, g_norm)


def _router_body(x_ref, wr_ref, bias_ref, eidx_ref, gw_ref, rank_ref, cnt_ref, carry_ref, *, tm):
    @pl.when(pl.program_id(0) == 0)
    def _():
        carry_ref[...] = jnp.zeros_like(carry_ref)

    neg = -jnp.inf
    logits = lax.dot_general(wr_ref[...], x_ref[...], (((1,), (1,)), ((), ())),
                             precision=lax.Precision.HIGHEST,
                             preferred_element_type=_F32)
    scores = _sigmoid(logits)
    choice = scores + bias_ref[...]
    c3 = choice.reshape(N_GROUPS, GROUP_SIZE, tm)
    i3 = lax.broadcasted_iota(jnp.int32, c3.shape, 1)
    m1 = jnp.max(c3, axis=1, keepdims=True)
    first = jnp.min(jnp.where(c3 == m1, i3, GROUP_SIZE), axis=1, keepdims=True)
    m2 = jnp.max(jnp.where(i3 == first, neg, c3), axis=1, keepdims=True)
    gs = (m1 + m2).reshape(N_GROUPS, tm)
    ig = lax.broadcasted_iota(jnp.int32, gs.shape, 0)
    gsel = jnp.zeros(gs.shape, jnp.bool_)
    for _ in range(TOPK_GROUPS):
        m = jnp.max(gs, axis=0, keepdims=True)
        gi = jnp.min(jnp.where(gs == m, ig, N_GROUPS), axis=0, keepdims=True)
        hit = ig == gi
        gsel = gsel | hit
        gs = jnp.where(hit, neg, gs)
    allowed = jnp.broadcast_to(gsel.reshape(N_GROUPS, 1, tm), c3.shape).reshape(N_EXPERTS, tm)
    masked = jnp.where(allowed, choice, neg)
    ie = lax.broadcasted_iota(jnp.int32, masked.shape, 0)
    picked = jnp.zeros(masked.shape, _F32)
    hits, e_rows, w_rows = [], [], []
    for _ in range(TOP_K):
        m = jnp.max(masked, axis=0, keepdims=True)
        ei = jnp.min(jnp.where(masked == m, ie, N_EXPERTS), axis=0, keepdims=True)
        hit = ie == ei
        hits.append(hit)
        e_rows.append(ei)
        w_rows.append(jnp.sum(jnp.where(hit, scores, 0.0), axis=0, keepdims=True))
        picked = picked + hit.astype(_F32)
        masked = jnp.where(hit, neg, masked)
    gw = jnp.concatenate(w_rows, axis=0)
    gw = gw / jnp.sum(gw, axis=0, keepdims=True) * ROUTED_SCALE
    before = (lax.broadcasted_iota(jnp.int32, (tm, tm), 0) <
              lax.broadcasted_iota(jnp.int32, (tm, tm), 1)).astype(_BF)
    cum = jnp.dot(picked.astype(_BF), before, preferred_element_type=_F32) + carry_ref[...]
    r_rows = [jnp.sum(jnp.where(h, cum, 0.0), axis=0, keepdims=True) for h in hits]
    carry = carry_ref[...] + jnp.sum(picked, axis=1, keepdims=True)
    carry_ref[...] = carry
    eidx_ref[...] = jnp.concatenate(e_rows, axis=0)
    gw_ref[...] = gw
    rank_ref[...] = jnp.concatenate(r_rows, axis=0).astype(jnp.int32)
    cnt_ref[...] = carry.astype(jnp.int32)


def _router(x, wr_t, bias_col, tm=512):
    t = x.shape[0]
    tok = lambda i: (0, i)
    fixed = lambda i: (0, 0)
    return pl.pallas_call(
        functools.partial(_router_body, tm=tm),
        grid=(t // tm,),
        in_specs=[pl.BlockSpec((tm, D_MODEL), lambda i: (i, 0)),
                  pl.BlockSpec((N_EXPERTS, D_MODEL), fixed),
                  pl.BlockSpec((N_EXPERTS, 1), fixed)],
        out_specs=[pl.BlockSpec((TOP_K, tm), tok), pl.BlockSpec((TOP_K, tm), tok),
                   pl.BlockSpec((TOP_K, tm), tok), pl.BlockSpec((N_EXPERTS, 1), fixed)],
        out_shape=[jax.ShapeDtypeStruct((TOP_K, t), jnp.int32),
                   jax.ShapeDtypeStruct((TOP_K, t), _F32),
                   jax.ShapeDtypeStruct((TOP_K, t), jnp.int32),
                   jax.ShapeDtypeStruct((N_EXPERTS, 1), jnp.int32)],
        scratch_shapes=[pltpu.VMEM((N_EXPERTS, 1), _F32)],
        compiler_params=_cparams(("arbitrary",)), name="router",
    )(x, wr_t, bias_col)


def _dispatch_body(pend_ref, dest_ref, x_hbm, xs_hbm, zbuf, sem, *, tm, n_tiles):
    i = pl.program_id(0)

    def pad_copy(e):
        start = pl.multiple_of(pend_ref[e] - ROW_BLOCK, ROW_BLOCK)
        return pltpu.make_async_copy(zbuf, xs_hbm.at[pl.ds(start, ROW_BLOCK), :], sem.at[2])

    def has_rows(e):
        return pend_ref[e] > jnp.where(e > 0, pend_ref[jnp.maximum(e - 1, 0)], 0)

    @pl.when(i == 0)
    def _():
        zbuf[...] = jnp.zeros_like(zbuf)

        def start(e, carry):
            @pl.when(has_rows(e))
            def _():
                pad_copy(e).start()
            return carry
        lax.fori_loop(0, N_EXPERTS, start, 0)

        def wait(e, carry):
            @pl.when(has_rows(e))
            def _():
                pad_copy(e).wait()
            return carry
        lax.fori_loop(0, N_EXPERTS, wait, 0)

        def tail_copy(b):
            start = pl.multiple_of(b * ROW_BLOCK, ROW_BLOCK)
            return pltpu.make_async_copy(zbuf, xs_hbm.at[pl.ds(start, ROW_BLOCK), :], sem.at[2])

        first_unused = pend_ref[N_EXPERTS - 1] // ROW_BLOCK
        n_blocks = xs_hbm.shape[0] // ROW_BLOCK

        def tail_start(b, carry):
            tail_copy(b).start()
            return carry
        lax.fori_loop(first_unused, n_blocks, tail_start, 0)

        def tail_wait(b, carry):
            tail_copy(b).wait()
            return carry
        lax.fori_loop(first_unused, n_blocks, tail_wait, 0)

    slot = i % 2
    base = i * tm

    def push(t, carry):
        for k in range(TOP_K):
            pltpu.make_async_copy(x_hbm.at[pl.ds(base + t, 1), :],
                                  xs_hbm.at[pl.ds(dest_ref[k, t], 1), :], sem.at[slot]).start()
        return carry
    lax.fori_loop(0, tm, push, 0)

    def wait_tile(s):
        pltpu.make_async_copy(x_hbm.at[pl.ds(0, TOP_K * tm), :],
                              xs_hbm.at[pl.ds(0, TOP_K * tm), :], sem.at[s]).wait()

    @pl.when(i > 0)
    def _():
        wait_tile(1 - slot)

    @pl.when(i == n_tiles - 1)
    def _():
        wait_tile(slot)


def _dispatch(pad_end, dest, x, n_rows, tm=256):
    t = x.shape[0]
    n_tiles = t // tm
    grid_spec = pltpu.PrefetchScalarGridSpec(
        num_scalar_prefetch=1, grid=(n_tiles,),
        in_specs=[pl.BlockSpec((TOP_K, tm), lambda i, pe: (0, i), memory_space=pltpu.SMEM),
                  pl.BlockSpec(memory_space=pl.ANY)],
        out_specs=pl.BlockSpec(memory_space=pl.ANY),
        scratch_shapes=[pltpu.VMEM((ROW_BLOCK, D_MODEL), _F32),
                        pltpu.SemaphoreType.DMA((3,))])
    return pl.pallas_call(
        functools.partial(_dispatch_body, tm=tm, n_tiles=n_tiles), grid_spec=grid_spec,
        out_shape=jax.ShapeDtypeStruct((n_rows, D_MODEL), _F32),
        compiler_params=_cparams(("arbitrary",)), name="dispatch",
    )(pad_end, dest, x)


def _expert_body(be_ref, nv_ref, nu_ref, x_ref, wg_ref, wu_ref, wd_ref, y_ref,
                 wg_b, wu_b, wd_b):
    i = pl.program_id(0)
    n_valid = nv_ref[i]
    new_expert = (i == 0) | (be_ref[i] != be_ref[jnp.maximum(i - 1, 0)])

    @pl.when(new_expert & (n_valid > 0))
    def _():
        _cast_rows(wg_ref.at[0], wg_b)
        _cast_rows(wu_ref.at[0], wu_b)
        _cast_rows(wd_ref.at[0], wd_b)

    @pl.when(n_valid > 0)
    def _():
        xb = x_ref[...].astype(_BF)
        hg = jnp.dot(xb, wg_b[...], preferred_element_type=_F32)
        hu = jnp.dot(xb, wu_b[...], preferred_element_type=_F32)
        hh = (_silu(hg) * hu).astype(_BF)
        y_ref[...] = jnp.dot(hh, wd_b[...], preferred_element_type=_F32)

    @pl.when(n_valid == 0)
    def _():
        y_ref[...] = jnp.zeros_like(y_ref)


def _experts(blk_expert, n_valid, n_used, xs, wg, wu, wd):
    n_rows = xs.shape[0]
    n_blocks = n_rows // ROW_BLOCK
    x_map = lambda i, be, nv, nu: (jnp.minimum(i, nu[0] - 1), 0)
    w_map = lambda i, be, nv, nu: (be[i], 0, 0)
    grid_spec = pltpu.PrefetchScalarGridSpec(
        num_scalar_prefetch=3, grid=(n_blocks,),
        in_specs=[pl.BlockSpec((ROW_BLOCK, D_MODEL), x_map),
                  pl.BlockSpec((1, D_MODEL, EXPERT_DIM), w_map),
                  pl.BlockSpec((1, D_MODEL, EXPERT_DIM), w_map),
                  pl.BlockSpec((1, EXPERT_DIM, D_MODEL), w_map)],
        out_specs=pl.BlockSpec((ROW_BLOCK, D_MODEL), lambda i, be, nv, nu: (i, 0)),
        scratch_shapes=[pltpu.VMEM((D_MODEL, EXPERT_DIM), _BF),
                        pltpu.VMEM((D_MODEL, EXPERT_DIM), _BF),
                        pltpu.VMEM((EXPERT_DIM, D_MODEL), _BF)])
    return pl.pallas_call(
        _expert_body, grid_spec=grid_spec,
        out_shape=jax.ShapeDtypeStruct((n_rows, D_MODEL), _F32),
        compiler_params=_cparams(("arbitrary",)), name="experts",
    )(blk_expert, n_valid, n_used, xs, wg, wu, wd)


def _combine_body(dcur_ref, dnxt_ref, y_hbm, gw_ref, x_ref, xb_ref, wg_ref, wu_ref, wd_ref,
                  g_ref, b_ref, o_ref, ob_ref, buf, sem, *, tm, n_tiles):
    i = pl.program_id(0)
    slot = i % 2

    def gather(d_ref, s):
        def row(t, carry):
            for k in range(TOP_K):
                pltpu.make_async_copy(y_hbm.at[pl.ds(d_ref[k, t], 1), :],
                                      buf.at[s, pl.ds(k * tm + t, 1), :], sem.at[s]).start()
            return carry
        lax.fori_loop(0, tm, row, 0)

    @pl.when(i == 0)
    def _():
        gather(dcur_ref, 0)

    @pl.when(i + 1 < n_tiles)
    def _():
        gather(dnxt_ref, 1 - slot)

    xb = xb_ref[...]
    hg = jnp.dot(xb, wg_ref[...], preferred_element_type=_F32)
    hu = jnp.dot(xb, wu_ref[...], preferred_element_type=_F32)
    acc = jnp.dot((_silu(hg) * hu).astype(_BF), wd_ref[...], preferred_element_type=_F32)
    pltpu.make_async_copy(y_hbm.at[pl.ds(0, TOP_K * tm), :], buf.at[slot], sem.at[slot]).wait()
    gw = gw_ref[...]
    for k in range(TOP_K):
        acc = acc + gw[:, k:k + 1] * buf[slot, pl.ds(k * tm, tm), :]
    r = _layer_norm(ALPHA * x_ref[...] + acc, g_ref[...], b_ref[...])
    o_ref[...] = r
    ob_ref[...] = r.astype(_BF)


def _combine(dest, y, gw_t, x, xb, wg, wu, wd, g, b, tm=128):
    t = x.shape[0]
    nt = t // tm
    row = lambda i: (i, 0)
    fixed = lambda i: (0, 0)
    return pl.pallas_call(
        functools.partial(_combine_body, tm=tm, n_tiles=nt),
        grid=(nt,),
        in_specs=[pl.BlockSpec((TOP_K, tm), lambda i: (0, i), memory_space=pltpu.SMEM),
                  pl.BlockSpec((TOP_K, tm), lambda i: (0, jnp.minimum(i + 1, nt - 1)),
                               memory_space=pltpu.SMEM),
                  pl.BlockSpec(memory_space=pl.ANY),
                  pl.BlockSpec((tm, TOP_K), row),
                  pl.BlockSpec((tm, D_MODEL), row), pl.BlockSpec((tm, D_MODEL), row),
                  pl.BlockSpec((D_MODEL, EXPERT_DIM), fixed),
                  pl.BlockSpec((D_MODEL, EXPERT_DIM), fixed),
                  pl.BlockSpec((EXPERT_DIM, D_MODEL), fixed),
                  pl.BlockSpec((1, D_MODEL), fixed), pl.BlockSpec((1, D_MODEL), fixed)],
        out_specs=[pl.BlockSpec((tm, D_MODEL), row), pl.BlockSpec((tm, D_MODEL), row)],
        out_shape=[jax.ShapeDtypeStruct((t, D_MODEL), _F32),
                   jax.ShapeDtypeStruct((t, D_MODEL), _BF)],
        scratch_shapes=[pltpu.VMEM((2, TOP_K * tm, D_MODEL), _F32),
                        pltpu.SemaphoreType.DMA((2,))],
        compiler_params=_cparams(("arbitrary",)), name="combine",
    )(dest, dest, y, gw_t, x, xb, wg, wu, wd, g, b)


def _moe(x, xb, w_router, e_bias, w_gate, w_up, w_down, ws_gate, ws_up, ws_down, g, b):
    t = x.shape[0]
    eidx, gw, rank, counts = _router(x, w_router.T, e_bias.reshape(N_EXPERTS, 1))
    counts = counts.reshape(N_EXPERTS)
    padded = (counts + ROW_BLOCK - 1) // ROW_BLOCK * ROW_BLOCK
    pad_end = jnp.cumsum(padded)
    pad_start = pad_end - padded
    n_blocks = t * TOP_K // ROW_BLOCK + N_EXPERTS
    expert_ids = jnp.arange(N_EXPERTS, dtype=jnp.int32)
    dest = rank + jnp.sum(jnp.where(eidx[None] == expert_ids[:, None, None],
                                    pad_start[:, None, None], 0), axis=0)
    blk_start = jnp.arange(n_blocks, dtype=jnp.int32) * ROW_BLOCK
    blk_expert = jnp.minimum(jnp.sum(pad_end[None, :] <= blk_start[:, None], axis=1),
                             N_EXPERTS - 1).astype(jnp.int32)
    n_valid = jnp.clip((pad_start + counts)[blk_expert] - blk_start, 0, ROW_BLOCK)
    n_valid = jnp.where(blk_start < pad_end[-1], n_valid, 0).astype(jnp.int32)
    n_used = (pad_end[-1:] // ROW_BLOCK).astype(jnp.int32)
    xs = _dispatch(pad_end.astype(jnp.int32), dest, x, n_blocks * ROW_BLOCK)
    y = _experts(blk_expert, n_valid, n_used, xs, w_gate, w_up, w_down)
    return _combine(dest, y, gw.T, x, xb, ws_gate.astype(_BF), ws_up.astype(_BF),
                    ws_down.astype(_BF), g.reshape(1, -1), b.reshape(1, -1))


def kernel(x, ln_mix_g, ln_mix_b, ln_ffn_g, ln_ffn_b, gmlp_w_in, gmlp_v_ln_g, gmlp_v_ln_b, gmlp_w_sp, gmlp_b_sp, gmlp_w_out, hgrn_w_in, hgrn_o_norm_g, hgrn_w_out, hgrn_lower_bounds, moe_w_router, moe_e_bias, moe_w_gate, moe_w_up, moe_w_down, moe_ws_gate, moe_ws_up, moe_ws_down):
    bsz, seq, d = x.shape
    t = bsz * seq
    row = lambda a: a.reshape(1, -1)
    xf = x.reshape(t, d)
    xb = xf.astype(_BF)

    z = _proj(xb, gmlp_w_in[0], _epi_gelu, [_BF], 2 * d, name="gmlp_in")[0]
    y = _sgu(z, row(gmlp_v_ln_g[0]), row(gmlp_v_ln_b[0]), gmlp_w_sp[0], gmlp_b_sp[0].T)
    xf, xb = _out_ln(y, gmlp_w_out[0].astype(_BF), xf, row(ln_mix_g[0]), row(ln_mix_b[0]))
    xf, xb = _moe(xf, xb, moe_w_router[0], moe_e_bias[0], moe_w_gate[0], moe_w_up[0],
                  moe_w_down[0], moe_ws_gate[0], moe_ws_up[0], moe_ws_down[0],
                  ln_ffn_g[0], ln_ffn_b[0])

    lb_soft = jax.nn.softmax(hgrn_lower_bounds.astype(_F32), axis=0)
    lb = (jnp.cumsum(lb_soft, axis=0) - lb_soft[0])[1]
    w_in = hgrn_w_in[0]
    q_s = _proj(xb, w_in, _epi_silu, [_BF], d, col0=0, name="hgrn_q")[0]
    log_f, kk = _proj(xb, w_in, _epi_forget, [_F32, _BF], d, col0=d,
                      vecs=(row(jnp.log(lb)), row(jnp.log1p(-lb)), row(1.0 - lb)), name="hgrn_f")
    vv = _proj(xb, w_in, _epi_id, [_BF], d, col0=2 * d, name="hgrn_i")[0]
    g_s = _proj(xb, w_in, _epi_silu, [_BF], d, col0=3 * d, name="hgrn_g")[0]
    o = _recurrence(q_s, log_f, kk, vv, g_s, row(hgrn_o_norm_g[0]), bsz, seq)
    xf, xb = _out_ln(o, hgrn_w_out[0].astype(_BF), xf, row(ln_mix_g[1]), row(ln_mix_b[1]))
    xf, xb = _moe(xf, xb, moe_w_router[1], moe_e_bias[1], moe_w_gate[1], moe_w_up[1],
                  moe_w_down[1], moe_ws_gate[1], moe_ws_up[1], moe_ws_down[1],
                  ln_ffn_g[1], ln_ffn_b[1])
    return xf.reshape(bsz, seq, d)
```

```python
import functools
import math

import numpy as np
import jax
import jax.numpy as jnp
from jax import lax
from jax.experimental import pallas as pl
from jax.experimental.pallas import tpu as pltpu

D_MODEL = 2048
N_HEADS = 16
HEAD_DIM = 128
GMLP_BLOCK = 128
STREAM_CHUNK = 64
N_EXPERTS = 64
TOP_K = 8
N_GROUPS = 8
GROUP_SIZE = N_EXPERTS // N_GROUPS
TOPK_GROUPS = 4
EXPERT_DIM = 512
ROUTED_SCALE = 2.5
LN_EPS = 1e-5
DEPTH = 2
ALPHA = (2 * DEPTH) ** 0.25

ROW_BLOCK = 256
PACK_ROWS = D_MODEL // HEAD_DIM
REC_CHUNK = 128
REC_LEVELS = int(math.log2(REC_CHUNK))
VMEM_LIMIT = 56 * 1024 * 1024

_BF = jnp.bfloat16
_F32 = jnp.float32


def _cparams(sem):
    return pltpu.CompilerParams(dimension_semantics=sem, vmem_limit_bytes=VMEM_LIMIT)


def _sigmoid(x):
    return 1.0 / (1.0 + jnp.exp(-x))


def _silu(x):
    return x * _sigmoid(x)


def _pack_store(ref, base, m, val):
    for j in range(PACK_ROWS):
        ref[pl.ds(base + j, m, stride=PACK_ROWS), :] = val[:, j * HEAD_DIM:(j + 1) * HEAD_DIM]


def _unpack_load(ref, base, m):
    return jnp.concatenate([ref[pl.ds(base + j, m, stride=PACK_ROWS), :]
                            for j in range(PACK_ROWS)], axis=1)


def _layer_norm(x, g, b):
    mu = jnp.mean(x, axis=-1, keepdims=True)
    xc = x - mu
    var = jnp.mean(xc * xc, axis=-1, keepdims=True)
    return xc * lax.rsqrt(var + LN_EPS) * g + b


CAST_ROWS = 256


def _cast_rows(src, dst):
    def step(i, carry):
        rows = pl.ds(pl.multiple_of(i * CAST_ROWS, CAST_ROWS), CAST_ROWS)
        dst[rows, :] = src[rows, :].astype(_BF)
        return carry
    lax.fori_loop(0, src.shape[0] // CAST_ROWS, step, 0)


def _proj_body(epi, n_vec, n_out, x_ref, w_ref, *refs):
    vecs = [r[...] for r in refs[:n_vec]]
    outs = refs[n_vec:n_vec + n_out]
    wb_ref = refs[n_vec + n_out]

    @pl.when(pl.program_id(1) == 0)
    def _():
        _cast_rows(w_ref, wb_ref)

    acc = jnp.dot(x_ref[...], wb_ref[...], preferred_element_type=_F32)
    res = epi(acc, *vecs)
    for o_ref, r in zip(outs, res):
        o_ref[...] = r.astype(o_ref.dtype)


def _proj(x, w, epi, out_dtypes, n, col0=0, vecs=(), tm=512, tn=1024, name="proj"):
    m, k = x.shape
    grid = (n // tn, m // tm)
    c0 = col0 // tn
    in_specs = [pl.BlockSpec((tm, k), lambda j, i: (i, 0)),
                pl.BlockSpec((k, tn), lambda j, i: (0, c0 + j))]
    in_specs += [pl.BlockSpec((1, tn), lambda j, i: (0, j)) for _ in vecs]
    out_specs = [pl.BlockSpec((tm, tn), lambda j, i: (i, j)) for _ in out_dtypes]
    out_shape = [jax.ShapeDtypeStruct((m, n), dt) for dt in out_dtypes]
    return pl.pallas_call(
        functools.partial(_proj_body, epi, len(vecs), len(out_dtypes)),
        grid=grid, in_specs=in_specs, out_specs=out_specs, out_shape=out_shape,
        scratch_shapes=[pltpu.VMEM((k, tn), _BF)],
        compiler_params=_cparams(("arbitrary", "arbitrary")), name=name,
    )(x, w, *vecs)


def _epi_gelu(acc):
    return (0.5 * acc * (1.0 + lax.erf(acc * (1.0 / math.sqrt(2.0)))),)


def _epi_silu(acc):
    return (_silu(acc),)


def _epi_id(acc):
    return (acc,)


def _epi_forget(acc, log_lb, log1m_lb, one_m_lb):
    ls = jnp.minimum(acc, 0.0) - jnp.log1p(jnp.exp(-jnp.abs(acc)))
    c = log1m_lb + ls
    hi = jnp.maximum(log_lb, c)
    log_f = hi + jnp.log1p(jnp.exp(-jnp.abs(log_lb - c)))
    k = one_m_lb * _sigmoid(-acc)
    return log_f, k


def _sgu_body(u_ref, v_ref, g_ref, b_ref, wsp_ref, bsp_ref, y_ref, *, n_blk):
    vn = _layer_norm(v_ref[...].astype(_F32), g_ref[...], b_ref[...]).astype(_BF)
    ri = lax.broadcasted_iota(jnp.int32, (GMLP_BLOCK, GMLP_BLOCK), 0) // STREAM_CHUNK
    ci = lax.broadcasted_iota(jnp.int32, (GMLP_BLOCK, GMLP_BLOCK), 1) // STREAM_CHUNK
    causal = ri >= ci
    for h in range(N_HEADS):
        w = jnp.where(causal, wsp_ref[h], 0.0).astype(_BF)
        bias = bsp_ref[:, h:h + 1]
        cs = slice(h * HEAD_DIM, (h + 1) * HEAD_DIM)
        for n in range(n_blk):
            rs = slice(n * GMLP_BLOCK, (n + 1) * GMLP_BLOCK)
            sv = jnp.dot(w, vn[rs, cs], preferred_element_type=_F32) + bias
            y_ref[rs, cs] = (u_ref[rs, cs].astype(_F32) * sv).astype(_BF)


def _sgu(z, g, b, w_sp, bsp_t, tm=256):
    t = z.shape[0]
    return pl.pallas_call(
        functools.partial(_sgu_body, n_blk=tm // GMLP_BLOCK),
        grid=(t // tm,),
        in_specs=[pl.BlockSpec((tm, D_MODEL), lambda i: (i, 0)),
                  pl.BlockSpec((tm, D_MODEL), lambda i: (i, 1)),
                  pl.BlockSpec((1, D_MODEL), lambda i: (0, 0)),
                  pl.BlockSpec((1, D_MODEL), lambda i: (0, 0)),
                  pl.BlockSpec((N_HEADS, GMLP_BLOCK, GMLP_BLOCK), lambda i: (0, 0, 0)),
                  pl.BlockSpec((GMLP_BLOCK, N_HEADS), lambda i: (0, 0))],
        out_specs=pl.BlockSpec((tm, D_MODEL), lambda i: (i, 0)),
        out_shape=jax.ShapeDtypeStruct((t, D_MODEL), _BF),
        compiler_params=_cparams(("parallel",)), name="sgu",
    )(z, z, g, b, w_sp, bsp_t)


def _out_ln_body(y_ref, w_ref, x_ref, g_ref, b_ref, o_ref, ob_ref, op_ref):
    h = jnp.dot(y_ref[...], w_ref[...], preferred_element_type=_F32)
    r = _layer_norm(ALPHA * x_ref[...] + h, g_ref[...], b_ref[...])
    o_ref[...] = r
    ob_ref[...] = r.astype(_BF)
    _pack_store(op_ref, 0, r.shape[0], r)


def _out_ln(y, w, x_res, g, b, tm=256):
    t = y.shape[0]
    row = lambda i: (i, 0)
    fixed = lambda i: (0, 0)
    return pl.pallas_call(
        _out_ln_body,
        grid=(t // tm,),
        in_specs=[pl.BlockSpec((tm, D_MODEL), row),
                  pl.BlockSpec((D_MODEL, D_MODEL), fixed),
                  pl.BlockSpec((tm, D_MODEL), row),
                  pl.BlockSpec((1, D_MODEL), fixed),
                  pl.BlockSpec((1, D_MODEL), fixed)],
        out_specs=[pl.BlockSpec((tm, D_MODEL), row), pl.BlockSpec((tm, D_MODEL), row),
                   pl.BlockSpec((tm * PACK_ROWS, HEAD_DIM), row)],
        out_shape=[jax.ShapeDtypeStruct((t, D_MODEL), _F32),
                   jax.ShapeDtypeStruct((t, D_MODEL), _BF),
                   jax.ShapeDtypeStruct((t * PACK_ROWS, HEAD_DIM), _F32)],
        compiler_params=_cparams(("parallel",)), name="out_ln",
    )(y, w, x_res, g, b)


def _rec_constants():
    c = REC_CHUNK
    t = np.arange(c)[:, None]
    j = np.arange(c)[None, :]
    seg = [(j <= t)]
    sec = []
    blk = []
    for lv in range(REC_LEVELS):
        m = c >> (lv + 1)
        start = (t // (2 * m)) * (2 * m)
        mid = start + m - 1
        second = t > mid
        a = np.where(second, (j > mid) & (j <= t), (j > t) & (j <= mid))
        seg.append(a)
        sec.append(np.broadcast_to(second, (c, HEAD_DIM)))
        blk.append((t // (2 * m)) == (j // (2 * m)))
    seg = np.concatenate(seg, 0).astype(np.float32)
    sec = np.concatenate(sec, 0).astype(np.float32)
    blk = np.concatenate(blk, 0).astype(np.float32)
    return seg, sec, blk


def _rec_body(seg_ref, sec_ref, blk_ref, q_ref, lf_ref, k_ref, v_ref, gs_ref, gn_ref,
              o_ref, st_ref, *, hpb):
    c = REC_CHUNK

    @pl.when(pl.program_id(2) == 0)
    def _():
        st_ref[...] = jnp.zeros_like(st_ref)

    seg = seg_ref[...]
    eye = (lax.broadcasted_iota(jnp.int32, (c, c), 0) ==
           lax.broadcasted_iota(jnp.int32, (c, c), 1))
    ones_c = jnp.ones((c, HEAD_DIM), _BF)
    for h in range(hpb):
        cs = slice(h * HEAD_DIM, (h + 1) * HEAD_DIM)
        lf = lf_ref[:, cs]
        q = q_ref[:, cs].astype(_F32)
        k = k_ref[:, cs].astype(_F32)
        v = v_ref[:, cs]
        l1 = lf.astype(_BF)
        r1 = lf - l1.astype(_F32)
        l2 = r1.astype(_BF)
        l3 = (r1 - l2.astype(_F32)).astype(_BF)
        lf3 = jnp.concatenate([l1, l2, l3], axis=1)
        dd = jnp.dot(seg, lf3, preferred_element_type=_F32)
        dd = dd[:, :HEAD_DIM] + dd[:, HEAD_DIM:2 * HEAD_DIM] + dd[:, 2 * HEAD_DIM:]
        b = dd[:c]
        b_last = b[c - 1:c, :]
        st = st_ref[h]
        o = lax.dot_general((q * jnp.exp(b)).astype(_BF), st.astype(_BF),
                            (((1,), (1,)), ((), ())), preferred_element_type=_F32)
        scores = jnp.where(eye, jnp.sum(q * k, axis=1, keepdims=True), 0.0)
        for lv in range(REC_LEVELS):
            e = jnp.exp(dd[(lv + 1) * c:(lv + 2) * c])
            sec = sec_ref[lv * c:(lv + 1) * c, :]
            qt = (q * (e * sec)).astype(_BF)
            kt = (k * (e * (1.0 - sec))).astype(_BF)
            s = lax.dot_general(qt, kt, (((1,), (1,)), ((), ())), preferred_element_type=_F32)
            if lv == 0:
                scores = scores + s
            else:
                scores = scores + s * blk_ref[lv * c:(lv + 1) * c, :]
        o = o + jnp.dot(scores.astype(_BF), v, preferred_element_type=_F32)
        k_dec = (k * jnp.exp(b_last - b)).astype(_BF)
        upd = lax.dot_general(v, k_dec, (((0,), (0,)), ((), ())), preferred_element_type=_F32)
        st_ref[h] = jnp.exp(b_last) * st + upd
        ms = jnp.mean(o * o, axis=1, keepdims=True)
        o = o * lax.rsqrt(ms + LN_EPS) * gn_ref[...]
        o_ref[:, cs] = (o * gs_ref[:, cs].astype(_F32)).astype(_BF)


def _recurrence(q_s, log_f, k, v, g_s, g_norm, bsz, seq, hpb=4):
    c = REC_CHUNK
    seg, sec, blk = _rec_constants()
    seg = jnp.asarray(seg, _BF)
    sec = jnp.asarray(sec, _F32)
    blk = jnp.asarray(blk, _F32)
    w = hpb * HEAD_DIM
    n_c = seq // c
    fixed = lambda b, h, s: (0, 0)
    tile = lambda b, h, s: (b * n_c + s, h)
    return pl.pallas_call(
        functools.partial(_rec_body, hpb=hpb),
        grid=(bsz, N_HEADS // hpb, n_c),
        in_specs=[pl.BlockSpec(seg.shape, fixed), pl.BlockSpec(sec.shape, fixed),
                  pl.BlockSpec(blk.shape, fixed),
                  pl.BlockSpec((c, w), tile), pl.BlockSpec((c, w), tile),
                  pl.BlockSpec((c, w), tile), pl.BlockSpec((c, w), tile),
                  pl.BlockSpec((c, w), tile), pl.BlockSpec((1, HEAD_DIM), fixed)],
        out_specs=pl.BlockSpec((c, w), tile),
        out_shape=jax.ShapeDtypeStruct((bsz * seq, D_MODEL), _BF),
        scratch_shapes=[pltpu.VMEM((hpb, HEAD_DIM, HEAD_DIM), _F32)],
        compiler_params=_cparams(("parallel", "parallel", "arbitrary")), name="hgrn_rec",
    )(seg, sec, blk, q_s, log_f, k, v, g_s, g_norm)


def _router_body(x_ref, wr_ref, bias_ref, eidx_ref, gw_ref, rank_ref, cnt_ref, carry_ref, *, tm):
    @pl.when(pl.program_id(0) == 0)
    def _():
        carry_ref[...] = jnp.zeros_like(carry_ref)

    neg = -jnp.inf
    logits = lax.dot_general(wr_ref[...], x_ref[...], (((1,), (1,)), ((), ())),
                             precision=lax.Precision.HIGHEST,
                             preferred_element_type=_F32)
    scores = _sigmoid(logits)
    choice = scores + bias_ref[...]
    c3 = choice.reshape(N_GROUPS, GROUP_SIZE, tm)
    i3 = lax.broadcasted_iota(jnp.int32, c3.shape, 1)
    m1 = jnp.max(c3, axis=1, keepdims=True)
    first = jnp.min(jnp.where(c3 == m1, i3, GROUP_SIZE), axis=1, keepdims=True)
    m2 = jnp.max(jnp.where(i3 == first, neg, c3), axis=1, keepdims=True)
    gs = (m1 + m2).reshape(N_GROUPS, tm)
    ig = lax.broadcasted_iota(jnp.int32, gs.shape, 0)
    gsel = jnp.zeros(gs.shape, jnp.bool_)
    for _ in range(TOPK_GROUPS):
        m = jnp.max(gs, axis=0, keepdims=True)
        gi = jnp.min(jnp.where(gs == m, ig, N_GROUPS), axis=0, keepdims=True)
        hit = ig == gi
        gsel = gsel | hit
        gs = jnp.where(hit, neg, gs)
    allowed = jnp.broadcast_to(gsel.reshape(N_GROUPS, 1, tm), c3.shape).reshape(N_EXPERTS, tm)
    masked = jnp.where(allowed, choice, neg)
    ie = lax.broadcasted_iota(jnp.int32, masked.shape, 0)
    picked = jnp.zeros(masked.shape, _F32)
    hits, e_rows, w_rows = [], [], []
    for _ in range(TOP_K):
        m = jnp.max(masked, axis=0, keepdims=True)
        ei = jnp.min(jnp.where(masked == m, ie, N_EXPERTS), axis=0, keepdims=True)
        hit = ie == ei
        hits.append(hit)
        e_rows.append(ei)
        w_rows.append(jnp.sum(jnp.where(hit, scores, 0.0), axis=0, keepdims=True))
        picked = picked + hit.astype(_F32)
        masked = jnp.where(hit, neg, masked)
    gw = jnp.concatenate(w_rows, axis=0)
    gw = gw / jnp.sum(gw, axis=0, keepdims=True) * ROUTED_SCALE
    before = (lax.broadcasted_iota(jnp.int32, (tm, tm), 0) <
              lax.broadcasted_iota(jnp.int32, (tm, tm), 1)).astype(_BF)
    cum = jnp.dot(picked.astype(_BF), before, preferred_element_type=_F32) + carry_ref[...]
    r_rows = [jnp.sum(jnp.where(h, cum, 0.0), axis=0, keepdims=True) for h in hits]
    carry = carry_ref[...] + jnp.sum(picked, axis=1, keepdims=True)
    carry_ref[...] = carry
    eidx_ref[...] = jnp.concatenate(e_rows, axis=0)
    gw_ref[...] = gw
    rank_ref[...] = jnp.concatenate(r_rows, axis=0).astype(jnp.int32)
    cnt_ref[...] = carry.astype(jnp.int32)


def _router(x, wr_t, bias_col, tm=512):
    t = x.shape[0]
    tok = lambda i: (0, i)
    fixed = lambda i: (0, 0)
    return pl.pallas_call(
        functools.partial(_router_body, tm=tm),
        grid=(t // tm,),
        in_specs=[pl.BlockSpec((tm, D_MODEL), lambda i: (i, 0)),
                  pl.BlockSpec((N_EXPERTS, D_MODEL), fixed),
                  pl.BlockSpec((N_EXPERTS, 1), fixed)],
        out_specs=[pl.BlockSpec((TOP_K, tm), tok), pl.BlockSpec((TOP_K, tm), tok),
                   pl.BlockSpec((TOP_K, tm), tok), pl.BlockSpec((N_EXPERTS, 1), fixed)],
        out_shape=[jax.ShapeDtypeStruct((TOP_K, t), jnp.int32),
                   jax.ShapeDtypeStruct((TOP_K, t), _F32),
                   jax.ShapeDtypeStruct((TOP_K, t), jnp.int32),
                   jax.ShapeDtypeStruct((N_EXPERTS, 1), jnp.int32)],
        scratch_shapes=[pltpu.VMEM((N_EXPERTS, 1), _F32)],
        compiler_params=_cparams(("arbitrary",)), name="router",
    )(x, wr_t, bias_col)


BLOCK_WORDS = ROW_BLOCK * PACK_ROWS


def _dispatch_body(pend_ref, dest_ref, xp_ref, xs_hbm, zbuf, sem, *, tm):
    i = pl.program_id(0)

    def block_copy(b):
        start = pl.multiple_of(b * BLOCK_WORDS, BLOCK_WORDS)
        return pltpu.make_async_copy(zbuf, xs_hbm.at[pl.ds(start, BLOCK_WORDS), :], sem.at[1])

    def pad_copy(e):
        return block_copy(pend_ref[e] // ROW_BLOCK - 1)

    def has_rows(e):
        return pend_ref[e] > jnp.where(e > 0, pend_ref[jnp.maximum(e - 1, 0)], 0)

    @pl.when(i == 0)
    def _():
        zbuf[...] = jnp.zeros_like(zbuf)

        def start(e, carry):
            @pl.when(has_rows(e))
            def _():
                pad_copy(e).start()
            return carry
        lax.fori_loop(0, N_EXPERTS, start, 0)

        def wait(e, carry):
            @pl.when(has_rows(e))
            def _():
                pad_copy(e).wait()
            return carry
        lax.fori_loop(0, N_EXPERTS, wait, 0)

        first_unused = pend_ref[N_EXPERTS - 1] // ROW_BLOCK
        n_blocks = xs_hbm.shape[0] // BLOCK_WORDS

        def tail_start(b, carry):
            block_copy(b).start()
            return carry
        lax.fori_loop(first_unused, n_blocks, tail_start, 0)

        def tail_wait(b, carry):
            block_copy(b).wait()
            return carry
        lax.fori_loop(first_unused, n_blocks, tail_wait, 0)

    def push(t, carry):
        src = xp_ref.at[pl.ds(pl.multiple_of(t * PACK_ROWS, PACK_ROWS), PACK_ROWS), :]
        for k in range(TOP_K):
            row = pl.multiple_of(dest_ref[k, t] * PACK_ROWS, PACK_ROWS)
            pltpu.make_async_copy(src, xs_hbm.at[pl.ds(row, PACK_ROWS), :], sem.at[0]).start()
        return carry
    lax.fori_loop(0, tm, push, 0)
    for k in range(TOP_K):
        pltpu.make_async_copy(xp_ref, xs_hbm.at[pl.ds(0, tm * PACK_ROWS), :], sem.at[0]).wait()


def _dispatch(pad_end, dest, xp, n_rows, tm=256):
    n_tiles = xp.shape[0] // (tm * PACK_ROWS)
    grid_spec = pltpu.PrefetchScalarGridSpec(
        num_scalar_prefetch=1, grid=(n_tiles,),
        in_specs=[pl.BlockSpec((TOP_K, tm), lambda i, pe: (0, i), memory_space=pltpu.SMEM),
                  pl.BlockSpec((tm * PACK_ROWS, HEAD_DIM), lambda i, pe: (i, 0))],
        out_specs=pl.BlockSpec(memory_space=pl.ANY),
        scratch_shapes=[pltpu.VMEM((BLOCK_WORDS, HEAD_DIM), _F32),
                        pltpu.SemaphoreType.DMA((2,))])
    return pl.pallas_call(
        functools.partial(_dispatch_body, tm=tm), grid_spec=grid_spec,
        out_shape=jax.ShapeDtypeStruct((n_rows * PACK_ROWS, HEAD_DIM), _F32),
        compiler_params=_cparams(("arbitrary",)), name="dispatch",
    )(pad_end, dest, xp)


def _expert_body(be_ref, nv_ref, nu_ref, x_ref, wg_ref, wu_ref, wd_ref, y_ref,
                 wg_b, wu_b, wd_b):
    i = pl.program_id(0)
    n_valid = nv_ref[i]
    new_expert = (i == 0) | (be_ref[i] != be_ref[jnp.maximum(i - 1, 0)])

    @pl.when(new_expert & (n_valid > 0))
    def _():
        _cast_rows(wg_ref.at[0, 0], wg_b)
        _cast_rows(wu_ref.at[0, 0], wu_b)
        _cast_rows(wd_ref.at[0, 0], wd_b)

    @pl.when(n_valid > 0)
    def _():
        xb = _unpack_load(x_ref, 0, ROW_BLOCK).astype(_BF)
        hg = jnp.dot(xb, wg_b[...], preferred_element_type=_F32)
        hu = jnp.dot(xb, wu_b[...], preferred_element_type=_F32)
        hh = (_silu(hg) * hu).astype(_BF)
        _pack_store(y_ref, 0, ROW_BLOCK, jnp.dot(hh, wd_b[...], preferred_element_type=_F32))

    @pl.when(n_valid == 0)
    def _():
        y_ref[...] = jnp.zeros_like(y_ref)


def _experts(blk_expert, n_valid, n_used, xs, wg, wu, wd, layer):
    n_blocks = xs.shape[0] // BLOCK_WORDS
    x_map = lambda i, be, nv, nu: (jnp.minimum(i, nu[0] - 1), 0)
    w_map = lambda i, be, nv, nu: (layer, be[i], 0, 0)
    grid_spec = pltpu.PrefetchScalarGridSpec(
        num_scalar_prefetch=3, grid=(n_blocks,),
        in_specs=[pl.BlockSpec((BLOCK_WORDS, HEAD_DIM), x_map),
                  pl.BlockSpec((1, 1, D_MODEL, EXPERT_DIM), w_map),
                  pl.BlockSpec((1, 1, D_MODEL, EXPERT_DIM), w_map),
                  pl.BlockSpec((1, 1, EXPERT_DIM, D_MODEL), w_map)],
        out_specs=pl.BlockSpec((BLOCK_WORDS, HEAD_DIM), lambda i, be, nv, nu: (i, 0)),
        scratch_shapes=[pltpu.VMEM((D_MODEL, EXPERT_DIM), _BF),
                        pltpu.VMEM((D_MODEL, EXPERT_DIM), _BF),
                        pltpu.VMEM((EXPERT_DIM, D_MODEL), _BF)])
    return pl.pallas_call(
        _expert_body, grid_spec=grid_spec,
        out_shape=jax.ShapeDtypeStruct(xs.shape, _F32),
        compiler_params=_cparams(("arbitrary",)), name="experts",
    )(blk_expert, n_valid, n_used, xs, wg, wu, wd)


COMBINE_ROWS = 32


def _combine_body(dcur_ref, dnxt_ref, y_hbm, gw_ref, x_ref, xb_ref, wg_ref, wu_ref, wd_ref,
                  g_ref, b_ref, o_ref, ob_ref, op_ref, buf, sem, *, tm, n_tiles):
    i = pl.program_id(0)
    slot = i % 2

    def gather(d_ref, s):
        def row(t, carry):
            for k in range(TOP_K):
                src = pl.multiple_of(d_ref[k, t] * PACK_ROWS, PACK_ROWS)
                dst = pl.multiple_of((k * tm + t) * PACK_ROWS, PACK_ROWS)
                pltpu.make_async_copy(y_hbm.at[pl.ds(src, PACK_ROWS), :],
                                      buf.at[s, pl.ds(dst, PACK_ROWS), :], sem.at[s]).start()
            return carry
        lax.fori_loop(0, tm, row, 0)

    @pl.when(i == 0)
    def _():
        gather(dcur_ref, 0)

    @pl.when(i + 1 < n_tiles)
    def _():
        gather(dnxt_ref, 1 - slot)

    xb = xb_ref[...]
    hg = jnp.dot(xb, wg_ref[...], preferred_element_type=_F32)
    hu = jnp.dot(xb, wu_ref[...], preferred_element_type=_F32)
    shared = jnp.dot((_silu(hg) * hu).astype(_BF), wd_ref[...], preferred_element_type=_F32)
    o_ref[...] = ALPHA * x_ref[...] + shared
    pltpu.make_async_copy(y_hbm.at[pl.ds(0, TOP_K * tm * PACK_ROWS), :], buf.at[slot],
                          sem.at[slot]).wait()
    rows_buf = buf.at[slot]

    def add_routed(c, carry):
        r0 = pl.multiple_of(c * COMBINE_ROWS, COMBINE_ROWS)
        rows = pl.ds(r0, COMBINE_ROWS)
        gate = gw_ref[rows, :]
        gates = [jnp.broadcast_to(gate[:, k:k + 1], (COMBINE_ROWS, HEAD_DIM)) for k in range(TOP_K)]
        for j in range(PACK_ROWS):
            cols = slice(j * HEAD_DIM, (j + 1) * HEAD_DIM)
            acc = o_ref[rows, cols]
            for k in range(TOP_K):
                start = (k * tm + r0) * PACK_ROWS + j
                acc = acc + gates[k] * rows_buf[pl.ds(start, COMBINE_ROWS, stride=PACK_ROWS), :]
            o_ref[rows, cols] = acc
        return carry
    lax.fori_loop(0, tm // COMBINE_ROWS, add_routed, 0)
    r = _layer_norm(o_ref[...], g_ref[...], b_ref[...])
    o_ref[...] = r
    ob_ref[...] = r.astype(_BF)
    _pack_store(op_ref, 0, tm, r)


def _combine(dest, y, gw_t, x, xb, wg, wu, wd, g, b, tm=128):
    t = x.shape[0]
    nt = t // tm
    row = lambda i: (i, 0)
    fixed = lambda i: (0, 0)
    return pl.pallas_call(
        functools.partial(_combine_body, tm=tm, n_tiles=nt),
        grid=(nt,),
        in_specs=[pl.BlockSpec((TOP_K, tm), lambda i: (0, i), memory_space=pltpu.SMEM),
                  pl.BlockSpec((TOP_K, tm), lambda i: (0, jnp.minimum(i + 1, nt - 1)),
                               memory_space=pltpu.SMEM),
                  pl.BlockSpec(memory_space=pl.ANY),
                  pl.BlockSpec((tm, TOP_K), row),
                  pl.BlockSpec((tm, D_MODEL), row), pl.BlockSpec((tm, D_MODEL), row),
                  pl.BlockSpec((D_MODEL, EXPERT_DIM), fixed),
                  pl.BlockSpec((D_MODEL, EXPERT_DIM), fixed),
                  pl.BlockSpec((EXPERT_DIM, D_MODEL), fixed),
                  pl.BlockSpec((1, D_MODEL), fixed), pl.BlockSpec((1, D_MODEL), fixed)],
        out_specs=[pl.BlockSpec((tm, D_MODEL), row), pl.BlockSpec((tm, D_MODEL), row),
                   pl.BlockSpec((tm * PACK_ROWS, HEAD_DIM), row)],
        out_shape=[jax.ShapeDtypeStruct((t, D_MODEL), _F32),
                   jax.ShapeDtypeStruct((t, D_MODEL), _BF),
                   jax.ShapeDtypeStruct((t * PACK_ROWS, HEAD_DIM), _F32)],
        scratch_shapes=[pltpu.VMEM((2, TOP_K * tm * PACK_ROWS, HEAD_DIM), _F32),
                        pltpu.SemaphoreType.DMA((2,))],
        compiler_params=_cparams(("arbitrary",)), name="combine",
    )(dest, dest, y, gw_t, x, xb, wg, wu, wd, g, b)


def _moe(x, xb, xp, layer, w_router, e_bias, w_gate, w_up, w_down, ws_gate, ws_up, ws_down, g, b):
    t = x.shape[0]
    eidx, gw, rank, counts = _router(x, w_router.T, e_bias.reshape(N_EXPERTS, 1))
    counts = counts.reshape(N_EXPERTS)
    padded = (counts + ROW_BLOCK - 1) // ROW_BLOCK * ROW_BLOCK
    pad_end = jnp.cumsum(padded)
    pad_start = pad_end - padded
    n_blocks = t * TOP_K // ROW_BLOCK + N_EXPERTS
    expert_ids = jnp.arange(N_EXPERTS, dtype=jnp.int32)
    dest = rank + jnp.sum(jnp.where(eidx[None] == expert_ids[:, None, None],
                                    pad_start[:, None, None], 0), axis=0)
    blk_start = jnp.arange(n_blocks, dtype=jnp.int32) * ROW_BLOCK
    blk_expert = jnp.minimum(jnp.sum(pad_end[None, :] <= blk_start[:, None], axis=1),
                             N_EXPERTS - 1).astype(jnp.int32)
    n_valid = jnp.clip((pad_start + counts)[blk_expert] - blk_start, 0, ROW_BLOCK)
    n_valid = jnp.where(blk_start < pad_end[-1], n_valid, 0).astype(jnp.int32)
    n_used = (pad_end[-1:] // ROW_BLOCK).astype(jnp.int32)
    xs = _dispatch(pad_end.astype(jnp.int32), dest, xp, n_blocks * ROW_BLOCK)
    y = _experts(blk_expert, n_valid, n_used, xs, w_gate, w_up, w_down, layer)
    return _combine(dest, y, gw.T, x, xb, ws_gate.astype(_BF), ws_up.astype(_BF),
                    ws_down.astype(_BF), g.reshape(1, -1), b.reshape(1, -1))


def kernel(x, ln_mix_g, ln_mix_b, ln_ffn_g, ln_ffn_b, gmlp_w_in, gmlp_v_ln_g, gmlp_v_ln_b, gmlp_w_sp, gmlp_b_sp, gmlp_w_out, hgrn_w_in, hgrn_o_norm_g, hgrn_w_out, hgrn_lower_bounds, moe_w_router, moe_e_bias, moe_w_gate, moe_w_up, moe_w_down, moe_ws_gate, moe_ws_up, moe_ws_down):
    bsz, seq, d = x.shape
    t = bsz * seq
    row = lambda a: a.reshape(1, -1)
    xf = x.reshape(t, d)
    xb = xf.astype(_BF)

    z = _proj(xb, gmlp_w_in[0], _epi_gelu, [_BF], 2 * d, name="gmlp_in")[0]
    y = _sgu(z, row(gmlp_v_ln_g[0]), row(gmlp_v_ln_b[0]), gmlp_w_sp[0], gmlp_b_sp[0].T)
    xf, xb, xp = _out_ln(y, gmlp_w_out[0].astype(_BF), xf, row(ln_mix_g[0]), row(ln_mix_b[0]))
    xf, xb, _ = _moe(xf, xb, xp, 0, moe_w_router[0], moe_e_bias[0], moe_w_gate, moe_w_up,
                     moe_w_down, moe_ws_gate[0], moe_ws_up[0], moe_ws_down[0],
                     ln_ffn_g[0], ln_ffn_b[0])

    lb_soft = jax.nn.softmax(hgrn_lower_bounds.astype(_F32), axis=0)
    lb = (jnp.cumsum(lb_soft, axis=0) - lb_soft[0])[1]
    w_in = hgrn_w_in[0]
    q_s = _proj(xb, w_in, _epi_silu, [_BF], d, col0=0, name="hgrn_q")[0]
    log_f, kk = _proj(xb, w_in, _epi_forget, [_F32, _BF], d, col0=d,
                      vecs=(row(jnp.log(lb)), row(jnp.log1p(-lb)), row(1.0 - lb)), name="hgrn_f")
    vv = _proj(xb, w_in, _epi_id, [_BF], d, col0=2 * d, name="hgrn_i")[0]
    g_s = _proj(xb, w_in, _epi_silu, [_BF], d, col0=3 * d, name="hgrn_g")[0]
    o = _recurrence(q_s, log_f, kk, vv, g_s, row(hgrn_o_norm_g[0]), bsz, seq)
    xf, xb, xp = _out_ln(o, hgrn_w_out[0].astype(_BF), xf, row(ln_mix_g[1]), row(ln_mix_b[1]))
    xf, xb, _ = _moe(xf, xb, xp, 1, moe_w_router[1], moe_e_bias[1], moe_w_gate, moe_w_up,
                     moe_w_down, moe_ws_gate[1], moe_ws_up[1], moe_ws_down[1],
                     ln_ffn_g[1], ln_ffn_b[1])
    return xf.reshape(bsz, seq, d)
```

```python
import functools
import math

import numpy as np
import jax
import jax.numpy as jnp
from jax import lax
from jax.experimental import pallas as pl
from jax.experimental.pallas import tpu as pltpu

D_MODEL = 2048
N_HEADS = 16
HEAD_DIM = 128
GMLP_BLOCK = 128
STREAM_CHUNK = 64
N_EXPERTS = 64
TOP_K = 8
N_GROUPS = 8
GROUP_SIZE = N_EXPERTS // N_GROUPS
TOPK_GROUPS = 4
EXPERT_DIM = 512
ROUTED_SCALE = 2.5
LN_EPS = 1e-5
DEPTH = 2
ALPHA = (2 * DEPTH) ** 0.25

ROW_BLOCK = 256
PACK_ROWS = D_MODEL // HEAD_DIM
REC_CHUNK = 128
REC_LEVELS = int(math.log2(REC_CHUNK))
VMEM_LIMIT = 56 * 1024 * 1024

_BF = jnp.bfloat16
_F32 = jnp.float32


def _cparams(sem):
    return pltpu.CompilerParams(dimension_semantics=sem, vmem_limit_bytes=VMEM_LIMIT)


def _sigmoid(x):
    return 1.0 / (1.0 + jnp.exp(-x))


def _silu(x):
    return x * _sigmoid(x)


STAGE_PAD = 8


def _slab_store(ref, stage, m, val):
    pitch = m + STAGE_PAD
    for j in range(PACK_ROWS):
        stage[j * pitch:j * pitch + m, :] = val[:, j * HEAD_DIM:(j + 1) * HEAD_DIM]

    for r in range(m):
        slab = stage[pl.ds(r, PACK_ROWS, stride=pitch), :]
        ref[r * PACK_ROWS:(r + 1) * PACK_ROWS, :] = slab.astype(_BF)


def _slab_load(stage, base, m):
    return jnp.concatenate([stage[pl.ds(base + j, m, stride=PACK_ROWS), :]
                            for j in range(PACK_ROWS)], axis=1)


def _layer_norm(x, g, b):
    mu = jnp.mean(x, axis=-1, keepdims=True)
    xc = x - mu
    var = jnp.mean(xc * xc, axis=-1, keepdims=True)
    return xc * lax.rsqrt(var + LN_EPS) * g + b


CAST_ROWS = 256


def _cast_rows(src, dst):
    def step(i, carry):
        rows = pl.ds(pl.multiple_of(i * CAST_ROWS, CAST_ROWS), CAST_ROWS)
        dst[rows, :] = src[rows, :].astype(_BF)
        return carry
    lax.fori_loop(0, src.shape[0] // CAST_ROWS, step, 0)


def _proj_body(epi, n_vec, n_out, x_ref, w_ref, *refs):
    vecs = [r[...] for r in refs[:n_vec]]
    outs = refs[n_vec:n_vec + n_out]
    wb_ref = refs[n_vec + n_out]

    @pl.when(pl.program_id(1) == 0)
    def _():
        _cast_rows(w_ref, wb_ref)

    acc = jnp.dot(x_ref[...], wb_ref[...], preferred_element_type=_F32)
    res = epi(acc, *vecs)
    for o_ref, r in zip(outs, res):
        o_ref[...] = r.astype(o_ref.dtype)


def _proj(x, w, epi, out_dtypes, n, col0=0, vecs=(), tm=512, tn=1024, name="proj"):
    m, k = x.shape
    grid = (n // tn, m // tm)
    c0 = col0 // tn
    in_specs = [pl.BlockSpec((tm, k), lambda j, i: (i, 0)),
                pl.BlockSpec((k, tn), lambda j, i: (0, c0 + j))]
    in_specs += [pl.BlockSpec((1, tn), lambda j, i: (0, j)) for _ in vecs]
    out_specs = [pl.BlockSpec((tm, tn), lambda j, i: (i, j)) for _ in out_dtypes]
    out_shape = [jax.ShapeDtypeStruct((m, n), dt) for dt in out_dtypes]
    return pl.pallas_call(
        functools.partial(_proj_body, epi, len(vecs), len(out_dtypes)),
        grid=grid, in_specs=in_specs, out_specs=out_specs, out_shape=out_shape,
        scratch_shapes=[pltpu.VMEM((k, tn), _BF)],
        compiler_params=_cparams(("arbitrary", "arbitrary")), name=name,
    )(x, w, *vecs)


def _epi_gelu(acc):
    return (0.5 * acc * (1.0 + lax.erf(acc * (1.0 / math.sqrt(2.0)))),)


def _epi_silu(acc):
    return (_silu(acc),)


def _epi_id(acc):
    return (acc,)


def _epi_forget(acc, log_lb, log1m_lb, one_m_lb):
    ls = jnp.minimum(acc, 0.0) - jnp.log1p(jnp.exp(-jnp.abs(acc)))
    c = log1m_lb + ls
    hi = jnp.maximum(log_lb, c)
    log_f = hi + jnp.log1p(jnp.exp(-jnp.abs(log_lb - c)))
    k = one_m_lb * _sigmoid(-acc)
    return log_f, k


def _sgu_body(u_ref, v_ref, g_ref, b_ref, wsp_ref, bsp_ref, y_ref, *, n_blk):
    vn = _layer_norm(v_ref[...].astype(_F32), g_ref[...], b_ref[...]).astype(_BF)
    ri = lax.broadcasted_iota(jnp.int32, (GMLP_BLOCK, GMLP_BLOCK), 0) // STREAM_CHUNK
    ci = lax.broadcasted_iota(jnp.int32, (GMLP_BLOCK, GMLP_BLOCK), 1) // STREAM_CHUNK
    causal = ri >= ci
    for h in range(N_HEADS):
        w = jnp.where(causal, wsp_ref[h], 0.0).astype(_BF)
        bias = bsp_ref[:, h:h + 1]
        cs = slice(h * HEAD_DIM, (h + 1) * HEAD_DIM)
        for n in range(n_blk):
            rs = slice(n * GMLP_BLOCK, (n + 1) * GMLP_BLOCK)
            sv = jnp.dot(w, vn[rs, cs], preferred_element_type=_F32) + bias
            y_ref[rs, cs] = (u_ref[rs, cs].astype(_F32) * sv).astype(_BF)


def _sgu(z, g, b, w_sp, bsp_t, tm=256):
    t = z.shape[0]
    return pl.pallas_call(
        functools.partial(_sgu_body, n_blk=tm // GMLP_BLOCK),
        grid=(t // tm,),
        in_specs=[pl.BlockSpec((tm, D_MODEL), lambda i: (i, 0)),
                  pl.BlockSpec((tm, D_MODEL), lambda i: (i, 1)),
                  pl.BlockSpec((1, D_MODEL), lambda i: (0, 0)),
                  pl.BlockSpec((1, D_MODEL), lambda i: (0, 0)),
                  pl.BlockSpec((N_HEADS, GMLP_BLOCK, GMLP_BLOCK), lambda i: (0, 0, 0)),
                  pl.BlockSpec((GMLP_BLOCK, N_HEADS), lambda i: (0, 0))],
        out_specs=pl.BlockSpec((tm, D_MODEL), lambda i: (i, 0)),
        out_shape=jax.ShapeDtypeStruct((t, D_MODEL), _BF),
        compiler_params=_cparams(("parallel",)), name="sgu",
    )(z, z, g, b, w_sp, bsp_t)


def _out_ln_body(y_ref, w_ref, x_ref, g_ref, b_ref, o_ref, ob_ref, op_ref, stage):
    h = jnp.dot(y_ref[...], w_ref[...], preferred_element_type=_F32)
    r = _layer_norm(ALPHA * x_ref[...] + h, g_ref[...], b_ref[...])
    o_ref[...] = r
    ob_ref[...] = r.astype(_BF)
    _slab_store(op_ref, stage, r.shape[0], r)


def _out_ln(y, w, x_res, g, b, tm=256):
    t = y.shape[0]
    row = lambda i: (i, 0)
    fixed = lambda i: (0, 0)
    return pl.pallas_call(
        _out_ln_body,
        grid=(t // tm,),
        in_specs=[pl.BlockSpec((tm, D_MODEL), row),
                  pl.BlockSpec((D_MODEL, D_MODEL), fixed),
                  pl.BlockSpec((tm, D_MODEL), row),
                  pl.BlockSpec((1, D_MODEL), fixed),
                  pl.BlockSpec((1, D_MODEL), fixed)],
        out_specs=[pl.BlockSpec((tm, D_MODEL), row), pl.BlockSpec((tm, D_MODEL), row),
                   pl.BlockSpec((tm * PACK_ROWS, HEAD_DIM), row)],
        out_shape=[jax.ShapeDtypeStruct((t, D_MODEL), _F32),
                   jax.ShapeDtypeStruct((t, D_MODEL), _BF),
                   jax.ShapeDtypeStruct((t * PACK_ROWS, HEAD_DIM), _BF)],
        scratch_shapes=[pltpu.VMEM((PACK_ROWS * (tm + STAGE_PAD), HEAD_DIM), _F32)],
        compiler_params=_cparams(("parallel",)), name="out_ln",
    )(y, w, x_res, g, b)


def _rec_constants():
    c = REC_CHUNK
    t = np.arange(c)[:, None]
    j = np.arange(c)[None, :]
    tri = (j <= t).astype(np.float32)
    pair = []
    for lv in range(REC_LEVELS):
        m = c >> (lv + 1)
        same_block = (t // (2 * m)) == (j // (2 * m))
        pair.append(same_block & (t % (2 * m) >= m) & (j % (2 * m) < m))
    pair = np.concatenate(pair, 0).astype(np.float32)
    return tri, pair


def _mid_rows(b, m, row):
    c = b.shape[0]
    if 2 * m >= 8:
        return jnp.concatenate([jnp.broadcast_to(b[s + m - 1:s + m, :], (2 * m, HEAD_DIM))
                                for s in range(0, c, 2 * m)], axis=0)
    prev1 = pltpu.roll(b, 1, 0)
    if m == 1:
        return jnp.where(row % 2 == 1, prev1, b)
    p = row % 4
    return jnp.where(p == 0, pltpu.roll(b, c - 1, 0),
                     jnp.where(p == 1, b, jnp.where(p == 2, prev1, pltpu.roll(b, 2, 0))))


def _rec_body(tri_ref, pair_ref, q_ref, lf_ref, k_ref, v_ref, gs_ref, gn_ref,
              o_ref, st_ref, *, hpb):
    c = REC_CHUNK
    nt = (((1,), (1,)), ((), ()))

    @pl.when(pl.program_id(2) == 0)
    def _():
        st_ref[...] = jnp.zeros_like(st_ref)

    tri = tri_ref[...]
    eye = (lax.broadcasted_iota(jnp.int32, (c, c), 0) ==
           lax.broadcasted_iota(jnp.int32, (c, c), 1))
    row = lax.broadcasted_iota(jnp.int32, (c, HEAD_DIM), 0)
    heads = range(hpb)
    cols = [slice(h * HEAD_DIM, (h + 1) * HEAD_DIM) for h in heads]
    q = [q_ref[:, cs].astype(_F32) for cs in cols]
    k = [k_ref[:, cs].astype(_F32) for cs in cols]
    v = [v_ref[:, cs] for cs in cols]

    b = []
    for cs in cols:
        lf = lf_ref[:, cs]
        l1 = lf.astype(_BF)
        r1 = lf - l1.astype(_F32)
        l2 = r1.astype(_BF)
        l3 = (r1 - l2.astype(_F32)).astype(_BF)
        b3 = jnp.dot(tri, jnp.concatenate([l1, l2, l3], axis=1), preferred_element_type=_F32)
        b.append(b3[:, :HEAD_DIM] + b3[:, HEAD_DIM:2 * HEAD_DIM] + b3[:, 2 * HEAD_DIM:])
    b_last = [bh[c - 1:c, :] for bh in b]
    st = [st_ref[h] for h in heads]
    o = [lax.dot_general((q[h] * jnp.exp(b[h])).astype(_BF), st[h].astype(_BF), nt,
                         preferred_element_type=_F32) for h in heads]
    scores = [jnp.where(eye, jnp.sum(q[h] * k[h], axis=1, keepdims=True), 0.0) for h in heads]
    for lv in range(REC_LEVELS):
        pair = pair_ref[lv * c:(lv + 1) * c, :]
        for h in heads:
            e = jnp.exp(-jnp.abs(b[h] - _mid_rows(b[h], c >> (lv + 1), row)))
            s = lax.dot_general((q[h] * e).astype(_BF), (k[h] * e).astype(_BF), nt,
                                preferred_element_type=_F32)
            scores[h] = scores[h] + s * pair
    for h in heads:
        o[h] = o[h] + jnp.dot(scores[h].astype(_BF), v[h], preferred_element_type=_F32)
        k_dec = (k[h] * jnp.exp(b_last[h] - b[h])).astype(_BF)
        upd = lax.dot_general(v[h], k_dec, (((0,), (0,)), ((), ())), preferred_element_type=_F32)
        st_ref[h] = jnp.exp(b_last[h]) * st[h] + upd
    for h in heads:
        ms = jnp.mean(o[h] * o[h], axis=1, keepdims=True)
        oh = o[h] * lax.rsqrt(ms + LN_EPS) * gn_ref[...]
        o_ref[:, cols[h]] = (oh * gs_ref[:, cols[h]].astype(_F32)).astype(_BF)


def _recurrence(q_s, log_f, k, v, g_s, g_norm, bsz, seq, hpb=8):
    c = REC_CHUNK
    tri, pair = _rec_constants()
    tri = jnp.asarray(tri, _BF)
    pair = jnp.asarray(pair, _F32)
    w = hpb * HEAD_DIM
    n_c = seq // c
    fixed = lambda b, h, s: (0, 0)
    tile = lambda b, h, s: (b * n_c + s, h)
    return pl.pallas_call(
        functools.partial(_rec_body, hpb=hpb),
        grid=(bsz, N_HEADS // hpb, n_c),
        in_specs=[pl.BlockSpec(tri.shape, fixed), pl.BlockSpec(pair.shape, fixed),
                  pl.BlockSpec((c, w), tile), pl.BlockSpec((c, w), tile),
                  pl.BlockSpec((c, w), tile), pl.BlockSpec((c, w), tile),
                  pl.BlockSpec((c, w), tile), pl.BlockSpec((1, HEAD_DIM), fixed)],
        out_specs=pl.BlockSpec((c, w), tile),
        out_shape=jax.ShapeDtypeStruct((bsz * seq, D_MODEL), _BF),
        scratch_shapes=[pltpu.VMEM((hpb, HEAD_DIM, HEAD_DIM), _F32)],
        compiler_params=_cparams(("parallel", "parallel", "arbitrary")), name="hgrn_rec",
    )(tri, pair, q_s, log_f, k, v, g_s, g_norm)


def _router_body(x_ref, wr_ref, bias_ref, eidx_ref, gw_ref, rank_ref, cnt_ref, carry_ref, *, tm):
    @pl.when(pl.program_id(0) == 0)
    def _():
        carry_ref[...] = jnp.zeros_like(carry_ref)

    neg = -jnp.inf
    logits = lax.dot_general(wr_ref[...], x_ref[...], (((1,), (1,)), ((), ())),
                             precision=lax.Precision.HIGHEST,
                             preferred_element_type=_F32)
    scores = _sigmoid(logits)
    choice = scores + bias_ref[...]
    c3 = choice.reshape(N_GROUPS, GROUP_SIZE, tm)
    i3 = lax.broadcasted_iota(jnp.int32, c3.shape, 1)
    m1 = jnp.max(c3, axis=1, keepdims=True)
    first = jnp.min(jnp.where(c3 == m1, i3, GROUP_SIZE), axis=1, keepdims=True)
    m2 = jnp.max(jnp.where(i3 == first, neg, c3), axis=1, keepdims=True)
    gs = (m1 + m2).reshape(N_GROUPS, tm)
    ig = lax.broadcasted_iota(jnp.int32, gs.shape, 0)
    gsel = jnp.zeros(gs.shape, jnp.bool_)
    for _ in range(TOPK_GROUPS):
        m = jnp.max(gs, axis=0, keepdims=True)
        gi = jnp.min(jnp.where(gs == m, ig, N_GROUPS), axis=0, keepdims=True)
        hit = ig == gi
        gsel = gsel | hit
        gs = jnp.where(hit, neg, gs)
    allowed = jnp.broadcast_to(gsel.reshape(N_GROUPS, 1, tm), c3.shape).reshape(N_EXPERTS, tm)
    masked = jnp.where(allowed, choice, neg)
    ie = lax.broadcasted_iota(jnp.int32, masked.shape, 0)
    picked = jnp.zeros(masked.shape, _F32)
    hits, e_rows, w_rows = [], [], []
    for _ in range(TOP_K):
        m = jnp.max(masked, axis=0, keepdims=True)
        ei = jnp.min(jnp.where(masked == m, ie, N_EXPERTS), axis=0, keepdims=True)
        hit = ie == ei
        hits.append(hit)
        e_rows.append(ei)
        w_rows.append(jnp.sum(jnp.where(hit, scores, 0.0), axis=0, keepdims=True))
        picked = picked + hit.astype(_F32)
        masked = jnp.where(hit, neg, masked)
    gw = jnp.concatenate(w_rows, axis=0)
    gw = gw / jnp.sum(gw, axis=0, keepdims=True) * ROUTED_SCALE
    before = (lax.broadcasted_iota(jnp.int32, (tm, tm), 0) <
              lax.broadcasted_iota(jnp.int32, (tm, tm), 1)).astype(_BF)
    cum = jnp.dot(picked.astype(_BF), before, preferred_element_type=_F32) + carry_ref[...]
    r_rows = [jnp.sum(jnp.where(h, cum, 0.0), axis=0, keepdims=True) for h in hits]
    carry = carry_ref[...] + jnp.sum(picked, axis=1, keepdims=True)
    carry_ref[...] = carry
    eidx_ref[...] = jnp.concatenate(e_rows, axis=0)
    gw_ref[...] = gw
    rank_ref[...] = jnp.concatenate(r_rows, axis=0).astype(jnp.int32)
    cnt_ref[...] = carry.astype(jnp.int32)


def _router(x, wr_t, bias_col, tm=512):
    t = x.shape[0]
    tok = lambda i: (0, i)
    fixed = lambda i: (0, 0)
    return pl.pallas_call(
        functools.partial(_router_body, tm=tm),
        grid=(t // tm,),
        in_specs=[pl.BlockSpec((tm, D_MODEL), lambda i: (i, 0)),
                  pl.BlockSpec((N_EXPERTS, D_MODEL), fixed),
                  pl.BlockSpec((N_EXPERTS, 1), fixed)],
        out_specs=[pl.BlockSpec((TOP_K, tm), tok), pl.BlockSpec((TOP_K, tm), tok),
                   pl.BlockSpec((TOP_K, tm), tok), pl.BlockSpec((N_EXPERTS, 1), fixed)],
        out_shape=[jax.ShapeDtypeStruct((TOP_K, t), jnp.int32),
                   jax.ShapeDtypeStruct((TOP_K, t), _F32),
                   jax.ShapeDtypeStruct((TOP_K, t), jnp.int32),
                   jax.ShapeDtypeStruct((N_EXPERTS, 1), jnp.int32)],
        scratch_shapes=[pltpu.VMEM((N_EXPERTS, 1), _F32)],
        compiler_params=_cparams(("arbitrary",)), name="router",
    )(x, wr_t, bias_col)


BLOCK_WORDS = ROW_BLOCK * PACK_ROWS


def _dispatch_body(pend_ref, dest_ref, xp_ref, xs_hbm, zbuf, sem, *, tm):
    i = pl.program_id(0)

    def block_copy(b):
        start = pl.multiple_of(b * BLOCK_WORDS, BLOCK_WORDS)
        return pltpu.make_async_copy(zbuf, xs_hbm.at[pl.ds(start, BLOCK_WORDS), :], sem.at[1])

    def pad_copy(e):
        return block_copy(pend_ref[e] // ROW_BLOCK - 1)

    def has_rows(e):
        return pend_ref[e] > jnp.where(e > 0, pend_ref[jnp.maximum(e - 1, 0)], 0)

    @pl.when(i == 0)
    def _():
        zbuf[...] = jnp.zeros_like(zbuf)

        def start(e, carry):
            @pl.when(has_rows(e))
            def _():
                pad_copy(e).start()
            return carry
        lax.fori_loop(0, N_EXPERTS, start, 0)

        def wait(e, carry):
            @pl.when(has_rows(e))
            def _():
                pad_copy(e).wait()
            return carry
        lax.fori_loop(0, N_EXPERTS, wait, 0)

        first_unused = pend_ref[N_EXPERTS - 1] // ROW_BLOCK
        n_blocks = xs_hbm.shape[0] // BLOCK_WORDS

        def tail_start(b, carry):
            block_copy(b).start()
            return carry
        lax.fori_loop(first_unused, n_blocks, tail_start, 0)

        def tail_wait(b, carry):
            block_copy(b).wait()
            return carry
        lax.fori_loop(first_unused, n_blocks, tail_wait, 0)

    def push(t, carry):
        src = xp_ref.at[pl.ds(pl.multiple_of(t * PACK_ROWS, PACK_ROWS), PACK_ROWS), :]
        for k in range(TOP_K):
            row = pl.multiple_of(dest_ref[k, t] * PACK_ROWS, PACK_ROWS)
            pltpu.make_async_copy(src, xs_hbm.at[pl.ds(row, PACK_ROWS), :], sem.at[0]).start()
        return carry
    lax.fori_loop(0, tm, push, 0)
    for k in range(TOP_K):
        pltpu.make_async_copy(xp_ref, xs_hbm.at[pl.ds(0, tm * PACK_ROWS), :], sem.at[0]).wait()


def _dispatch(pad_end, dest, xp, n_rows, tm=256):
    n_tiles = xp.shape[0] // (tm * PACK_ROWS)
    grid_spec = pltpu.PrefetchScalarGridSpec(
        num_scalar_prefetch=1, grid=(n_tiles,),
        in_specs=[pl.BlockSpec((TOP_K, tm), lambda i, pe: (0, i), memory_space=pltpu.SMEM),
                  pl.BlockSpec((tm * PACK_ROWS, HEAD_DIM), lambda i, pe: (i, 0))],
        out_specs=pl.BlockSpec(memory_space=pl.ANY),
        scratch_shapes=[pltpu.VMEM((BLOCK_WORDS, HEAD_DIM), _BF),
                        pltpu.SemaphoreType.DMA((2,))])
    return pl.pallas_call(
        functools.partial(_dispatch_body, tm=tm), grid_spec=grid_spec,
        out_shape=jax.ShapeDtypeStruct((n_rows * PACK_ROWS, HEAD_DIM), _BF),
        compiler_params=_cparams(("arbitrary",)), name="dispatch",
    )(pad_end, dest, xp)


def _expert_body(be_ref, nv_ref, nu_ref, x_ref, wg_ref, wu_ref, wd_ref, y_ref,
                 wg_b, wu_b, wd_b, stage):
    i = pl.program_id(0)
    n_valid = nv_ref[i]
    new_expert = (i == 0) | (be_ref[i] != be_ref[jnp.maximum(i - 1, 0)])

    @pl.when(new_expert & (n_valid > 0))
    def _():
        _cast_rows(wg_ref.at[0, 0], wg_b)
        _cast_rows(wu_ref.at[0, 0], wu_b)
        _cast_rows(wd_ref.at[0, 0], wd_b)

    @pl.when(n_valid > 0)
    def _():
        stage[0:BLOCK_WORDS, :] = x_ref[...].astype(_F32)
        xb = _slab_load(stage, 0, ROW_BLOCK).astype(_BF)
        hg = jnp.dot(xb, wg_b[...], preferred_element_type=_F32)
        hu = jnp.dot(xb, wu_b[...], preferred_element_type=_F32)
        hh = (_silu(hg) * hu).astype(_BF)
        _slab_store(y_ref, stage, ROW_BLOCK, jnp.dot(hh, wd_b[...], preferred_element_type=_F32))

    @pl.when(n_valid == 0)
    def _():
        y_ref[...] = jnp.zeros_like(y_ref)


def _experts(blk_expert, n_valid, n_used, xs, wg, wu, wd, layer):
    n_blocks = xs.shape[0] // BLOCK_WORDS
    x_map = lambda i, be, nv, nu: (jnp.minimum(i, nu[0] - 1), 0)
    w_map = lambda i, be, nv, nu: (layer, be[i], 0, 0)
    grid_spec = pltpu.PrefetchScalarGridSpec(
        num_scalar_prefetch=3, grid=(n_blocks,),
        in_specs=[pl.BlockSpec((BLOCK_WORDS, HEAD_DIM), x_map),
                  pl.BlockSpec((1, 1, D_MODEL, EXPERT_DIM), w_map),
                  pl.BlockSpec((1, 1, D_MODEL, EXPERT_DIM), w_map),
                  pl.BlockSpec((1, 1, EXPERT_DIM, D_MODEL), w_map)],
        out_specs=pl.BlockSpec((BLOCK_WORDS, HEAD_DIM), lambda i, be, nv, nu: (i, 0)),
        scratch_shapes=[pltpu.VMEM((D_MODEL, EXPERT_DIM), _BF),
                        pltpu.VMEM((D_MODEL, EXPERT_DIM), _BF),
                        pltpu.VMEM((EXPERT_DIM, D_MODEL), _BF),
                        pltpu.VMEM((PACK_ROWS * (ROW_BLOCK + STAGE_PAD), HEAD_DIM), _F32)])
    return pl.pallas_call(
        _expert_body, grid_spec=grid_spec,
        out_shape=jax.ShapeDtypeStruct(xs.shape, _BF),
        compiler_params=_cparams(("arbitrary",)), name="experts",
    )(blk_expert, n_valid, n_used, xs, wg, wu, wd)


COMBINE_ROWS = 32


def _combine_body(dcur_ref, dnxt_ref, y_hbm, gw_ref, x_ref, xb_ref, wg_ref, wu_ref, wd_ref,
                  g_ref, b_ref, o_ref, ob_ref, buf, stage, sem, *, tm, n_tiles):
    i = pl.program_id(0)
    slot = i % 2

    def gather(d_ref, s):
        def row(t, carry):
            for k in range(TOP_K):
                src = pl.multiple_of(d_ref[k, t] * PACK_ROWS, PACK_ROWS)
                dst = pl.multiple_of((k * tm + t) * PACK_ROWS, PACK_ROWS)
                pltpu.make_async_copy(y_hbm.at[pl.ds(src, PACK_ROWS), :],
                                      buf.at[s, pl.ds(dst, PACK_ROWS), :], sem.at[s]).start()
            return carry
        lax.fori_loop(0, tm, row, 0)

    @pl.when(i == 0)
    def _():
        gather(dcur_ref, 0)

    @pl.when(i + 1 < n_tiles)
    def _():
        gather(dnxt_ref, 1 - slot)

    xb = xb_ref[...]
    hg = jnp.dot(xb, wg_ref[...], preferred_element_type=_F32)
    hu = jnp.dot(xb, wu_ref[...], preferred_element_type=_F32)
    shared = jnp.dot((_silu(hg) * hu).astype(_BF), wd_ref[...], preferred_element_type=_F32)
    o_ref[...] = ALPHA * x_ref[...] + shared
    pltpu.make_async_copy(y_hbm.at[pl.ds(0, TOP_K * tm * PACK_ROWS), :], buf.at[slot],
                          sem.at[slot]).wait()
    stage[...] = buf[slot].astype(_F32)

    def add_routed(c, carry):
        r0 = pl.multiple_of(c * COMBINE_ROWS, COMBINE_ROWS)
        rows = pl.ds(r0, COMBINE_ROWS)
        gate = gw_ref[rows, :]
        gates = [jnp.broadcast_to(gate[:, k:k + 1], (COMBINE_ROWS, HEAD_DIM)) for k in range(TOP_K)]
        for j in range(PACK_ROWS):
            cols = slice(j * HEAD_DIM, (j + 1) * HEAD_DIM)
            acc = o_ref[rows, cols]
            for k in range(TOP_K):
                start = (k * tm + r0) * PACK_ROWS + j
                acc = acc + gates[k] * stage[pl.ds(start, COMBINE_ROWS, stride=PACK_ROWS), :]
            o_ref[rows, cols] = acc
        return carry
    lax.fori_loop(0, tm // COMBINE_ROWS, add_routed, 0)
    r = _layer_norm(o_ref[...], g_ref[...], b_ref[...])
    o_ref[...] = r
    ob_ref[...] = r.astype(_BF)


def _combine(dest, y, gw_t, x, xb, wg, wu, wd, g, b, tm=128):
    t = x.shape[0]
    nt = t // tm
    row = lambda i: (i, 0)
    fixed = lambda i: (0, 0)
    return pl.pallas_call(
        functools.partial(_combine_body, tm=tm, n_tiles=nt),
        grid=(nt,),
        in_specs=[pl.BlockSpec((TOP_K, tm), lambda i: (0, i), memory_space=pltpu.SMEM),
                  pl.BlockSpec((TOP_K, tm), lambda i: (0, jnp.minimum(i + 1, nt - 1)),
                               memory_space=pltpu.SMEM),
                  pl.BlockSpec(memory_space=pl.ANY),
                  pl.BlockSpec((tm, TOP_K), row),
                  pl.BlockSpec((tm, D_MODEL), row), pl.BlockSpec((tm, D_MODEL), row),
                  pl.BlockSpec((D_MODEL, EXPERT_DIM), fixed),
                  pl.BlockSpec((D_MODEL, EXPERT_DIM), fixed),
                  pl.BlockSpec((EXPERT_DIM, D_MODEL), fixed),
                  pl.BlockSpec((1, D_MODEL), fixed), pl.BlockSpec((1, D_MODEL), fixed)],
        out_specs=[pl.BlockSpec((tm, D_MODEL), row), pl.BlockSpec((tm, D_MODEL), row)],
        out_shape=[jax.ShapeDtypeStruct((t, D_MODEL), _F32),
                   jax.ShapeDtypeStruct((t, D_MODEL), _BF)],
        scratch_shapes=[pltpu.VMEM((2, TOP_K * tm * PACK_ROWS, HEAD_DIM), _BF),
                        pltpu.VMEM((TOP_K * tm * PACK_ROWS, HEAD_DIM), _F32),
                        pltpu.SemaphoreType.DMA((2,))],
        compiler_params=_cparams(("arbitrary",)), name="combine",
    )(dest, dest, y, gw_t, x, xb, wg, wu, wd, g, b)


def _moe(x, xb, xp, layer, w_router, e_bias, w_gate, w_up, w_down, ws_gate, ws_up, ws_down, g, b):
    t = x.shape[0]
    eidx, gw, rank, counts = _router(x, w_router.T, e_bias.reshape(N_EXPERTS, 1))
    counts = counts.reshape(N_EXPERTS)
    padded = (counts + ROW_BLOCK - 1) // ROW_BLOCK * ROW_BLOCK
    pad_end = jnp.cumsum(padded)
    pad_start = pad_end - padded
    n_blocks = t * TOP_K // ROW_BLOCK + N_EXPERTS
    expert_ids = jnp.arange(N_EXPERTS, dtype=jnp.int32)
    dest = rank + jnp.sum(jnp.where(eidx[None] == expert_ids[:, None, None],
                                    pad_start[:, None, None], 0), axis=0)
    blk_start = jnp.arange(n_blocks, dtype=jnp.int32) * ROW_BLOCK
    blk_expert = jnp.minimum(jnp.sum(pad_end[None, :] <= blk_start[:, None], axis=1),
                             N_EXPERTS - 1).astype(jnp.int32)
    n_valid = jnp.clip((pad_start + counts)[blk_expert] - blk_start, 0, ROW_BLOCK)
    n_valid = jnp.where(blk_start < pad_end[-1], n_valid, 0).astype(jnp.int32)
    n_used = (pad_end[-1:] // ROW_BLOCK).astype(jnp.int32)
    xs = _dispatch(pad_end.astype(jnp.int32), dest, xp, n_blocks * ROW_BLOCK)
    y = _experts(blk_expert, n_valid, n_used, xs, w_gate, w_up, w_down, layer)
    return _combine(dest, y, gw.T, x, xb, ws_gate.astype(_BF), ws_up.astype(_BF),
                    ws_down.astype(_BF), g.reshape(1, -1), b.reshape(1, -1))


def kernel(x, ln_mix_g, ln_mix_b, ln_ffn_g, ln_ffn_b, gmlp_w_in, gmlp_v_ln_g, gmlp_v_ln_b, gmlp_w_sp, gmlp_b_sp, gmlp_w_out, hgrn_w_in, hgrn_o_norm_g, hgrn_w_out, hgrn_lower_bounds, moe_w_router, moe_e_bias, moe_w_gate, moe_w_up, moe_w_down, moe_ws_gate, moe_ws_up, moe_ws_down):
    bsz, seq, d = x.shape
    t = bsz * seq
    row = lambda a: a.reshape(1, -1)
    xf = x.reshape(t, d)
    xb = xf.astype(_BF)

    z = _proj(xb, gmlp_w_in[0], _epi_gelu, [_BF], 2 * d, name="gmlp_in")[0]
    y = _sgu(z, row(gmlp_v_ln_g[0]), row(gmlp_v_ln_b[0]), gmlp_w_sp[0], gmlp_b_sp[0].T)
    xf, xb, xp = _out_ln(y, gmlp_w_out[0].astype(_BF), xf, row(ln_mix_g[0]), row(ln_mix_b[0]))
    xf, xb = _moe(xf, xb, xp, 0, moe_w_router[0], moe_e_bias[0], moe_w_gate, moe_w_up,
                     moe_w_down, moe_ws_gate[0], moe_ws_up[0], moe_ws_down[0],
                     ln_ffn_g[0], ln_ffn_b[0])

    lb_soft = jax.nn.softmax(hgrn_lower_bounds.astype(_F32), axis=0)
    lb = (jnp.cumsum(lb_soft, axis=0) - lb_soft[0])[1]
    w_in = hgrn_w_in[0]
    q_s = _proj(xb, w_in, _epi_silu, [_BF], d, col0=0, name="hgrn_q")[0]
    log_f, kk = _proj(xb, w_in, _epi_forget, [_F32, _BF], d, col0=d,
                      vecs=(row(jnp.log(lb)), row(jnp.log1p(-lb)), row(1.0 - lb)), name="hgrn_f")
    vv = _proj(xb, w_in, _epi_id, [_BF], d, col0=2 * d, name="hgrn_i")[0]
    g_s = _proj(xb, w_in, _epi_silu, [_BF], d, col0=3 * d, name="hgrn_g")[0]
    o = _recurrence(q_s, log_f, kk, vv, g_s, row(hgrn_o_norm_g[0]), bsz, seq)
    xf, xb, xp = _out_ln(o, hgrn_w_out[0].astype(_BF), xf, row(ln_mix_g[1]), row(ln_mix_b[1]))
    xf, xb = _moe(xf, xb, xp, 1, moe_w_router[1], moe_e_bias[1], moe_w_gate, moe_w_up,
                     moe_w_down, moe_ws_gate[1], moe_ws_up[1], moe_ws_down[1],
                     ln_ffn_g[1], ln_ffn_b[1])
    return xf.reshape(bsz, seq, d)
```

```python
import functools
import math

import numpy as np
import jax
import jax.numpy as jnp
from jax import lax
from jax.experimental import pallas as pl
from jax.experimental.pallas import tpu as pltpu

D_MODEL = 2048
N_HEADS = 16
HEAD_DIM = 128
GMLP_BLOCK = 128
STREAM_CHUNK = 64
N_EXPERTS = 64
TOP_K = 8
N_GROUPS = 8
GROUP_SIZE = N_EXPERTS // N_GROUPS
TOPK_GROUPS = 4
EXPERT_DIM = 512
ROUTED_SCALE = 2.5
LN_EPS = 1e-5
DEPTH = 2
ALPHA = (2 * DEPTH) ** 0.25

ROW_BLOCK = 256
PACK_ROWS = D_MODEL // HEAD_DIM
REC_CHUNK = 128
REC_LEVELS = int(math.log2(REC_CHUNK))
VMEM_LIMIT = 56 * 1024 * 1024

_BF = jnp.bfloat16
_F32 = jnp.float32


def _cparams(sem):
    return pltpu.CompilerParams(dimension_semantics=sem, vmem_limit_bytes=VMEM_LIMIT)


def _sigmoid(x):
    return 1.0 / (1.0 + jnp.exp(-x))


def _silu(x):
    return x * _sigmoid(x)


STAGE_PAD = 8


def _slab_store(ref, stage, m, val):
    pitch = m + STAGE_PAD
    for j in range(PACK_ROWS):
        stage[j * pitch:j * pitch + m, :] = val[:, j * HEAD_DIM:(j + 1) * HEAD_DIM]

    for r in range(m):
        slab = stage[pl.ds(r, PACK_ROWS, stride=pitch), :]
        ref[r * PACK_ROWS:(r + 1) * PACK_ROWS, :] = slab.astype(_BF)


def _layer_norm(x, g, b):
    mu = jnp.mean(x, axis=-1, keepdims=True)
    xc = x - mu
    var = jnp.mean(xc * xc, axis=-1, keepdims=True)
    return xc * lax.rsqrt(var + LN_EPS) * g + b


CAST_ROWS = 256


def _cast_rows(src, dst):
    def step(i, carry):
        rows = pl.ds(pl.multiple_of(i * CAST_ROWS, CAST_ROWS), CAST_ROWS)
        dst[rows, :] = src[rows, :].astype(_BF)
        return carry
    lax.fori_loop(0, src.shape[0] // CAST_ROWS, step, 0)


def _proj_body(epi, n_vec, n_out, x_ref, w_ref, *refs):
    vecs = [r[...] for r in refs[:n_vec]]
    outs = refs[n_vec:n_vec + n_out]
    wb_ref = refs[n_vec + n_out]

    @pl.when(pl.program_id(1) == 0)
    def _():
        _cast_rows(w_ref, wb_ref)

    acc = jnp.dot(x_ref[...], wb_ref[...], preferred_element_type=_F32)
    res = epi(acc, *vecs)
    for o_ref, r in zip(outs, res):
        o_ref[...] = r.astype(o_ref.dtype)


def _proj(x, w, epi, out_dtypes, n, col0=0, vecs=(), tm=512, tn=1024, name="proj"):
    m, k = x.shape
    grid = (n // tn, m // tm)
    c0 = col0 // tn
    in_specs = [pl.BlockSpec((tm, k), lambda j, i: (i, 0)),
                pl.BlockSpec((k, tn), lambda j, i: (0, c0 + j))]
    in_specs += [pl.BlockSpec((1, tn), lambda j, i: (0, j)) for _ in vecs]
    out_specs = [pl.BlockSpec((tm, tn), lambda j, i: (i, j)) for _ in out_dtypes]
    out_shape = [jax.ShapeDtypeStruct((m, n), dt) for dt in out_dtypes]
    return pl.pallas_call(
        functools.partial(_proj_body, epi, len(vecs), len(out_dtypes)),
        grid=grid, in_specs=in_specs, out_specs=out_specs, out_shape=out_shape,
        scratch_shapes=[pltpu.VMEM((k, tn), _BF)],
        compiler_params=_cparams(("arbitrary", "arbitrary")), name=name,
    )(x, w, *vecs)


def _epi_gelu(acc):
    return (0.5 * acc * (1.0 + lax.erf(acc * (1.0 / math.sqrt(2.0)))),)


def _epi_silu(acc):
    return (_silu(acc),)


def _epi_id(acc):
    return (acc,)


def _epi_forget(acc, log_lb, log1m_lb, one_m_lb):
    e = jnp.exp(-jnp.abs(acc))
    ls = jnp.minimum(acc, 0.0) - jnp.log(1.0 + e)
    c = log1m_lb + ls
    log_f = jnp.maximum(log_lb, c) + jnp.log(1.0 + jnp.exp(-jnp.abs(log_lb - c)))
    inv = 1.0 / (1.0 + e)
    k = one_m_lb * jnp.where(acc > 0.0, e * inv, inv)
    return log_f, k


def _sgu_body(u_ref, v_ref, g_ref, b_ref, wsp_ref, bsp_ref, y_ref, *, n_blk):
    vn = _layer_norm(v_ref[...].astype(_F32), g_ref[...], b_ref[...]).astype(_BF)
    ri = lax.broadcasted_iota(jnp.int32, (GMLP_BLOCK, GMLP_BLOCK), 0) // STREAM_CHUNK
    ci = lax.broadcasted_iota(jnp.int32, (GMLP_BLOCK, GMLP_BLOCK), 1) // STREAM_CHUNK
    causal = ri >= ci
    for h in range(N_HEADS):
        w = jnp.where(causal, wsp_ref[h], 0.0).astype(_BF)
        bias = bsp_ref[:, h:h + 1]
        cs = slice(h * HEAD_DIM, (h + 1) * HEAD_DIM)
        for n in range(n_blk):
            rs = slice(n * GMLP_BLOCK, (n + 1) * GMLP_BLOCK)
            sv = jnp.dot(w, vn[rs, cs], preferred_element_type=_F32) + bias
            y_ref[rs, cs] = (u_ref[rs, cs].astype(_F32) * sv).astype(_BF)


def _sgu(z, g, b, w_sp, bsp_t, tm=256):
    t = z.shape[0]
    return pl.pallas_call(
        functools.partial(_sgu_body, n_blk=tm // GMLP_BLOCK),
        grid=(t // tm,),
        in_specs=[pl.BlockSpec((tm, D_MODEL), lambda i: (i, 0)),
                  pl.BlockSpec((tm, D_MODEL), lambda i: (i, 1)),
                  pl.BlockSpec((1, D_MODEL), lambda i: (0, 0)),
                  pl.BlockSpec((1, D_MODEL), lambda i: (0, 0)),
                  pl.BlockSpec((N_HEADS, GMLP_BLOCK, GMLP_BLOCK), lambda i: (0, 0, 0)),
                  pl.BlockSpec((GMLP_BLOCK, N_HEADS), lambda i: (0, 0))],
        out_specs=pl.BlockSpec((tm, D_MODEL), lambda i: (i, 0)),
        out_shape=jax.ShapeDtypeStruct((t, D_MODEL), _BF),
        compiler_params=_cparams(("parallel",)), name="sgu",
    )(z, z, g, b, w_sp, bsp_t)


def _out_ln_body(y_ref, w_ref, x_ref, g_ref, b_ref, o_ref, ob_ref, op_ref, stage):
    h = jnp.dot(y_ref[...], w_ref[...], preferred_element_type=_F32)
    r = _layer_norm(ALPHA * x_ref[...] + h, g_ref[...], b_ref[...])
    o_ref[...] = r
    ob_ref[...] = r.astype(_BF)
    _slab_store(op_ref, stage, r.shape[0], r)


def _out_ln(y, w, x_res, g, b, tm=256):
    t = y.shape[0]
    row = lambda i: (i, 0)
    fixed = lambda i: (0, 0)
    return pl.pallas_call(
        _out_ln_body,
        grid=(t // tm,),
        in_specs=[pl.BlockSpec((tm, D_MODEL), row),
                  pl.BlockSpec((D_MODEL, D_MODEL), fixed),
                  pl.BlockSpec((tm, D_MODEL), row),
                  pl.BlockSpec((1, D_MODEL), fixed),
                  pl.BlockSpec((1, D_MODEL), fixed)],
        out_specs=[pl.BlockSpec((tm, D_MODEL), row), pl.BlockSpec((tm, D_MODEL), row),
                   pl.BlockSpec((tm * PACK_ROWS, HEAD_DIM), row)],
        out_shape=[jax.ShapeDtypeStruct((t, D_MODEL), _F32),
                   jax.ShapeDtypeStruct((t, D_MODEL), _BF),
                   jax.ShapeDtypeStruct((t * PACK_ROWS, HEAD_DIM), _BF)],
        scratch_shapes=[pltpu.VMEM((PACK_ROWS * (tm + STAGE_PAD), HEAD_DIM), _F32)],
        compiler_params=_cparams(("parallel",)), name="out_ln",
    )(y, w, x_res, g, b)


def _rec_constants():
    c = REC_CHUNK
    t = np.arange(c)[:, None]
    j = np.arange(c)[None, :]
    tri = (j <= t).astype(np.float32)
    pair = []
    for lv in range(REC_LEVELS):
        m = c >> (lv + 1)
        same_block = (t // (2 * m)) == (j // (2 * m))
        pair.append(same_block & (t % (2 * m) >= m) & (j % (2 * m) < m))
    pair = np.concatenate(pair, 0).astype(np.float32)
    return tri, pair


def _mid_rows(b, m, row):
    c = b.shape[0]
    if 2 * m >= 8:
        return jnp.concatenate([jnp.broadcast_to(b[s + m - 1:s + m, :], (2 * m, HEAD_DIM))
                                for s in range(0, c, 2 * m)], axis=0)
    prev1 = pltpu.roll(b, 1, 0)
    if m == 1:
        return jnp.where(row % 2 == 1, prev1, b)
    p = row % 4
    return jnp.where(p == 0, pltpu.roll(b, c - 1, 0),
                     jnp.where(p == 1, b, jnp.where(p == 2, prev1, pltpu.roll(b, 2, 0))))


def _rec_body(tri_ref, pair_ref, q_ref, lf_ref, k_ref, v_ref, gs_ref, gn_ref,
              o_ref, st_ref, *, hpb):
    c = REC_CHUNK
    nt = (((1,), (1,)), ((), ()))

    @pl.when(pl.program_id(2) == 0)
    def _():
        st_ref[...] = jnp.zeros_like(st_ref)

    tri = tri_ref[...]
    eye = (lax.broadcasted_iota(jnp.int32, (c, c), 0) ==
           lax.broadcasted_iota(jnp.int32, (c, c), 1))
    row = lax.broadcasted_iota(jnp.int32, (c, HEAD_DIM), 0)
    heads = range(hpb)
    cols = [slice(h * HEAD_DIM, (h + 1) * HEAD_DIM) for h in heads]
    q = [q_ref[:, cs].astype(_F32) for cs in cols]
    k = [k_ref[:, cs].astype(_F32) for cs in cols]
    v = [v_ref[:, cs] for cs in cols]

    b = []
    for cs in cols:
        lf = lf_ref[:, cs]
        l1 = lf.astype(_BF)
        r1 = lf - l1.astype(_F32)
        l2 = r1.astype(_BF)
        l3 = (r1 - l2.astype(_F32)).astype(_BF)
        b3 = jnp.dot(tri, jnp.concatenate([l1, l2, l3], axis=1), preferred_element_type=_F32)
        b.append(b3[:, :HEAD_DIM] + b3[:, HEAD_DIM:2 * HEAD_DIM] + b3[:, 2 * HEAD_DIM:])
    b_last = [bh[c - 1:c, :] for bh in b]
    st = [st_ref[h] for h in heads]
    o = [lax.dot_general((q[h] * jnp.exp(b[h])).astype(_BF), st[h].astype(_BF), nt,
                         preferred_element_type=_F32) for h in heads]
    scores = [jnp.where(eye, jnp.sum(q[h] * k[h], axis=1, keepdims=True), 0.0) for h in heads]
    for lv in range(REC_LEVELS):
        pair = pair_ref[lv * c:(lv + 1) * c, :]
        for h in heads:
            e = jnp.exp(-jnp.abs(b[h] - _mid_rows(b[h], c >> (lv + 1), row)))
            s = lax.dot_general((q[h] * e).astype(_BF), (k[h] * e).astype(_BF), nt,
                                preferred_element_type=_F32)
            scores[h] = scores[h] + s * pair
    for h in heads:
        o[h] = o[h] + jnp.dot(scores[h].astype(_BF), v[h], preferred_element_type=_F32)
        k_dec = (k[h] * jnp.exp(b_last[h] - b[h])).astype(_BF)
        upd = lax.dot_general(v[h], k_dec, (((0,), (0,)), ((), ())), preferred_element_type=_F32)
        st_ref[h] = jnp.exp(b_last[h]) * st[h] + upd
    for h in heads:
        ms = jnp.mean(o[h] * o[h], axis=1, keepdims=True)
        oh = o[h] * lax.rsqrt(ms + LN_EPS) * gn_ref[...]
        o_ref[:, cols[h]] = (oh * gs_ref[:, cols[h]].astype(_F32)).astype(_BF)


def _recurrence(q_s, log_f, k, v, g_s, g_norm, bsz, seq, hpb=8):
    c = REC_CHUNK
    tri, pair = _rec_constants()
    tri = jnp.asarray(tri, _BF)
    pair = jnp.asarray(pair, _F32)
    w = hpb * HEAD_DIM
    n_c = seq // c
    fixed = lambda b, h, s: (0, 0)
    tile = lambda b, h, s: (b * n_c + s, h)
    return pl.pallas_call(
        functools.partial(_rec_body, hpb=hpb),
        grid=(bsz, N_HEADS // hpb, n_c),
        in_specs=[pl.BlockSpec(tri.shape, fixed), pl.BlockSpec(pair.shape, fixed),
                  pl.BlockSpec((c, w), tile), pl.BlockSpec((c, w), tile),
                  pl.BlockSpec((c, w), tile), pl.BlockSpec((c, w), tile),
                  pl.BlockSpec((c, w), tile), pl.BlockSpec((1, HEAD_DIM), fixed)],
        out_specs=pl.BlockSpec((c, w), tile),
        out_shape=jax.ShapeDtypeStruct((bsz * seq, D_MODEL), _BF),
        scratch_shapes=[pltpu.VMEM((hpb, HEAD_DIM, HEAD_DIM), _F32)],
        compiler_params=_cparams(("parallel", "parallel", "arbitrary")), name="hgrn_rec",
    )(tri, pair, q_s, log_f, k, v, g_s, g_norm)


def _router_body(x_ref, wr_ref, bias_ref, eidx_ref, gw_ref, rank_ref, cnt_ref, carry_ref, *, tm):
    @pl.when(pl.program_id(0) == 0)
    def _():
        carry_ref[...] = jnp.zeros_like(carry_ref)

    neg = -jnp.inf
    logits = lax.dot_general(wr_ref[...], x_ref[...], (((1,), (1,)), ((), ())),
                             precision=lax.Precision.HIGHEST,
                             preferred_element_type=_F32)
    scores = _sigmoid(logits)
    choice = scores + bias_ref[...]
    c3 = choice.reshape(N_GROUPS, GROUP_SIZE, tm)
    i3 = lax.broadcasted_iota(jnp.int32, c3.shape, 1)
    m1 = jnp.max(c3, axis=1, keepdims=True)
    first = jnp.min(jnp.where(c3 == m1, i3, GROUP_SIZE), axis=1, keepdims=True)
    m2 = jnp.max(jnp.where(i3 == first, neg, c3), axis=1, keepdims=True)
    gs = (m1 + m2).reshape(N_GROUPS, tm)
    ig = lax.broadcasted_iota(jnp.int32, gs.shape, 0)
    gsel = jnp.zeros(gs.shape, jnp.bool_)
    for _ in range(TOPK_GROUPS):
        m = jnp.max(gs, axis=0, keepdims=True)
        gi = jnp.min(jnp.where(gs == m, ig, N_GROUPS), axis=0, keepdims=True)
        hit = ig == gi
        gsel = gsel | hit
        gs = jnp.where(hit, neg, gs)
    allowed = jnp.broadcast_to(gsel.reshape(N_GROUPS, 1, tm), c3.shape).reshape(N_EXPERTS, tm)
    masked = jnp.where(allowed, choice, neg)
    ie = lax.broadcasted_iota(jnp.int32, masked.shape, 0)
    picked = jnp.zeros(masked.shape, _F32)
    hits, e_rows, w_rows = [], [], []
    for _ in range(TOP_K):
        m = jnp.max(masked, axis=0, keepdims=True)
        ei = jnp.min(jnp.where(masked == m, ie, N_EXPERTS), axis=0, keepdims=True)
        hit = ie == ei
        hits.append(hit)
        e_rows.append(ei)
        w_rows.append(jnp.sum(jnp.where(hit, scores, 0.0), axis=0, keepdims=True))
        picked = picked + hit.astype(_F32)
        masked = jnp.where(hit, neg, masked)
    gw = jnp.concatenate(w_rows, axis=0)
    gw = gw / jnp.sum(gw, axis=0, keepdims=True) * ROUTED_SCALE
    before = (lax.broadcasted_iota(jnp.int32, (tm, tm), 0) <
              lax.broadcasted_iota(jnp.int32, (tm, tm), 1)).astype(_BF)
    cum = jnp.dot(picked.astype(_BF), before, preferred_element_type=_F32) + carry_ref[...]
    r_rows = [jnp.sum(jnp.where(h, cum, 0.0), axis=0, keepdims=True) for h in hits]
    carry = carry_ref[...] + jnp.sum(picked, axis=1, keepdims=True)
    carry_ref[...] = carry
    eidx_ref[...] = jnp.concatenate(e_rows, axis=0)
    gw_ref[...] = gw
    rank_ref[...] = jnp.concatenate(r_rows, axis=0).astype(jnp.int32)
    cnt_ref[...] = carry.astype(jnp.int32)


def _router(x, wr_t, bias_col, tm=512):
    t = x.shape[0]
    tok = lambda i: (0, i)
    fixed = lambda i: (0, 0)
    return pl.pallas_call(
        functools.partial(_router_body, tm=tm),
        grid=(t // tm,),
        in_specs=[pl.BlockSpec((tm, D_MODEL), lambda i: (i, 0)),
                  pl.BlockSpec((N_EXPERTS, D_MODEL), fixed),
                  pl.BlockSpec((N_EXPERTS, 1), fixed)],
        out_specs=[pl.BlockSpec((TOP_K, tm), tok), pl.BlockSpec((TOP_K, tm), tok),
                   pl.BlockSpec((TOP_K, tm), tok), pl.BlockSpec((N_EXPERTS, 1), fixed)],
        out_shape=[jax.ShapeDtypeStruct((TOP_K, t), jnp.int32),
                   jax.ShapeDtypeStruct((TOP_K, t), _F32),
                   jax.ShapeDtypeStruct((TOP_K, t), jnp.int32),
                   jax.ShapeDtypeStruct((N_EXPERTS, 1), jnp.int32)],
        scratch_shapes=[pltpu.VMEM((N_EXPERTS, 1), _F32)],
        compiler_params=_cparams(("arbitrary",)), name="router",
    )(x, wr_t, bias_col)


BLOCK_WORDS = ROW_BLOCK * PACK_ROWS
EXPERT_SPLIT = 512


def _dispatch_body(pend_ref, dest_ref, xp_ref, xs_hbm, zbuf, sem, *, tm):
    i = pl.program_id(0)

    def block_copy(b):
        start = pl.multiple_of(b * BLOCK_WORDS, BLOCK_WORDS)
        return pltpu.make_async_copy(zbuf, xs_hbm.at[pl.ds(start, BLOCK_WORDS), :], sem.at[1])

    def pad_copy(e):
        return block_copy(pend_ref[e] // ROW_BLOCK - 1)

    def has_rows(e):
        return pend_ref[e] > jnp.where(e > 0, pend_ref[jnp.maximum(e - 1, 0)], 0)

    @pl.when(i == 0)
    def _():
        zbuf[...] = jnp.zeros_like(zbuf)

        def start(e, carry):
            @pl.when(has_rows(e))
            def _():
                pad_copy(e).start()
            return carry
        lax.fori_loop(0, N_EXPERTS, start, 0)

        def wait(e, carry):
            @pl.when(has_rows(e))
            def _():
                pad_copy(e).wait()
            return carry
        lax.fori_loop(0, N_EXPERTS, wait, 0)

        first_unused = pend_ref[N_EXPERTS - 1] // ROW_BLOCK
        n_blocks = xs_hbm.shape[0] // BLOCK_WORDS

        def tail_start(b, carry):
            block_copy(b).start()
            return carry
        lax.fori_loop(first_unused, n_blocks, tail_start, 0)

        def tail_wait(b, carry):
            block_copy(b).wait()
            return carry
        lax.fori_loop(first_unused, n_blocks, tail_wait, 0)

    def push(t, carry):
        src = xp_ref.at[pl.ds(pl.multiple_of(t * PACK_ROWS, PACK_ROWS), PACK_ROWS), :]
        for k in range(TOP_K):
            row = pl.multiple_of(dest_ref[k, t] * PACK_ROWS, PACK_ROWS)
            pltpu.make_async_copy(src, xs_hbm.at[pl.ds(row, PACK_ROWS), :], sem.at[0]).start()
        return carry
    lax.fori_loop(0, tm, push, 0)
    for k in range(TOP_K):
        pltpu.make_async_copy(xp_ref, xs_hbm.at[pl.ds(0, tm * PACK_ROWS), :], sem.at[0]).wait()


def _dispatch(pad_end, dest, xp, n_rows, tm=256):
    n_tiles = xp.shape[0] // (tm * PACK_ROWS)
    grid_spec = pltpu.PrefetchScalarGridSpec(
        num_scalar_prefetch=1, grid=(n_tiles,),
        in_specs=[pl.BlockSpec((TOP_K, tm), lambda i, pe: (0, i), memory_space=pltpu.SMEM),
                  pl.BlockSpec((tm * PACK_ROWS, HEAD_DIM), lambda i, pe: (i, 0))],
        out_specs=pl.BlockSpec(memory_space=pl.ANY),
        scratch_shapes=[pltpu.VMEM((BLOCK_WORDS, HEAD_DIM), _BF),
                        pltpu.SemaphoreType.DMA((2,))])
    return pl.pallas_call(
        functools.partial(_dispatch_body, tm=tm), grid_spec=grid_spec,
        out_shape=jax.ShapeDtypeStruct((n_rows * PACK_ROWS, HEAD_DIM), _BF),
        compiler_params=_cparams(("arbitrary",)), name="dispatch",
    )(pad_end, dest, xp)


def _expert_body(be_ref, nv_ref, nu_ref, x_ref, wg_ref, wu_ref, wd_ref, y_ref,
                 wg_b, wu_b, wd_b, stage):
    i = pl.program_id(0)
    n_valid = nv_ref[i]
    new_expert = (i == 0) | (be_ref[i] != be_ref[jnp.maximum(i - 1, 0)])

    @pl.when(new_expert & (n_valid > 0))
    def _():
        _cast_rows(wg_ref.at[0, 0], wg_b)
        _cast_rows(wu_ref.at[0, 0], wu_b)
        _cast_rows(wd_ref.at[0, 0], wd_b)

    @pl.when(n_valid > 0)
    def _():
        stage[0:BLOCK_WORDS, :] = x_ref[...].astype(_F32)
        n_slab = EXPERT_SPLIT // HEAD_DIM
        hg = None
        hu = None
        for c in range(D_MODEL // EXPERT_SPLIT):
            ks = slice(c * EXPERT_SPLIT, (c + 1) * EXPERT_SPLIT)
            xc = jnp.concatenate(
                [stage[pl.ds(c * n_slab + j, ROW_BLOCK, stride=PACK_ROWS), :]
                 for j in range(n_slab)], axis=1).astype(_BF)
            pg = jnp.dot(xc, wg_b[ks, :], preferred_element_type=_F32)
            pu = jnp.dot(xc, wu_b[ks, :], preferred_element_type=_F32)
            hg = pg if hg is None else hg + pg
            hu = pu if hu is None else hu + pu
        hh = (_silu(hg) * hu).astype(_BF)
        pitch = ROW_BLOCK + STAGE_PAD
        for c in range(D_MODEL // EXPERT_SPLIT):
            yc = jnp.dot(hh, wd_b[:, c * EXPERT_SPLIT:(c + 1) * EXPERT_SPLIT],
                         preferred_element_type=_F32)
            for j in range(n_slab):
                jj = c * n_slab + j
                stage[jj * pitch:jj * pitch + ROW_BLOCK, :] = yc[:, j * HEAD_DIM:(j + 1) * HEAD_DIM]
        for r in range(ROW_BLOCK):
            y_ref[r * PACK_ROWS:(r + 1) * PACK_ROWS, :] = stage[pl.ds(r, PACK_ROWS, stride=pitch), :]

    @pl.when(n_valid == 0)
    def _():
        y_ref[...] = jnp.zeros_like(y_ref)


def _experts(blk_expert, n_valid, n_used, xs, wg, wu, wd, layer):
    n_blocks = xs.shape[0] // BLOCK_WORDS
    x_map = lambda i, be, nv, nu: (jnp.minimum(i, nu[0] - 1), 0)
    w_map = lambda i, be, nv, nu: (layer, be[i], 0, 0)
    grid_spec = pltpu.PrefetchScalarGridSpec(
        num_scalar_prefetch=3, grid=(n_blocks,),
        in_specs=[pl.BlockSpec((BLOCK_WORDS, HEAD_DIM), x_map),
                  pl.BlockSpec((1, 1, D_MODEL, EXPERT_DIM), w_map),
                  pl.BlockSpec((1, 1, D_MODEL, EXPERT_DIM), w_map),
                  pl.BlockSpec((1, 1, EXPERT_DIM, D_MODEL), w_map)],
        out_specs=pl.BlockSpec((BLOCK_WORDS, HEAD_DIM), lambda i, be, nv, nu: (i, 0)),
        scratch_shapes=[pltpu.VMEM((D_MODEL, EXPERT_DIM), _BF),
                        pltpu.VMEM((D_MODEL, EXPERT_DIM), _BF),
                        pltpu.VMEM((EXPERT_DIM, D_MODEL), _BF),
                        pltpu.VMEM((PACK_ROWS * (ROW_BLOCK + STAGE_PAD), HEAD_DIM), _F32)])
    return pl.pallas_call(
        _expert_body, grid_spec=grid_spec,
        out_shape=jax.ShapeDtypeStruct(xs.shape, _F32),
        compiler_params=_cparams(("arbitrary",)), name="experts",
    )(blk_expert, n_valid, n_used, xs, wg, wu, wd)


COMBINE_ROWS = 32


def _combine_body(dcur_ref, dnxt_ref, y_hbm, gw_ref, x_ref, xb_ref, wg_ref, wu_ref, wd_ref,
                  g_ref, b_ref, o_ref, ob_ref, buf, sem, *, tm, n_tiles):
    i = pl.program_id(0)
    slot = i % 2

    def gather(d_ref, s):
        def row(t, carry):
            for k in range(TOP_K):
                src = pl.multiple_of(d_ref[k, t] * PACK_ROWS, PACK_ROWS)
                dst = pl.multiple_of((k * tm + t) * PACK_ROWS, PACK_ROWS)
                pltpu.make_async_copy(y_hbm.at[pl.ds(src, PACK_ROWS), :],
                                      buf.at[s, pl.ds(dst, PACK_ROWS), :], sem.at[s]).start()
            return carry
        lax.fori_loop(0, tm, row, 0)

    @pl.when(i == 0)
    def _():
        gather(dcur_ref, 0)

    @pl.when(i + 1 < n_tiles)
    def _():
        gather(dnxt_ref, 1 - slot)

    xb = xb_ref[...]
    hg = jnp.dot(xb, wg_ref[...], preferred_element_type=_F32)
    hu = jnp.dot(xb, wu_ref[...], preferred_element_type=_F32)
    shared = jnp.dot((_silu(hg) * hu).astype(_BF), wd_ref[...], preferred_element_type=_F32)
    o_ref[...] = ALPHA * x_ref[...] + shared
    pltpu.make_async_copy(y_hbm.at[pl.ds(0, TOP_K * tm * PACK_ROWS), :], buf.at[slot],
                          sem.at[slot]).wait()
    rows_buf = buf.at[slot]

    def add_routed(c, carry):
        r0 = pl.multiple_of(c * COMBINE_ROWS, COMBINE_ROWS)
        rows = pl.ds(r0, COMBINE_ROWS)
        gate = gw_ref[rows, :]
        gates = [jnp.broadcast_to(gate[:, k:k + 1], (COMBINE_ROWS, HEAD_DIM)) for k in range(TOP_K)]
        for j in range(PACK_ROWS):
            cols = slice(j * HEAD_DIM, (j + 1) * HEAD_DIM)
            acc = o_ref[rows, cols]
            for k in range(TOP_K):
                start = (k * tm + r0) * PACK_ROWS + j
                acc = acc + gates[k] * rows_buf[pl.ds(start, COMBINE_ROWS, stride=PACK_ROWS), :]
            o_ref[rows, cols] = acc
        return carry
    lax.fori_loop(0, tm // COMBINE_ROWS, add_routed, 0)
    r = _layer_norm(o_ref[...], g_ref[...], b_ref[...])
    o_ref[...] = r
    ob_ref[...] = r.astype(_BF)


def _combine(dest, y, gw_t, x, xb, wg, wu, wd, g, b, tm=128):
    t = x.shape[0]
    nt = t // tm
    row = lambda i: (i, 0)
    fixed = lambda i: (0, 0)
    return pl.pallas_call(
        functools.partial(_combine_body, tm=tm, n_tiles=nt),
        grid=(nt,),
        in_specs=[pl.BlockSpec((TOP_K, tm), lambda i: (0, i), memory_space=pltpu.SMEM),
                  pl.BlockSpec((TOP_K, tm), lambda i: (0, jnp.minimum(i + 1, nt - 1)),
                               memory_space=pltpu.SMEM),
                  pl.BlockSpec(memory_space=pl.ANY),
                  pl.BlockSpec((tm, TOP_K), row),
                  pl.BlockSpec((tm, D_MODEL), row), pl.BlockSpec((tm, D_MODEL), row),
                  pl.BlockSpec((D_MODEL, EXPERT_DIM), fixed),
                  pl.BlockSpec((D_MODEL, EXPERT_DIM), fixed),
                  pl.BlockSpec((EXPERT_DIM, D_MODEL), fixed),
                  pl.BlockSpec((1, D_MODEL), fixed), pl.BlockSpec((1, D_MODEL), fixed)],
        out_specs=[pl.BlockSpec((tm, D_MODEL), row), pl.BlockSpec((tm, D_MODEL), row)],
        out_shape=[jax.ShapeDtypeStruct((t, D_MODEL), _F32),
                   jax.ShapeDtypeStruct((t, D_MODEL), _BF)],
        scratch_shapes=[pltpu.VMEM((2, TOP_K * tm * PACK_ROWS, HEAD_DIM), _F32),
                        pltpu.SemaphoreType.DMA((2,))],
        compiler_params=_cparams(("arbitrary",)), name="combine",
    )(dest, dest, y, gw_t, x, xb, wg, wu, wd, g, b)


def _moe(x, xb, xp, layer, w_router, e_bias, w_gate, w_up, w_down, ws_gate, ws_up, ws_down, g, b):
    t = x.shape[0]
    eidx, gw, rank, counts = _router(x, w_router.T, e_bias.reshape(N_EXPERTS, 1))
    counts = counts.reshape(N_EXPERTS)
    padded = (counts + ROW_BLOCK - 1) // ROW_BLOCK * ROW_BLOCK
    pad_end = jnp.cumsum(padded)
    pad_start = pad_end - padded
    n_blocks = t * TOP_K // ROW_BLOCK + N_EXPERTS
    expert_ids = jnp.arange(N_EXPERTS, dtype=jnp.int32)
    dest = rank + jnp.sum(jnp.where(eidx[None] == expert_ids[:, None, None],
                                    pad_start[:, None, None], 0), axis=0)
    blk_start = jnp.arange(n_blocks, dtype=jnp.int32) * ROW_BLOCK
    blk_expert = jnp.minimum(jnp.sum(pad_end[None, :] <= blk_start[:, None], axis=1),
                             N_EXPERTS - 1).astype(jnp.int32)
    n_valid = jnp.clip((pad_start + counts)[blk_expert] - blk_start, 0, ROW_BLOCK)
    n_valid = jnp.where(blk_start < pad_end[-1], n_valid, 0).astype(jnp.int32)
    n_used = (pad_end[-1:] // ROW_BLOCK).astype(jnp.int32)
    xs = _dispatch(pad_end.astype(jnp.int32), dest, xp, n_blocks * ROW_BLOCK)
    y = _experts(blk_expert, n_valid, n_used, xs, w_gate, w_up, w_down, layer)
    return _combine(dest, y, gw.T, x, xb, ws_gate.astype(_BF), ws_up.astype(_BF),
                    ws_down.astype(_BF), g.reshape(1, -1), b.reshape(1, -1))


def kernel(x, ln_mix_g, ln_mix_b, ln_ffn_g, ln_ffn_b, gmlp_w_in, gmlp_v_ln_g, gmlp_v_ln_b, gmlp_w_sp, gmlp_b_sp, gmlp_w_out, hgrn_w_in, hgrn_o_norm_g, hgrn_w_out, hgrn_lower_bounds, moe_w_router, moe_e_bias, moe_w_gate, moe_w_up, moe_w_down, moe_ws_gate, moe_ws_up, moe_ws_down):
    bsz, seq, d = x.shape
    t = bsz * seq
    row = lambda a: a.reshape(1, -1)
    xf = x.reshape(t, d)
    xb = xf.astype(_BF)

    z = _proj(xb, gmlp_w_in[0], _epi_gelu, [_BF], 2 * d, name="gmlp_in")[0]
    y = _sgu(z, row(gmlp_v_ln_g[0]), row(gmlp_v_ln_b[0]), gmlp_w_sp[0], gmlp_b_sp[0].T)
    xf, xb, xp = _out_ln(y, gmlp_w_out[0].astype(_BF), xf, row(ln_mix_g[0]), row(ln_mix_b[0]))
    xf, xb = _moe(xf, xb, xp, 0, moe_w_router[0], moe_e_bias[0], moe_w_gate, moe_w_up,
                     moe_w_down, moe_ws_gate[0], moe_ws_up[0], moe_ws_down[0],
                     ln_ffn_g[0], ln_ffn_b[0])

    lb_soft = jax.nn.softmax(hgrn_lower_bounds.astype(_F32), axis=0)
    lb = (jnp.cumsum(lb_soft, axis=0) - lb_soft[0])[1]
    w_in = hgrn_w_in[0]
    q_s = _proj(xb, w_in, _epi_silu, [_BF], d, col0=0, name="hgrn_q")[0]
    log_f, kk = _proj(xb, w_in, _epi_forget, [_F32, _BF], d, col0=d,
                      vecs=(row(jnp.log(lb)), row(jnp.log1p(-lb)), row(1.0 - lb)), name="hgrn_f")
    vv = _proj(xb, w_in, _epi_id, [_BF], d, col0=2 * d, name="hgrn_i")[0]
    g_s = _proj(xb, w_in, _epi_silu, [_BF], d, col0=3 * d, name="hgrn_g")[0]
    o = _recurrence(q_s, log_f, kk, vv, g_s, row(hgrn_o_norm_g[0]), bsz, seq)
    xf, xb, xp = _out_ln(o, hgrn_w_out[0].astype(_BF), xf, row(ln_mix_g[1]), row(ln_mix_b[1]))
    xf, xb = _moe(xf, xb, xp, 1, moe_w_router[1], moe_e_bias[1], moe_w_gate, moe_w_up,
                     moe_w_down, moe_ws_gate[1], moe_ws_up[1], moe_ws_down[1],
                     ln_ffn_g[1], ln_ffn_b[1])
    return xf.reshape(bsz, seq, d)
```

```python
import functools
import math

import numpy as np
import jax
import jax.numpy as jnp
from jax import lax
from jax.experimental import pallas as pl
from jax.experimental.pallas import tpu as pltpu

D_MODEL = 2048
N_HEADS = 16
HEAD_DIM = 128
GMLP_BLOCK = 128
STREAM_CHUNK = 64
N_EXPERTS = 64
TOP_K = 8
N_GROUPS = 8
GROUP_SIZE = N_EXPERTS // N_GROUPS
TOPK_GROUPS = 4
EXPERT_DIM = 512
ROUTED_SCALE = 2.5
LN_EPS = 1e-5
DEPTH = 2
ALPHA = (2 * DEPTH) ** 0.25

ROW_BLOCK = 256
PACK_ROWS = D_MODEL // HEAD_DIM
REC_CHUNK = 128
REC_LEVELS = int(math.log2(REC_CHUNK))
VMEM_LIMIT = 56 * 1024 * 1024

_BF = jnp.bfloat16
_F32 = jnp.float32


def _cparams(sem):
    return pltpu.CompilerParams(dimension_semantics=sem, vmem_limit_bytes=VMEM_LIMIT)


def _sigmoid(x):
    return 1.0 / (1.0 + jnp.exp(-x))


def _silu(x):
    return x * _sigmoid(x)


STAGE_PAD = 8


def _slab_store(ref, stage, m, val):
    pitch = m + STAGE_PAD
    for j in range(PACK_ROWS):
        stage[j * pitch:j * pitch + m, :] = val[:, j * HEAD_DIM:(j + 1) * HEAD_DIM]

    for r in range(m):
        slab = stage[pl.ds(r, PACK_ROWS, stride=pitch), :]
        ref[r * PACK_ROWS:(r + 1) * PACK_ROWS, :] = slab.astype(_BF)


def _layer_norm(x, g, b):
    mu = jnp.mean(x, axis=-1, keepdims=True)
    xc = x - mu
    var = jnp.mean(xc * xc, axis=-1, keepdims=True)
    return xc * lax.rsqrt(var + LN_EPS) * g + b


CAST_ROWS = 256


def _cast_rows(src, dst):
    def step(i, carry):
        rows = pl.ds(pl.multiple_of(i * CAST_ROWS, CAST_ROWS), CAST_ROWS)
        dst[rows, :] = src[rows, :].astype(_BF)
        return carry
    lax.fori_loop(0, src.shape[0] // CAST_ROWS, step, 0)


def _proj_body(epi, n_vec, n_out, x_ref, w_ref, *refs):
    vecs = [r[...] for r in refs[:n_vec]]
    outs = refs[n_vec:n_vec + n_out]
    wb_ref = refs[n_vec + n_out]

    @pl.when(pl.program_id(1) == 0)
    def _():
        _cast_rows(w_ref, wb_ref)

    acc = jnp.dot(x_ref[...], wb_ref[...], preferred_element_type=_F32)
    res = epi(acc, *vecs)
    for o_ref, r in zip(outs, res):
        o_ref[...] = r.astype(o_ref.dtype)


def _proj(x, w, epi, out_dtypes, n, col0=0, vecs=(), tm=512, tn=1024, name="proj"):
    m, k = x.shape
    grid = (n // tn, m // tm)
    c0 = col0 // tn
    in_specs = [pl.BlockSpec((tm, k), lambda j, i: (i, 0)),
                pl.BlockSpec((k, tn), lambda j, i: (0, c0 + j))]
    in_specs += [pl.BlockSpec((1, tn), lambda j, i: (0, j)) for _ in vecs]
    out_specs = [pl.BlockSpec((tm, tn), lambda j, i: (i, j)) for _ in out_dtypes]
    out_shape = [jax.ShapeDtypeStruct((m, n), dt) for dt in out_dtypes]
    return pl.pallas_call(
        functools.partial(_proj_body, epi, len(vecs), len(out_dtypes)),
        grid=grid, in_specs=in_specs, out_specs=out_specs, out_shape=out_shape,
        scratch_shapes=[pltpu.VMEM((k, tn), _BF)],
        compiler_params=_cparams(("arbitrary", "arbitrary")), name=name,
    )(x, w, *vecs)


def _epi_gelu(acc):
    return (0.5 * acc * (1.0 + lax.erf(acc * (1.0 / math.sqrt(2.0)))),)


def _epi_silu(acc):
    return (_silu(acc),)


def _epi_id(acc):
    return (acc,)


def _epi_forget(acc, log_lb, log1m_lb, one_m_lb):
    e = jnp.exp(-jnp.abs(acc))
    ls = jnp.minimum(acc, 0.0) - jnp.log(1.0 + e)
    c = log1m_lb + ls
    log_f = jnp.maximum(log_lb, c) + jnp.log(1.0 + jnp.exp(-jnp.abs(log_lb - c)))
    inv = 1.0 / (1.0 + e)
    k = one_m_lb * jnp.where(acc > 0.0, e * inv, inv)
    return log_f, k


def _sgu_body(u_ref, v_ref, g_ref, b_ref, wsp_ref, bsp_ref, y_ref, *, n_blk):
    vn = _layer_norm(v_ref[...].astype(_F32), g_ref[...], b_ref[...]).astype(_BF)
    ri = lax.broadcasted_iota(jnp.int32, (GMLP_BLOCK, GMLP_BLOCK), 0) // STREAM_CHUNK
    ci = lax.broadcasted_iota(jnp.int32, (GMLP_BLOCK, GMLP_BLOCK), 1) // STREAM_CHUNK
    causal = ri >= ci
    for h in range(N_HEADS):
        w = jnp.where(causal, wsp_ref[h], 0.0).astype(_BF)
        bias = bsp_ref[:, h:h + 1]
        cs = slice(h * HEAD_DIM, (h + 1) * HEAD_DIM)
        for n in range(n_blk):
            rs = slice(n * GMLP_BLOCK, (n + 1) * GMLP_BLOCK)
            sv = jnp.dot(w, vn[rs, cs], preferred_element_type=_F32) + bias
            y_ref[rs, cs] = (u_ref[rs, cs].astype(_F32) * sv).astype(_BF)


def _sgu(z, g, b, w_sp, bsp_t, tm=256):
    t = z.shape[0]
    return pl.pallas_call(
        functools.partial(_sgu_body, n_blk=tm // GMLP_BLOCK),
        grid=(t // tm,),
        in_specs=[pl.BlockSpec((tm, D_MODEL), lambda i: (i, 0)),
                  pl.BlockSpec((tm, D_MODEL), lambda i: (i, 1)),
                  pl.BlockSpec((1, D_MODEL), lambda i: (0, 0)),
                  pl.BlockSpec((1, D_MODEL), lambda i: (0, 0)),
                  pl.BlockSpec((N_HEADS, GMLP_BLOCK, GMLP_BLOCK), lambda i: (0, 0, 0)),
                  pl.BlockSpec((GMLP_BLOCK, N_HEADS), lambda i: (0, 0))],
        out_specs=pl.BlockSpec((tm, D_MODEL), lambda i: (i, 0)),
        out_shape=jax.ShapeDtypeStruct((t, D_MODEL), _BF),
        compiler_params=_cparams(("parallel",)), name="sgu",
    )(z, z, g, b, w_sp, bsp_t)


def _out_ln_body(y_ref, w_ref, x_ref, g_ref, b_ref, o_ref, ob_ref, op_ref, stage):
    h = jnp.dot(y_ref[...], w_ref[...], preferred_element_type=_F32)
    r = _layer_norm(ALPHA * x_ref[...] + h, g_ref[...], b_ref[...])
    o_ref[...] = r
    ob_ref[...] = r.astype(_BF)
    _slab_store(op_ref, stage, r.shape[0], r)


def _out_ln(y, w, x_res, g, b, tm=256):
    t = y.shape[0]
    row = lambda i: (i, 0)
    fixed = lambda i: (0, 0)
    return pl.pallas_call(
        _out_ln_body,
        grid=(t // tm,),
        in_specs=[pl.BlockSpec((tm, D_MODEL), row),
                  pl.BlockSpec((D_MODEL, D_MODEL), fixed),
                  pl.BlockSpec((tm, D_MODEL), row),
                  pl.BlockSpec((1, D_MODEL), fixed),
                  pl.BlockSpec((1, D_MODEL), fixed)],
        out_specs=[pl.BlockSpec((tm, D_MODEL), row), pl.BlockSpec((tm, D_MODEL), row),
                   pl.BlockSpec((tm * PACK_ROWS, HEAD_DIM), row)],
        out_shape=[jax.ShapeDtypeStruct((t, D_MODEL), _F32),
                   jax.ShapeDtypeStruct((t, D_MODEL), _BF),
                   jax.ShapeDtypeStruct((t * PACK_ROWS, HEAD_DIM), _BF)],
        scratch_shapes=[pltpu.VMEM((PACK_ROWS * (tm + STAGE_PAD), HEAD_DIM), _F32)],
        compiler_params=_cparams(("parallel",)), name="out_ln",
    )(y, w, x_res, g, b)


def _rec_constants():
    c = REC_CHUNK
    t = np.arange(c)[:, None]
    j = np.arange(c)[None, :]
    tri = (j <= t).astype(np.float32)
    pair = []
    for lv in range(REC_LEVELS):
        m = c >> (lv + 1)
        same_block = (t // (2 * m)) == (j // (2 * m))
        pair.append(same_block & (t % (2 * m) >= m) & (j % (2 * m) < m))
    pair = np.concatenate(pair, 0).astype(np.float32)
    return tri, pair


def _mid_rows(b, m, row):
    c = b.shape[0]
    if 2 * m >= 8:
        return jnp.concatenate([jnp.broadcast_to(b[s + m - 1:s + m, :], (2 * m, HEAD_DIM))
                                for s in range(0, c, 2 * m)], axis=0)
    prev1 = pltpu.roll(b, 1, 0)
    if m == 1:
        return jnp.where(row % 2 == 1, prev1, b)
    p = row % 4
    return jnp.where(p == 0, pltpu.roll(b, c - 1, 0),
                     jnp.where(p == 1, b, jnp.where(p == 2, prev1, pltpu.roll(b, 2, 0))))


def _rec_body(tri_ref, pair_ref, q_ref, lf_ref, k_ref, v_ref, gs_ref, gn_ref,
              o_ref, st_ref, *, hpb):
    c = REC_CHUNK
    nt = (((1,), (1,)), ((), ()))

    @pl.when(pl.program_id(2) == 0)
    def _():
        st_ref[...] = jnp.zeros_like(st_ref)

    tri = tri_ref[...]
    eye = (lax.broadcasted_iota(jnp.int32, (c, c), 0) ==
           lax.broadcasted_iota(jnp.int32, (c, c), 1))
    row = lax.broadcasted_iota(jnp.int32, (c, HEAD_DIM), 0)
    heads = range(hpb)
    cols = [slice(h * HEAD_DIM, (h + 1) * HEAD_DIM) for h in heads]
    q = [q_ref[:, cs].astype(_F32) for cs in cols]
    k = [k_ref[:, cs].astype(_F32) for cs in cols]
    v = [v_ref[:, cs] for cs in cols]

    b = []
    for cs in cols:
        lf = lf_ref[:, cs]
        l1 = lf.astype(_BF)
        r1 = lf - l1.astype(_F32)
        l2 = r1.astype(_BF)
        l3 = (r1 - l2.astype(_F32)).astype(_BF)
        b3 = jnp.dot(tri, jnp.concatenate([l1, l2, l3], axis=1), preferred_element_type=_F32)
        b.append(b3[:, :HEAD_DIM] + b3[:, HEAD_DIM:2 * HEAD_DIM] + b3[:, 2 * HEAD_DIM:])
    b_last = [bh[c - 1:c, :] for bh in b]
    st = [st_ref[h] for h in heads]
    o = [lax.dot_general((q[h] * jnp.exp(b[h])).astype(_BF), st[h].astype(_BF), nt,
                         preferred_element_type=_F32) for h in heads]
    scores = [jnp.where(eye, jnp.sum(q[h] * k[h], axis=1, keepdims=True), 0.0) for h in heads]
    for lv in range(REC_LEVELS):
        pair = pair_ref[lv * c:(lv + 1) * c, :]
        for h in heads:
            e = jnp.exp(-jnp.abs(b[h] - _mid_rows(b[h], c >> (lv + 1), row)))
            s = lax.dot_general((q[h] * e).astype(_BF), (k[h] * e).astype(_BF), nt,
                                preferred_element_type=_F32)
            scores[h] = scores[h] + s * pair
    for h in heads:
        o[h] = o[h] + jnp.dot(scores[h].astype(_BF), v[h], preferred_element_type=_F32)
        k_dec = (k[h] * jnp.exp(b_last[h] - b[h])).astype(_BF)
        upd = lax.dot_general(v[h], k_dec, (((0,), (0,)), ((), ())), preferred_element_type=_F32)
        st_ref[h] = jnp.exp(b_last[h]) * st[h] + upd
    for h in heads:
        ms = jnp.mean(o[h] * o[h], axis=1, keepdims=True)
        oh = o[h] * lax.rsqrt(ms + LN_EPS) * gn_ref[...]
        o_ref[:, cols[h]] = (oh * gs_ref[:, cols[h]].astype(_F32)).astype(_BF)


def _recurrence(q_s, log_f, k, v, g_s, g_norm, bsz, seq, hpb=8):
    c = REC_CHUNK
    tri, pair = _rec_constants()
    tri = jnp.asarray(tri, _BF)
    pair = jnp.asarray(pair, _F32)
    w = hpb * HEAD_DIM
    n_c = seq // c
    fixed = lambda b, h, s: (0, 0)
    tile = lambda b, h, s: (b * n_c + s, h)
    return pl.pallas_call(
        functools.partial(_rec_body, hpb=hpb),
        grid=(bsz, N_HEADS // hpb, n_c),
        in_specs=[pl.BlockSpec(tri.shape, fixed), pl.BlockSpec(pair.shape, fixed),
                  pl.BlockSpec((c, w), tile), pl.BlockSpec((c, w), tile),
                  pl.BlockSpec((c, w), tile), pl.BlockSpec((c, w), tile),
                  pl.BlockSpec((c, w), tile), pl.BlockSpec((1, HEAD_DIM), fixed)],
        out_specs=pl.BlockSpec((c, w), tile),
        out_shape=jax.ShapeDtypeStruct((bsz * seq, D_MODEL), _BF),
        scratch_shapes=[pltpu.VMEM((hpb, HEAD_DIM, HEAD_DIM), _F32)],
        compiler_params=_cparams(("parallel", "parallel", "arbitrary")), name="hgrn_rec",
    )(tri, pair, q_s, log_f, k, v, g_s, g_norm)


def _router_body(x_ref, wr_ref, bias_ref, eidx_ref, gw_ref, rank_ref, cnt_ref, carry_ref, *, tm):
    @pl.when(pl.program_id(0) == 0)
    def _():
        carry_ref[...] = jnp.zeros_like(carry_ref)

    neg = -jnp.inf
    logits = lax.dot_general(wr_ref[...], x_ref[...], (((1,), (1,)), ((), ())),
                             precision=lax.Precision.HIGHEST,
                             preferred_element_type=_F32)
    scores = _sigmoid(logits)
    choice = scores + bias_ref[...]
    c3 = choice.reshape(N_GROUPS, GROUP_SIZE, tm)
    i3 = lax.broadcasted_iota(jnp.int32, c3.shape, 1)
    m1 = jnp.max(c3, axis=1, keepdims=True)
    first = jnp.min(jnp.where(c3 == m1, i3, GROUP_SIZE), axis=1, keepdims=True)
    m2 = jnp.max(jnp.where(i3 == first, neg, c3), axis=1, keepdims=True)
    gs = (m1 + m2).reshape(N_GROUPS, tm)
    ig = lax.broadcasted_iota(jnp.int32, gs.shape, 0)
    gsel = jnp.zeros(gs.shape, jnp.bool_)
    for _ in range(TOPK_GROUPS):
        m = jnp.max(gs, axis=0, keepdims=True)
        gi = jnp.min(jnp.where(gs == m, ig, N_GROUPS), axis=0, keepdims=True)
        hit = ig == gi
        gsel = gsel | hit
        gs = jnp.where(hit, neg, gs)
    allowed = jnp.broadcast_to(gsel.reshape(N_GROUPS, 1, tm), c3.shape).reshape(N_EXPERTS, tm)
    masked = jnp.where(allowed, choice, neg)
    ie = lax.broadcasted_iota(jnp.int32, masked.shape, 0)
    picked = jnp.zeros(masked.shape, _F32)
    hits, e_rows, w_rows = [], [], []
    for _ in range(TOP_K):
        m = jnp.max(masked, axis=0, keepdims=True)
        ei = jnp.min(jnp.where(masked == m, ie, N_EXPERTS), axis=0, keepdims=True)
        hit = ie == ei
        hits.append(hit)
        e_rows.append(ei)
        w_rows.append(jnp.sum(jnp.where(hit, scores, 0.0), axis=0, keepdims=True))
        picked = picked + hit.astype(_F32)
        masked = jnp.where(hit, neg, masked)
    gw = jnp.concatenate(w_rows, axis=0)
    gw = gw / jnp.sum(gw, axis=0, keepdims=True) * ROUTED_SCALE
    before = (lax.broadcasted_iota(jnp.int32, (tm, tm), 0) <
              lax.broadcasted_iota(jnp.int32, (tm, tm), 1)).astype(_BF)
    cum = jnp.dot(picked.astype(_BF), before, preferred_element_type=_F32) + carry_ref[...]
    r_rows = [jnp.sum(jnp.where(h, cum, 0.0), axis=0, keepdims=True) for h in hits]
    carry = carry_ref[...] + jnp.sum(picked, axis=1, keepdims=True)
    carry_ref[...] = carry
    eidx_ref[...] = jnp.concatenate(e_rows, axis=0)
    gw_ref[...] = gw
    rank_ref[...] = jnp.concatenate(r_rows, axis=0).astype(jnp.int32)
    cnt_ref[...] = carry.astype(jnp.int32)


def _router(x, wr_t, bias_col, tm=512):
    t = x.shape[0]
    tok = lambda i: (0, i)
    fixed = lambda i: (0, 0)
    return pl.pallas_call(
        functools.partial(_router_body, tm=tm),
        grid=(t // tm,),
        in_specs=[pl.BlockSpec((tm, D_MODEL), lambda i: (i, 0)),
                  pl.BlockSpec((N_EXPERTS, D_MODEL), fixed),
                  pl.BlockSpec((N_EXPERTS, 1), fixed)],
        out_specs=[pl.BlockSpec((TOP_K, tm), tok), pl.BlockSpec((TOP_K, tm), tok),
                   pl.BlockSpec((TOP_K, tm), tok), pl.BlockSpec((N_EXPERTS, 1), fixed)],
        out_shape=[jax.ShapeDtypeStruct((TOP_K, t), jnp.int32),
                   jax.ShapeDtypeStruct((TOP_K, t), _F32),
                   jax.ShapeDtypeStruct((TOP_K, t), jnp.int32),
                   jax.ShapeDtypeStruct((N_EXPERTS, 1), jnp.int32)],
        scratch_shapes=[pltpu.VMEM((N_EXPERTS, 1), _F32)],
        compiler_params=_cparams(("arbitrary",)), name="router",
    )(x, wr_t, bias_col)


BLOCK_WORDS = ROW_BLOCK * PACK_ROWS
EXPERT_SPLIT = 512


def _dispatch_body(pend_ref, dest_ref, xp_ref, xb_ref, wg_ref, wu_ref, wd_ref, xs_hbm, sh_ref,
                   zbuf, sem, *, tm):
    i = pl.program_id(0)

    def block_copy(b):
        start = pl.multiple_of(b * BLOCK_WORDS, BLOCK_WORDS)
        return pltpu.make_async_copy(zbuf, xs_hbm.at[pl.ds(start, BLOCK_WORDS), :], sem.at[1])

    def pad_copy(e):
        return block_copy(pend_ref[e] // ROW_BLOCK - 1)

    def has_rows(e):
        return pend_ref[e] > jnp.where(e > 0, pend_ref[jnp.maximum(e - 1, 0)], 0)

    @pl.when(i == 0)
    def _():
        zbuf[...] = jnp.zeros_like(zbuf)

        def start(e, carry):
            @pl.when(has_rows(e))
            def _():
                pad_copy(e).start()
            return carry
        lax.fori_loop(0, N_EXPERTS, start, 0)

        def wait(e, carry):
            @pl.when(has_rows(e))
            def _():
                pad_copy(e).wait()
            return carry
        lax.fori_loop(0, N_EXPERTS, wait, 0)

        first_unused = pend_ref[N_EXPERTS - 1] // ROW_BLOCK
        n_blocks = xs_hbm.shape[0] // BLOCK_WORDS

        def tail_start(b, carry):
            block_copy(b).start()
            return carry
        lax.fori_loop(first_unused, n_blocks, tail_start, 0)

        def tail_wait(b, carry):
            block_copy(b).wait()
            return carry
        lax.fori_loop(first_unused, n_blocks, tail_wait, 0)

    def push(t, carry):
        src = xp_ref.at[pl.ds(pl.multiple_of(t * PACK_ROWS, PACK_ROWS), PACK_ROWS), :]
        for k in range(TOP_K):
            row = pl.multiple_of(dest_ref[k, t] * PACK_ROWS, PACK_ROWS)
            pltpu.make_async_copy(src, xs_hbm.at[pl.ds(row, PACK_ROWS), :], sem.at[0]).start()
        return carry
    lax.fori_loop(0, tm, push, 0)
    xb = xb_ref[...]
    hg = jnp.dot(xb, wg_ref[...], preferred_element_type=_F32)
    hu = jnp.dot(xb, wu_ref[...], preferred_element_type=_F32)
    sh_ref[...] = jnp.dot((_silu(hg) * hu).astype(_BF), wd_ref[...], preferred_element_type=_F32)
    for k in range(TOP_K):
        pltpu.make_async_copy(xp_ref, xs_hbm.at[pl.ds(0, tm * PACK_ROWS), :], sem.at[0]).wait()


def _dispatch(pad_end, dest, xp, xb, wg, wu, wd, n_rows, tm=256):
    t = xb.shape[0]
    n_tiles = t // tm
    fixed = lambda i, pe: (0, 0)
    grid_spec = pltpu.PrefetchScalarGridSpec(
        num_scalar_prefetch=1, grid=(n_tiles,),
        in_specs=[pl.BlockSpec((TOP_K, tm), lambda i, pe: (0, i), memory_space=pltpu.SMEM),
                  pl.BlockSpec((tm * PACK_ROWS, HEAD_DIM), lambda i, pe: (i, 0)),
                  pl.BlockSpec((tm, D_MODEL), lambda i, pe: (i, 0)),
                  pl.BlockSpec((D_MODEL, EXPERT_DIM), fixed),
                  pl.BlockSpec((D_MODEL, EXPERT_DIM), fixed),
                  pl.BlockSpec((EXPERT_DIM, D_MODEL), fixed)],
        out_specs=[pl.BlockSpec(memory_space=pl.ANY),
                   pl.BlockSpec((tm, D_MODEL), lambda i, pe: (i, 0))],
        scratch_shapes=[pltpu.VMEM((BLOCK_WORDS, HEAD_DIM), _BF),
                        pltpu.SemaphoreType.DMA((2,))])
    return pl.pallas_call(
        functools.partial(_dispatch_body, tm=tm), grid_spec=grid_spec,
        out_shape=[jax.ShapeDtypeStruct((n_rows * PACK_ROWS, HEAD_DIM), _BF),
                   jax.ShapeDtypeStruct((t, D_MODEL), _F32)],
        compiler_params=_cparams(("arbitrary",)), name="dispatch",
    )(pad_end, dest, xp, xb, wg, wu, wd)


def _expert_body(be_ref, nv_ref, nu_ref, seg_ref, nxt_ref, x_ref, wg_hbm, wu_hbm, wd_hbm, y_ref,
                 wg_b, wu_b, wd_b, stage, wg_f, wu_f, wd_f, sem, *, layer):
    i = pl.program_id(0)
    n_valid = nv_ref[i]
    new_expert = (i == 0) | (be_ref[i] != be_ref[jnp.maximum(i - 1, 0)])

    def weight_copies(e, s):
        return [pltpu.make_async_copy(src.at[layer, e], dst.at[s], sem.at[s])
                for src, dst in ((wg_hbm, wg_f), (wu_hbm, wu_f), (wd_hbm, wd_f))]

    @pl.when(new_expert & (n_valid > 0))
    def _():
        s = seg_ref[i] % 2

        @pl.when(i == 0)
        def _():
            for cp in weight_copies(be_ref[0], 0):
                cp.start()

        for cp in weight_copies(be_ref[i], s):
            cp.wait()

        @pl.when(nxt_ref[i] >= 0)
        def _():
            for cp in weight_copies(nxt_ref[i], 1 - s):
                cp.start()

        _cast_rows(wg_f.at[s], wg_b)
        _cast_rows(wu_f.at[s], wu_b)
        _cast_rows(wd_f.at[s], wd_b)

    @pl.when(n_valid > 0)
    def _():
        stage[0:BLOCK_WORDS, :] = x_ref[...].astype(_F32)
        n_slab = EXPERT_SPLIT // HEAD_DIM
        hg = None
        hu = None
        for c in range(D_MODEL // EXPERT_SPLIT):
            ks = slice(c * EXPERT_SPLIT, (c + 1) * EXPERT_SPLIT)
            xc = jnp.concatenate(
                [stage[pl.ds(c * n_slab + j, ROW_BLOCK, stride=PACK_ROWS), :]
                 for j in range(n_slab)], axis=1).astype(_BF)
            pg = jnp.dot(xc, wg_b[ks, :], preferred_element_type=_F32)
            pu = jnp.dot(xc, wu_b[ks, :], preferred_element_type=_F32)
            hg = pg if hg is None else hg + pg
            hu = pu if hu is None else hu + pu
        hh = (_silu(hg) * hu).astype(_BF)
        pitch = ROW_BLOCK + STAGE_PAD
        for c in range(D_MODEL // EXPERT_SPLIT):
            yc = jnp.dot(hh, wd_b[:, c * EXPERT_SPLIT:(c + 1) * EXPERT_SPLIT],
                         preferred_element_type=_F32)
            for j in range(n_slab):
                jj = c * n_slab + j
                stage[jj * pitch:jj * pitch + ROW_BLOCK, :] = yc[:, j * HEAD_DIM:(j + 1) * HEAD_DIM]
        for r in range(ROW_BLOCK):
            y_ref[r * PACK_ROWS:(r + 1) * PACK_ROWS, :] = stage[pl.ds(r, PACK_ROWS, stride=pitch), :]

    @pl.when(n_valid == 0)
    def _():
        y_ref[...] = jnp.zeros_like(y_ref)


def _experts(blk_expert, n_valid, n_used, seg, nxt, xs, wg, wu, wd, layer):
    n_blocks = xs.shape[0] // BLOCK_WORDS
    x_map = lambda i, be, nv, nu, sg, nx: (jnp.minimum(i, nu[0] - 1), 0)
    grid_spec = pltpu.PrefetchScalarGridSpec(
        num_scalar_prefetch=5, grid=(n_blocks,),
        in_specs=[pl.BlockSpec((BLOCK_WORDS, HEAD_DIM), x_map),
                  pl.BlockSpec(memory_space=pl.ANY),
                  pl.BlockSpec(memory_space=pl.ANY),
                  pl.BlockSpec(memory_space=pl.ANY)],
        out_specs=pl.BlockSpec((BLOCK_WORDS, HEAD_DIM), lambda i, be, nv, nu, sg, nx: (i, 0)),
        scratch_shapes=[pltpu.VMEM((D_MODEL, EXPERT_DIM), _BF),
                        pltpu.VMEM((D_MODEL, EXPERT_DIM), _BF),
                        pltpu.VMEM((EXPERT_DIM, D_MODEL), _BF),
                        pltpu.VMEM((PACK_ROWS * (ROW_BLOCK + STAGE_PAD), HEAD_DIM), _F32),
                        pltpu.VMEM((2, D_MODEL, EXPERT_DIM), _F32),
                        pltpu.VMEM((2, D_MODEL, EXPERT_DIM), _F32),
                        pltpu.VMEM((2, EXPERT_DIM, D_MODEL), _F32),
                        pltpu.SemaphoreType.DMA((2,))])
    return pl.pallas_call(
        functools.partial(_expert_body, layer=layer), grid_spec=grid_spec,
        out_shape=jax.ShapeDtypeStruct(xs.shape, _F32),
        compiler_params=_cparams(("arbitrary",)), name="experts",
    )(blk_expert, n_valid, n_used, seg, nxt, xs, wg, wu, wd)


COMBINE_ROWS = 32


def _combine_body(dcur_ref, dnxt_ref, y_hbm, gw_ref, x_ref, sh_ref, g_ref, b_ref,
                  o_ref, ob_ref, buf, sem, *, tm, n_tiles):
    i = pl.program_id(0)
    slot = i % 2

    def gather(d_ref, s):
        def row(t, carry):
            for k in range(TOP_K):
                src = pl.multiple_of(d_ref[k, t] * PACK_ROWS, PACK_ROWS)
                dst = pl.multiple_of((k * tm + t) * PACK_ROWS, PACK_ROWS)
                pltpu.make_async_copy(y_hbm.at[pl.ds(src, PACK_ROWS), :],
                                      buf.at[s, pl.ds(dst, PACK_ROWS), :], sem.at[s]).start()
            return carry
        lax.fori_loop(0, tm, row, 0)

    @pl.when(i == 0)
    def _():
        gather(dcur_ref, 0)

    @pl.when(i + 1 < n_tiles)
    def _():
        gather(dnxt_ref, 1 - slot)

    o_ref[...] = ALPHA * x_ref[...] + sh_ref[...]
    pltpu.make_async_copy(y_hbm.at[pl.ds(0, TOP_K * tm * PACK_ROWS), :], buf.at[slot],
                          sem.at[slot]).wait()
    rows_buf = buf.at[slot]

    def add_routed(c, carry):
        r0 = pl.multiple_of(c * COMBINE_ROWS, COMBINE_ROWS)
        rows = pl.ds(r0, COMBINE_ROWS)
        gate = gw_ref[rows, :]
        gates = [jnp.broadcast_to(gate[:, k:k + 1], (COMBINE_ROWS, HEAD_DIM)) for k in range(TOP_K)]
        for j in range(PACK_ROWS):
            cols = slice(j * HEAD_DIM, (j + 1) * HEAD_DIM)
            acc = o_ref[rows, cols]
            for k in range(TOP_K):
                start = (k * tm + r0) * PACK_ROWS + j
                acc = acc + gates[k] * rows_buf[pl.ds(start, COMBINE_ROWS, stride=PACK_ROWS), :]
            o_ref[rows, cols] = acc
        return carry
    lax.fori_loop(0, tm // COMBINE_ROWS, add_routed, 0)
    r = _layer_norm(o_ref[...], g_ref[...], b_ref[...])
    o_ref[...] = r
    ob_ref[...] = r.astype(_BF)


def _combine(dest, y, gw_t, x, shared, g, b, tm=128):
    t = x.shape[0]
    nt = t // tm
    row = lambda i: (i, 0)
    fixed = lambda i: (0, 0)
    return pl.pallas_call(
        functools.partial(_combine_body, tm=tm, n_tiles=nt),
        grid=(nt,),
        in_specs=[pl.BlockSpec((TOP_K, tm), lambda i: (0, i), memory_space=pltpu.SMEM),
                  pl.BlockSpec((TOP_K, tm), lambda i: (0, jnp.minimum(i + 1, nt - 1)),
                               memory_space=pltpu.SMEM),
                  pl.BlockSpec(memory_space=pl.ANY),
                  pl.BlockSpec((tm, TOP_K), row),
                  pl.BlockSpec((tm, D_MODEL), row), pl.BlockSpec((tm, D_MODEL), row),
                  pl.BlockSpec((1, D_MODEL), fixed), pl.BlockSpec((1, D_MODEL), fixed)],
        out_specs=[pl.BlockSpec((tm, D_MODEL), row), pl.BlockSpec((tm, D_MODEL), row)],
        out_shape=[jax.ShapeDtypeStruct((t, D_MODEL), _F32),
                   jax.ShapeDtypeStruct((t, D_MODEL), _BF)],
        scratch_shapes=[pltpu.VMEM((2, TOP_K * tm * PACK_ROWS, HEAD_DIM), _F32),
                        pltpu.SemaphoreType.DMA((2,))],
        compiler_params=_cparams(("arbitrary",)), name="combine",
    )(dest, dest, y, gw_t, x, shared, g, b)


def _moe(x, xb, xp, layer, w_router, e_bias, w_gate, w_up, w_down, ws_gate, ws_up, ws_down, g, b):
    t = x.shape[0]
    eidx, gw, rank, counts = _router(x, w_router.T, e_bias.reshape(N_EXPERTS, 1))
    counts = counts.reshape(N_EXPERTS)
    padded = (counts + ROW_BLOCK - 1) // ROW_BLOCK * ROW_BLOCK
    pad_end = jnp.cumsum(padded)
    pad_start = pad_end - padded
    n_blocks = t * TOP_K // ROW_BLOCK + N_EXPERTS
    expert_ids = jnp.arange(N_EXPERTS, dtype=jnp.int32)
    dest = rank + jnp.sum(jnp.where(eidx[None] == expert_ids[:, None, None],
                                    pad_start[:, None, None], 0), axis=0)
    blk_start = jnp.arange(n_blocks, dtype=jnp.int32) * ROW_BLOCK
    blk_expert = jnp.minimum(jnp.sum(pad_end[None, :] <= blk_start[:, None], axis=1),
                             N_EXPERTS - 1).astype(jnp.int32)
    n_valid = jnp.clip((pad_start + counts)[blk_expert] - blk_start, 0, ROW_BLOCK)
    n_valid = jnp.where(blk_start < pad_end[-1], n_valid, 0).astype(jnp.int32)
    n_used = (pad_end[-1:] // ROW_BLOCK).astype(jnp.int32)
    xs, shared = _dispatch(pad_end.astype(jnp.int32), dest, xp, xb, ws_gate.astype(_BF),
                           ws_up.astype(_BF), ws_down.astype(_BF), n_blocks * ROW_BLOCK)
    used = blk_start < pad_end[-1]
    change = jnp.concatenate([jnp.zeros((1,), jnp.int32),
                              (blk_expert[1:] != blk_expert[:-1]).astype(jnp.int32)])
    seg = jnp.cumsum(change)
    later = jnp.where(used[None, :] & (seg[None, :] == seg[:, None] + 1), blk_expert[None, :], -1)
    nxt = jnp.max(later, axis=1).astype(jnp.int32)
    y = _experts(blk_expert, n_valid, n_used, seg.astype(jnp.int32), nxt, xs,
                 w_gate, w_up, w_down, layer)
    return _combine(dest, y, gw.T, x, shared, g.reshape(1, -1), b.reshape(1, -1))


def kernel(x, ln_mix_g, ln_mix_b, ln_ffn_g, ln_ffn_b, gmlp_w_in, gmlp_v_ln_g, gmlp_v_ln_b, gmlp_w_sp, gmlp_b_sp, gmlp_w_out, hgrn_w_in, hgrn_o_norm_g, hgrn_w_out, hgrn_lower_bounds, moe_w_router, moe_e_bias, moe_w_gate, moe_w_up, moe_w_down, moe_ws_gate, moe_ws_up, moe_ws_down):
    bsz, seq, d = x.shape
    t = bsz * seq
    row = lambda a: a.reshape(1, -1)
    xf = x.reshape(t, d)
    xb = xf.astype(_BF)

    z = _proj(xb, gmlp_w_in[0], _epi_gelu, [_BF], 2 * d, name="gmlp_in")[0]
    y = _sgu(z, row(gmlp_v_ln_g[0]), row(gmlp_v_ln_b[0]), gmlp_w_sp[0], gmlp_b_sp[0].T)
    xf, xb, xp = _out_ln(y, gmlp_w_out[0].astype(_BF), xf, row(ln_mix_g[0]), row(ln_mix_b[0]))
    xf, xb = _moe(xf, xb, xp, 0, moe_w_router[0], moe_e_bias[0], moe_w_gate, moe_w_up,
                     moe_w_down, moe_ws_gate[0], moe_ws_up[0], moe_ws_down[0],
                     ln_ffn_g[0], ln_ffn_b[0])

    lb_soft = jax.nn.softmax(hgrn_lower_bounds.astype(_F32), axis=0)
    lb = (jnp.cumsum(lb_soft, axis=0) - lb_soft[0])[1]
    w_in = hgrn_w_in[0]
    q_s = _proj(xb, w_in, _epi_silu, [_BF], d, col0=0, name="hgrn_q")[0]
    log_f, kk = _proj(xb, w_in, _epi_forget, [_F32, _BF], d, col0=d,
                      vecs=(row(jnp.log(lb)), row(jnp.log1p(-lb)), row(1.0 - lb)), name="hgrn_f")
    vv = _proj(xb, w_in, _epi_id, [_BF], d, col0=2 * d, name="hgrn_i")[0]
    g_s = _proj(xb, w_in, _epi_silu, [_BF], d, col0=3 * d, name="hgrn_g")[0]
    o = _recurrence(q_s, log_f, kk, vv, g_s, row(hgrn_o_norm_g[0]), bsz, seq)
    xf, xb, xp = _out_ln(o, hgrn_w_out[0].astype(_BF), xf, row(ln_mix_g[1]), row(ln_mix_b[1]))
    xf, xb = _moe(xf, xb, xp, 1, moe_w_router[1], moe_e_bias[1], moe_w_gate, moe_w_up,
                     moe_w_down, moe_ws_gate[1], moe_ws_up[1], moe_ws_down[1],
                     ln_ffn_g[1], ln_ffn_b[1])
    return xf.reshape(bsz, seq, d)
```

```python
import functools
import math

import numpy as np
import jax
import jax.numpy as jnp
from jax import lax
from jax.experimental import pallas as pl
from jax.experimental.pallas import tpu as pltpu

D_MODEL = 2048
N_HEADS = 16
HEAD_DIM = 128
GMLP_BLOCK = 128
STREAM_CHUNK = 64
N_EXPERTS = 64
TOP_K = 8
N_GROUPS = 8
GROUP_SIZE = N_EXPERTS // N_GROUPS
TOPK_GROUPS = 4
EXPERT_DIM = 512
ROUTED_SCALE = 2.5
LN_EPS = 1e-5
DEPTH = 2
ALPHA = (2 * DEPTH) ** 0.25

ROW_BLOCK = 256
PACK_ROWS = D_MODEL // HEAD_DIM
REC_CHUNK = 128
REC_LEVELS = int(math.log2(REC_CHUNK))
VMEM_LIMIT = 56 * 1024 * 1024

_BF = jnp.bfloat16
_F32 = jnp.float32


def _cparams(sem):
    return pltpu.CompilerParams(dimension_semantics=sem, vmem_limit_bytes=VMEM_LIMIT)


def _sigmoid(x):
    return 1.0 / (1.0 + jnp.exp(-x))


def _silu(x):
    return x * _sigmoid(x)


STAGE_PAD = 8


def _slab_store(ref, stage, m, val):
    pitch = m + STAGE_PAD
    for j in range(PACK_ROWS):
        stage[j * pitch:j * pitch + m, :] = val[:, j * HEAD_DIM:(j + 1) * HEAD_DIM]

    for r in range(m):
        slab = stage[pl.ds(r, PACK_ROWS, stride=pitch), :]
        ref[r * PACK_ROWS:(r + 1) * PACK_ROWS, :] = slab.astype(_BF)


def _layer_norm(x, g, b):
    mu = jnp.mean(x, axis=-1, keepdims=True)
    xc = x - mu
    var = jnp.mean(xc * xc, axis=-1, keepdims=True)
    return xc * lax.rsqrt(var + LN_EPS) * g + b


CAST_ROWS = 256


def _cast_rows(src, dst):
    def step(i, carry):
        rows = pl.ds(pl.multiple_of(i * CAST_ROWS, CAST_ROWS), CAST_ROWS)
        dst[rows, :] = src[rows, :].astype(_BF)
        return carry
    lax.fori_loop(0, src.shape[0] // CAST_ROWS, step, 0)


def _proj_body(epi, n_vec, n_out, x_ref, w_ref, *refs):
    vecs = [r[...] for r in refs[:n_vec]]
    outs = refs[n_vec:n_vec + n_out]
    wb_ref = refs[n_vec + n_out]

    @pl.when(pl.program_id(1) == 0)
    def _():
        _cast_rows(w_ref, wb_ref)

    acc = jnp.dot(x_ref[...], wb_ref[...], preferred_element_type=_F32)
    res = epi(acc, *vecs)
    for o_ref, r in zip(outs, res):
        o_ref[...] = r.astype(o_ref.dtype)


def _proj(x, w, epi, out_dtypes, n, col0=0, vecs=(), tm=512, tn=1024, name="proj"):
    m, k = x.shape
    grid = (n // tn, m // tm)
    c0 = col0 // tn
    in_specs = [pl.BlockSpec((tm, k), lambda j, i: (i, 0)),
                pl.BlockSpec((k, tn), lambda j, i: (0, c0 + j))]
    in_specs += [pl.BlockSpec((1, tn), lambda j, i: (0, j)) for _ in vecs]
    out_specs = [pl.BlockSpec((tm, tn), lambda j, i: (i, j)) for _ in out_dtypes]
    out_shape = [jax.ShapeDtypeStruct((m, n), dt) for dt in out_dtypes]
    return pl.pallas_call(
        functools.partial(_proj_body, epi, len(vecs), len(out_dtypes)),
        grid=grid, in_specs=in_specs, out_specs=out_specs, out_shape=out_shape,
        scratch_shapes=[pltpu.VMEM((k, tn), _BF)],
        compiler_params=_cparams(("arbitrary", "arbitrary")), name=name,
    )(x, w, *vecs)


def _epi_gelu(acc):
    return (0.5 * acc * (1.0 + lax.erf(acc * (1.0 / math.sqrt(2.0)))),)


def _epi_silu(acc):
    return (_silu(acc),)


def _epi_id(acc):
    return (acc,)


def _epi_forget(acc, log_lb, log1m_lb, one_m_lb):
    e = jnp.exp(-jnp.abs(acc))
    ls = jnp.minimum(acc, 0.0) - jnp.log(1.0 + e)
    c = log1m_lb + ls
    log_f = jnp.maximum(log_lb, c) + jnp.log(1.0 + jnp.exp(-jnp.abs(log_lb - c)))
    inv = 1.0 / (1.0 + e)
    k = one_m_lb * jnp.where(acc > 0.0, e * inv, inv)
    return log_f, k


def _sgu_body(u_ref, v_ref, g_ref, b_ref, wsp_ref, bsp_ref, y_ref, *, n_blk):
    vn = _layer_norm(v_ref[...].astype(_F32), g_ref[...], b_ref[...]).astype(_BF)
    ri = lax.broadcasted_iota(jnp.int32, (GMLP_BLOCK, GMLP_BLOCK), 0) // STREAM_CHUNK
    ci = lax.broadcasted_iota(jnp.int32, (GMLP_BLOCK, GMLP_BLOCK), 1) // STREAM_CHUNK
    causal = ri >= ci
    for h in range(N_HEADS):
        w = jnp.where(causal, wsp_ref[h], 0.0).astype(_BF)
        bias = bsp_ref[:, h:h + 1]
        cs = slice(h * HEAD_DIM, (h + 1) * HEAD_DIM)
        for n in range(n_blk):
            rs = slice(n * GMLP_BLOCK, (n + 1) * GMLP_BLOCK)
            sv = jnp.dot(w, vn[rs, cs], preferred_element_type=_F32) + bias
            y_ref[rs, cs] = (u_ref[rs, cs].astype(_F32) * sv).astype(_BF)


def _sgu(z, g, b, w_sp, bsp_t, tm=256):
    t = z.shape[0]
    return pl.pallas_call(
        functools.partial(_sgu_body, n_blk=tm // GMLP_BLOCK),
        grid=(t // tm,),
        in_specs=[pl.BlockSpec((tm, D_MODEL), lambda i: (i, 0)),
                  pl.BlockSpec((tm, D_MODEL), lambda i: (i, 1)),
                  pl.BlockSpec((1, D_MODEL), lambda i: (0, 0)),
                  pl.BlockSpec((1, D_MODEL), lambda i: (0, 0)),
                  pl.BlockSpec((N_HEADS, GMLP_BLOCK, GMLP_BLOCK), lambda i: (0, 0, 0)),
                  pl.BlockSpec((GMLP_BLOCK, N_HEADS), lambda i: (0, 0))],
        out_specs=pl.BlockSpec((tm, D_MODEL), lambda i: (i, 0)),
        out_shape=jax.ShapeDtypeStruct((t, D_MODEL), _BF),
        compiler_params=_cparams(("parallel",)), name="sgu",
    )(z, z, g, b, w_sp, bsp_t)


def _out_ln_body(y_ref, w_ref, x_ref, g_ref, b_ref, o_ref, ob_ref, op_ref, stage):
    h = jnp.dot(y_ref[...], w_ref[...], preferred_element_type=_F32)
    r = _layer_norm(ALPHA * x_ref[...] + h, g_ref[...], b_ref[...])
    o_ref[...] = r
    ob_ref[...] = r.astype(_BF)
    _slab_store(op_ref, stage, r.shape[0], r)


def _out_ln(y, w, x_res, g, b, tm=256):
    t = y.shape[0]
    row = lambda i: (i, 0)
    fixed = lambda i: (0, 0)
    return pl.pallas_call(
        _out_ln_body,
        grid=(t // tm,),
        in_specs=[pl.BlockSpec((tm, D_MODEL), row),
                  pl.BlockSpec((D_MODEL, D_MODEL), fixed),
                  pl.BlockSpec((tm, D_MODEL), row),
                  pl.BlockSpec((1, D_MODEL), fixed),
                  pl.BlockSpec((1, D_MODEL), fixed)],
        out_specs=[pl.BlockSpec((tm, D_MODEL), row), pl.BlockSpec((tm, D_MODEL), row),
                   pl.BlockSpec((tm * PACK_ROWS, HEAD_DIM), row)],
        out_shape=[jax.ShapeDtypeStruct((t, D_MODEL), _F32),
                   jax.ShapeDtypeStruct((t, D_MODEL), _BF),
                   jax.ShapeDtypeStruct((t * PACK_ROWS, HEAD_DIM), _BF)],
        scratch_shapes=[pltpu.VMEM((PACK_ROWS * (tm + STAGE_PAD), HEAD_DIM), _F32)],
        compiler_params=_cparams(("parallel",)), name="out_ln",
    )(y, w, x_res, g, b)


def _rec_constants():
    c = REC_CHUNK
    t = np.arange(c)[:, None]
    j = np.arange(c)[None, :]
    tri = (j <= t).astype(np.float32)
    pair = []
    for lv in range(REC_LEVELS):
        m = c >> (lv + 1)
        same_block = (t // (2 * m)) == (j // (2 * m))
        pair.append(same_block & (t % (2 * m) >= m) & (j % (2 * m) < m))
    pair = np.concatenate(pair, 0).astype(np.float32)
    return tri, pair


def _mid_rows(b, m, row):
    c = b.shape[0]
    if 2 * m >= 8:
        return jnp.concatenate([jnp.broadcast_to(b[s + m - 1:s + m, :], (2 * m, HEAD_DIM))
                                for s in range(0, c, 2 * m)], axis=0)
    prev1 = pltpu.roll(b, 1, 0)
    if m == 1:
        return jnp.where(row % 2 == 1, prev1, b)
    p = row % 4
    return jnp.where(p == 0, pltpu.roll(b, c - 1, 0),
                     jnp.where(p == 1, b, jnp.where(p == 2, prev1, pltpu.roll(b, 2, 0))))


def _rec_body(tri_ref, pair_ref, q_ref, lf_ref, k_ref, v_ref, gs_ref, gn_ref,
              o_ref, st_ref, *, hpb):
    c = REC_CHUNK
    nt = (((1,), (1,)), ((), ()))

    @pl.when(pl.program_id(2) == 0)
    def _():
        st_ref[...] = jnp.zeros_like(st_ref)

    tri = tri_ref[...]
    eye = (lax.broadcasted_iota(jnp.int32, (c, c), 0) ==
           lax.broadcasted_iota(jnp.int32, (c, c), 1))
    row = lax.broadcasted_iota(jnp.int32, (c, HEAD_DIM), 0)
    heads = range(hpb)
    cols = [slice(h * HEAD_DIM, (h + 1) * HEAD_DIM) for h in heads]
    q = [q_ref[:, cs].astype(_F32) for cs in cols]
    k = [k_ref[:, cs].astype(_F32) for cs in cols]
    v = [v_ref[:, cs] for cs in cols]

    b = []
    for cs in cols:
        lf = lf_ref[:, cs]
        l1 = lf.astype(_BF)
        r1 = lf - l1.astype(_F32)
        l2 = r1.astype(_BF)
        l3 = (r1 - l2.astype(_F32)).astype(_BF)
        b3 = jnp.dot(tri, jnp.concatenate([l1, l2, l3], axis=1), preferred_element_type=_F32)
        b.append(b3[:, :HEAD_DIM] + b3[:, HEAD_DIM:2 * HEAD_DIM] + b3[:, 2 * HEAD_DIM:])
    b_last = [bh[c - 1:c, :] for bh in b]
    st = [st_ref[h] for h in heads]
    o = [lax.dot_general((q[h] * jnp.exp(b[h])).astype(_BF), st[h].astype(_BF), nt,
                         preferred_element_type=_F32) for h in heads]
    scores = [jnp.where(eye, jnp.sum(q[h] * k[h], axis=1, keepdims=True), 0.0) for h in heads]
    for lv in range(REC_LEVELS):
        pair = pair_ref[lv * c:(lv + 1) * c, :]
        for h in heads:
            e = jnp.exp(-jnp.abs(b[h] - _mid_rows(b[h], c >> (lv + 1), row)))
            s = lax.dot_general((q[h] * e).astype(_BF), (k[h] * e).astype(_BF), nt,
                                preferred_element_type=_F32)
            scores[h] = scores[h] + s * pair
    for h in heads:
        o[h] = o[h] + jnp.dot(scores[h].astype(_BF), v[h], preferred_element_type=_F32)
        k_dec = (k[h] * jnp.exp(b_last[h] - b[h])).astype(_BF)
        upd = lax.dot_general(v[h], k_dec, (((0,), (0,)), ((), ())), preferred_element_type=_F32)
        st_ref[h] = jnp.exp(b_last[h]) * st[h] + upd
    for h in heads:
        ms = jnp.mean(o[h] * o[h], axis=1, keepdims=True)
        oh = o[h] * lax.rsqrt(ms + LN_EPS) * gn_ref[...]
        o_ref[:, cols[h]] = (oh * gs_ref[:, cols[h]].astype(_F32)).astype(_BF)


def _recurrence(q_s, log_f, k, v, g_s, g_norm, bsz, seq, hpb=8):
    c = REC_CHUNK
    tri, pair = _rec_constants()
    tri = jnp.asarray(tri, _BF)
    pair = jnp.asarray(pair, _F32)
    w = hpb * HEAD_DIM
    n_c = seq // c
    fixed = lambda b, h, s: (0, 0)
    tile = lambda b, h, s: (b * n_c + s, h)
    return pl.pallas_call(
        functools.partial(_rec_body, hpb=hpb),
        grid=(bsz, N_HEADS // hpb, n_c),
        in_specs=[pl.BlockSpec(tri.shape, fixed), pl.BlockSpec(pair.shape, fixed),
                  pl.BlockSpec((c, w), tile), pl.BlockSpec((c, w), tile),
                  pl.BlockSpec((c, w), tile), pl.BlockSpec((c, w), tile),
                  pl.BlockSpec((c, w), tile), pl.BlockSpec((1, HEAD_DIM), fixed)],
        out_specs=pl.BlockSpec((c, w), tile),
        out_shape=jax.ShapeDtypeStruct((bsz * seq, D_MODEL), _BF),
        scratch_shapes=[pltpu.VMEM((hpb, HEAD_DIM, HEAD_DIM), _F32)],
        compiler_params=_cparams(("parallel", "parallel", "arbitrary")), name="hgrn_rec",
    )(tri, pair, q_s, log_f, k, v, g_s, g_norm)


def _router_body(x_ref, wr_ref, bias_ref, eidx_ref, gw_ref, rank_ref, cnt_ref, carry_ref, *, tm):
    @pl.when(pl.program_id(0) == 0)
    def _():
        carry_ref[...] = jnp.zeros_like(carry_ref)

    neg = -jnp.inf
    logits = lax.dot_general(wr_ref[...], x_ref[...], (((1,), (1,)), ((), ())),
                             precision=lax.Precision.HIGHEST,
                             preferred_element_type=_F32)
    scores = _sigmoid(logits)
    choice = scores + bias_ref[...]
    c3 = choice.reshape(N_GROUPS, GROUP_SIZE, tm)
    i3 = lax.broadcasted_iota(jnp.int32, c3.shape, 1)
    m1 = jnp.max(c3, axis=1, keepdims=True)
    first = jnp.min(jnp.where(c3 == m1, i3, GROUP_SIZE), axis=1, keepdims=True)
    m2 = jnp.max(jnp.where(i3 == first, neg, c3), axis=1, keepdims=True)
    gs = (m1 + m2).reshape(N_GROUPS, tm)
    ig = lax.broadcasted_iota(jnp.int32, gs.shape, 0)
    gsel = jnp.zeros(gs.shape, jnp.bool_)
    for _ in range(TOPK_GROUPS):
        m = jnp.max(gs, axis=0, keepdims=True)
        gi = jnp.min(jnp.where(gs == m, ig, N_GROUPS), axis=0, keepdims=True)
        hit = ig == gi
        gsel = gsel | hit
        gs = jnp.where(hit, neg, gs)
    allowed = jnp.broadcast_to(gsel.reshape(N_GROUPS, 1, tm), c3.shape).reshape(N_EXPERTS, tm)
    masked = jnp.where(allowed, choice, neg)
    ie = lax.broadcasted_iota(jnp.int32, masked.shape, 0)
    picked = jnp.zeros(masked.shape, _F32)
    hits, e_rows, w_rows = [], [], []
    for _ in range(TOP_K):
        m = jnp.max(masked, axis=0, keepdims=True)
        ei = jnp.min(jnp.where(masked == m, ie, N_EXPERTS), axis=0, keepdims=True)
        hit = ie == ei
        hits.append(hit)
        e_rows.append(ei)
        w_rows.append(jnp.sum(jnp.where(hit, scores, 0.0), axis=0, keepdims=True))
        picked = picked + hit.astype(_F32)
        masked = jnp.where(hit, neg, masked)
    gw = jnp.concatenate(w_rows, axis=0)
    gw = gw / jnp.sum(gw, axis=0, keepdims=True) * ROUTED_SCALE
    before = (lax.broadcasted_iota(jnp.int32, (tm, tm), 0) <
              lax.broadcasted_iota(jnp.int32, (tm, tm), 1)).astype(_BF)
    cum = jnp.dot(picked.astype(_BF), before, preferred_element_type=_F32) + carry_ref[...]
    r_rows = [jnp.sum(jnp.where(h, cum, 0.0), axis=0, keepdims=True) for h in hits]
    carry = carry_ref[...] + jnp.sum(picked, axis=1, keepdims=True)
    carry_ref[...] = carry
    eidx_ref[...] = jnp.concatenate(e_rows, axis=0)
    gw_ref[...] = gw
    rank_ref[...] = jnp.concatenate(r_rows, axis=0).astype(jnp.int32)
    cnt_ref[...] = carry.astype(jnp.int32)


def _router(x, wr_t, bias_col, tm=512):
    t = x.shape[0]
    tok = lambda i: (0, i)
    fixed = lambda i: (0, 0)
    return pl.pallas_call(
        functools.partial(_router_body, tm=tm),
        grid=(t // tm,),
        in_specs=[pl.BlockSpec((tm, D_MODEL), lambda i: (i, 0)),
                  pl.BlockSpec((N_EXPERTS, D_MODEL), fixed),
                  pl.BlockSpec((N_EXPERTS, 1), fixed)],
        out_specs=[pl.BlockSpec((TOP_K, tm), tok), pl.BlockSpec((TOP_K, tm), tok),
                   pl.BlockSpec((TOP_K, tm), tok), pl.BlockSpec((N_EXPERTS, 1), fixed)],
        out_shape=[jax.ShapeDtypeStruct((TOP_K, t), jnp.int32),
                   jax.ShapeDtypeStruct((TOP_K, t), _F32),
                   jax.ShapeDtypeStruct((TOP_K, t), jnp.int32),
                   jax.ShapeDtypeStruct((N_EXPERTS, 1), jnp.int32)],
        scratch_shapes=[pltpu.VMEM((N_EXPERTS, 1), _F32)],
        compiler_params=_cparams(("arbitrary",)), name="router",
    )(x, wr_t, bias_col)


BLOCK_WORDS = ROW_BLOCK * PACK_ROWS
EXPERT_SPLIT = 512


def _dispatch_body(pend_ref, dest_ref, xp_ref, xb_ref, wg_ref, wu_ref, wd_ref, xs_hbm, sh_ref,
                   zbuf, sem, *, tm):
    i = pl.program_id(0)

    def block_copy(b):
        start = pl.multiple_of(b * BLOCK_WORDS, BLOCK_WORDS)
        return pltpu.make_async_copy(zbuf, xs_hbm.at[pl.ds(start, BLOCK_WORDS), :], sem.at[1])

    def pad_copy(e):
        return block_copy(pend_ref[e] // ROW_BLOCK - 1)

    def has_rows(e):
        return pend_ref[e] > jnp.where(e > 0, pend_ref[jnp.maximum(e - 1, 0)], 0)

    @pl.when(i == 0)
    def _():
        zbuf[...] = jnp.zeros_like(zbuf)

        def start(e, carry):
            @pl.when(has_rows(e))
            def _():
                pad_copy(e).start()
            return carry
        lax.fori_loop(0, N_EXPERTS, start, 0)

        def wait(e, carry):
            @pl.when(has_rows(e))
            def _():
                pad_copy(e).wait()
            return carry
        lax.fori_loop(0, N_EXPERTS, wait, 0)

        first_unused = pend_ref[N_EXPERTS - 1] // ROW_BLOCK
        n_blocks = xs_hbm.shape[0] // BLOCK_WORDS

        def tail_start(b, carry):
            block_copy(b).start()
            return carry
        lax.fori_loop(first_unused, n_blocks, tail_start, 0)

        def tail_wait(b, carry):
            block_copy(b).wait()
            return carry
        lax.fori_loop(first_unused, n_blocks, tail_wait, 0)

    def push(t, carry):
        src = xp_ref.at[pl.ds(pl.multiple_of(t * PACK_ROWS, PACK_ROWS), PACK_ROWS), :]
        for k in range(TOP_K):
            row = pl.multiple_of(dest_ref[k, t] * PACK_ROWS, PACK_ROWS)
            pltpu.make_async_copy(src, xs_hbm.at[pl.ds(row, PACK_ROWS), :], sem.at[0]).start()
        return carry
    lax.fori_loop(0, tm, push, 0)
    xb = xb_ref[...]
    hg = jnp.dot(xb, wg_ref[...], preferred_element_type=_F32)
    hu = jnp.dot(xb, wu_ref[...], preferred_element_type=_F32)
    sh_ref[...] = jnp.dot((_silu(hg) * hu).astype(_BF), wd_ref[...], preferred_element_type=_F32)
    for k in range(TOP_K):
        pltpu.make_async_copy(xp_ref, xs_hbm.at[pl.ds(0, tm * PACK_ROWS), :], sem.at[0]).wait()


def _dispatch(pad_end, dest, xp, xb, wg, wu, wd, n_rows, tm=256):
    t = xb.shape[0]
    n_tiles = t // tm
    fixed = lambda i, pe: (0, 0)
    grid_spec = pltpu.PrefetchScalarGridSpec(
        num_scalar_prefetch=1, grid=(n_tiles,),
        in_specs=[pl.BlockSpec((TOP_K, tm), lambda i, pe: (0, i), memory_space=pltpu.SMEM),
                  pl.BlockSpec((tm * PACK_ROWS, HEAD_DIM), lambda i, pe: (i, 0)),
                  pl.BlockSpec((tm, D_MODEL), lambda i, pe: (i, 0)),
                  pl.BlockSpec((D_MODEL, EXPERT_DIM), fixed),
                  pl.BlockSpec((D_MODEL, EXPERT_DIM), fixed),
                  pl.BlockSpec((EXPERT_DIM, D_MODEL), fixed)],
        out_specs=[pl.BlockSpec(memory_space=pl.ANY),
                   pl.BlockSpec((tm, D_MODEL), lambda i, pe: (i, 0))],
        scratch_shapes=[pltpu.VMEM((BLOCK_WORDS, HEAD_DIM), _BF),
                        pltpu.SemaphoreType.DMA((2,))])
    return pl.pallas_call(
        functools.partial(_dispatch_body, tm=tm), grid_spec=grid_spec,
        out_shape=[jax.ShapeDtypeStruct((n_rows * PACK_ROWS, HEAD_DIM), _BF),
                   jax.ShapeDtypeStruct((t, D_MODEL), _F32)],
        compiler_params=_cparams(("arbitrary",)), name="dispatch",
    )(pad_end, dest, xp, xb, wg, wu, wd)


def _expert_body(be_ref, nv_ref, nu_ref, seg_ref, nxt_ref, x_ref, wg_hbm, wu_hbm, wd_hbm, y_ref,
                 wg_b, wu_b, wd_b, stage, wg_f, wu_f, wd_f, sem, *, layer):
    i = pl.program_id(0)
    n_valid = nv_ref[i]
    new_expert = (i == 0) | (be_ref[i] != be_ref[jnp.maximum(i - 1, 0)])

    def weight_copies(e, s):
        return [pltpu.make_async_copy(src.at[layer, e], dst.at[s], sem.at[s])
                for src, dst in ((wg_hbm, wg_f), (wu_hbm, wu_f), (wd_hbm, wd_f))]

    @pl.when(new_expert & (n_valid > 0))
    def _():
        s = seg_ref[i] % 2

        @pl.when(i == 0)
        def _():
            for cp in weight_copies(be_ref[0], 0):
                cp.start()

        for cp in weight_copies(be_ref[i], s):
            cp.wait()

        @pl.when(nxt_ref[i] >= 0)
        def _():
            for cp in weight_copies(nxt_ref[i], 1 - s):
                cp.start()

        _cast_rows(wg_f.at[s], wg_b)
        _cast_rows(wu_f.at[s], wu_b)
        _cast_rows(wd_f.at[s], wd_b)

    @pl.when(n_valid > 0)
    def _():
        stage[0:BLOCK_WORDS, :] = x_ref[...].astype(_F32)
        n_slab = EXPERT_SPLIT // HEAD_DIM
        hg = None
        hu = None
        for c in range(D_MODEL // EXPERT_SPLIT):
            ks = slice(c * EXPERT_SPLIT, (c + 1) * EXPERT_SPLIT)
            xc = jnp.concatenate(
                [stage[pl.ds(c * n_slab + j, ROW_BLOCK, stride=PACK_ROWS), :]
                 for j in range(n_slab)], axis=1).astype(_BF)
            pg = jnp.dot(xc, wg_b[ks, :], preferred_element_type=_F32)
            pu = jnp.dot(xc, wu_b[ks, :], preferred_element_type=_F32)
            hg = pg if hg is None else hg + pg
            hu = pu if hu is None else hu + pu
        hh = (_silu(hg) * hu).astype(_BF)
        pitch = ROW_BLOCK + STAGE_PAD
        for c in range(D_MODEL // EXPERT_SPLIT):
            yc = jnp.dot(hh, wd_b[:, c * EXPERT_SPLIT:(c + 1) * EXPERT_SPLIT],
                         preferred_element_type=_F32)
            for j in range(n_slab):
                jj = c * n_slab + j
                stage[jj * pitch:jj * pitch + ROW_BLOCK, :] = yc[:, j * HEAD_DIM:(j + 1) * HEAD_DIM]
        for r in range(ROW_BLOCK):
            y_ref[r * PACK_ROWS:(r + 1) * PACK_ROWS, :] = stage[pl.ds(r, PACK_ROWS, stride=pitch), :]

    @pl.when(n_valid == 0)
    def _():
        y_ref[...] = jnp.zeros_like(y_ref)


def _experts(blk_expert, n_valid, n_used, seg, nxt, xs, wg, wu, wd, layer):
    n_blocks = xs.shape[0] // BLOCK_WORDS
    x_map = lambda i, be, nv, nu, sg, nx: (jnp.minimum(i, nu[0] - 1), 0)
    grid_spec = pltpu.PrefetchScalarGridSpec(
        num_scalar_prefetch=5, grid=(n_blocks,),
        in_specs=[pl.BlockSpec((BLOCK_WORDS, HEAD_DIM), x_map),
                  pl.BlockSpec(memory_space=pl.ANY),
                  pl.BlockSpec(memory_space=pl.ANY),
                  pl.BlockSpec(memory_space=pl.ANY)],
        out_specs=pl.BlockSpec((BLOCK_WORDS, HEAD_DIM), lambda i, be, nv, nu, sg, nx: (i, 0)),
        scratch_shapes=[pltpu.VMEM((D_MODEL, EXPERT_DIM), _BF),
                        pltpu.VMEM((D_MODEL, EXPERT_DIM), _BF),
                        pltpu.VMEM((EXPERT_DIM, D_MODEL), _BF),
                        pltpu.VMEM((PACK_ROWS * (ROW_BLOCK + STAGE_PAD), HEAD_DIM), _F32),
                        pltpu.VMEM((2, D_MODEL, EXPERT_DIM), _F32),
                        pltpu.VMEM((2, D_MODEL, EXPERT_DIM), _F32),
                        pltpu.VMEM((2, EXPERT_DIM, D_MODEL), _F32),
                        pltpu.SemaphoreType.DMA((2,))])
    return pl.pallas_call(
        functools.partial(_expert_body, layer=layer), grid_spec=grid_spec,
        out_shape=jax.ShapeDtypeStruct(xs.shape, _F32),
        compiler_params=_cparams(("arbitrary",)), name="experts",
    )(blk_expert, n_valid, n_used, seg, nxt, xs, wg, wu, wd)


COMBINE_ROWS = 32
SLAB_PITCH = PACK_ROWS + 8


def _combine_body(dcur_ref, dnxt_ref, y_hbm, gw_ref, x_ref, sh_ref, g_ref, b_ref,
                  o_ref, ob_ref, buf, sem, *, tm, n_tiles):
    i = pl.program_id(0)
    slot = i % 2
    n_rows = TOP_K * tm * PACK_ROWS

    def start_gather(d_ref, s, t):
        for k in range(TOP_K):
            src = pl.multiple_of(d_ref[k, t] * PACK_ROWS, PACK_ROWS)
            dst = pl.multiple_of((k * tm + t) * SLAB_PITCH, 8)
            pltpu.make_async_copy(y_hbm.at[pl.ds(src, PACK_ROWS), :],
                                  buf.at[s, pl.ds(dst, PACK_ROWS), :], sem.at[s]).start()

    def wait_gather(s):
        pltpu.make_async_copy(y_hbm.at[pl.ds(0, n_rows), :], buf.at[s, pl.ds(0, n_rows), :],
                              sem.at[s]).wait()

    @pl.when(i == 0)
    def _():
        def first(t, carry):
            start_gather(dcur_ref, 0, t)
            return carry
        lax.fori_loop(0, tm, first, 0)

    o_ref[...] = ALPHA * x_ref[...] + sh_ref[...]
    wait_gather(slot)
    rows_buf = buf.at[slot]

    def add_routed(c, carry):
        r0 = pl.multiple_of(c * COMBINE_ROWS, COMBINE_ROWS)
        for u in range(COMBINE_ROWS):
            start_gather(dnxt_ref, 1 - slot, r0 + u)
        rows = pl.ds(r0, COMBINE_ROWS)
        gate = gw_ref[rows, :]
        gates = [jnp.broadcast_to(gate[:, k:k + 1], (COMBINE_ROWS, HEAD_DIM)) for k in range(TOP_K)]
        for j in range(PACK_ROWS):
            cols = slice(j * HEAD_DIM, (j + 1) * HEAD_DIM)
            acc = o_ref[rows, cols]
            for k in range(TOP_K):
                start = (k * tm + r0) * SLAB_PITCH + j
                acc = acc + gates[k] * rows_buf[pl.ds(start, COMBINE_ROWS, stride=SLAB_PITCH), :]
            o_ref[rows, cols] = acc
        return carry
    lax.fori_loop(0, tm // COMBINE_ROWS, add_routed, 0)
    r = _layer_norm(o_ref[...], g_ref[...], b_ref[...])
    o_ref[...] = r
    ob_ref[...] = r.astype(_BF)

    @pl.when(i == n_tiles - 1)
    def _():
        wait_gather(1 - slot)


def _combine(dest, y, gw_t, x, shared, g, b, tm=128):
    t = x.shape[0]
    nt = t // tm
    row = lambda i: (i, 0)
    fixed = lambda i: (0, 0)
    return pl.pallas_call(
        functools.partial(_combine_body, tm=tm, n_tiles=nt),
        grid=(nt,),
        in_specs=[pl.BlockSpec((TOP_K, tm), lambda i: (0, i), memory_space=pltpu.SMEM),
                  pl.BlockSpec((TOP_K, tm), lambda i: (0, jnp.minimum(i + 1, nt - 1)),
                               memory_space=pltpu.SMEM),
                  pl.BlockSpec(memory_space=pl.ANY),
                  pl.BlockSpec((tm, TOP_K), row),
                  pl.BlockSpec((tm, D_MODEL), row), pl.BlockSpec((tm, D_MODEL), row),
                  pl.BlockSpec((1, D_MODEL), fixed), pl.BlockSpec((1, D_MODEL), fixed)],
        out_specs=[pl.BlockSpec((tm, D_MODEL), row), pl.BlockSpec((tm, D_MODEL), row)],
        out_shape=[jax.ShapeDtypeStruct((t, D_MODEL), _F32),
                   jax.ShapeDtypeStruct((t, D_MODEL), _BF)],
        scratch_shapes=[pltpu.VMEM((2, TOP_K * tm * SLAB_PITCH, HEAD_DIM), _F32),
                        pltpu.SemaphoreType.DMA((2,))],
        compiler_params=_cparams(("arbitrary",)), name="combine",
    )(dest, dest, y, gw_t, x, shared, g, b)


def _moe(x, xb, xp, layer, w_router, e_bias, w_gate, w_up, w_down, ws_gate, ws_up, ws_down, g, b):
    t = x.shape[0]
    eidx, gw, rank, counts = _router(x, w_router.T, e_bias.reshape(N_EXPERTS, 1))
    counts = counts.reshape(N_EXPERTS)
    padded = (counts + ROW_BLOCK - 1) // ROW_BLOCK * ROW_BLOCK
    pad_end = jnp.cumsum(padded)
    pad_start = pad_end - padded
    n_blocks = t * TOP_K // ROW_BLOCK + N_EXPERTS
    expert_ids = jnp.arange(N_EXPERTS, dtype=jnp.int32)
    dest = rank + jnp.sum(jnp.where(eidx[None] == expert_ids[:, None, None],
                                    pad_start[:, None, None], 0), axis=0)
    blk_start = jnp.arange(n_blocks, dtype=jnp.int32) * ROW_BLOCK
    blk_expert = jnp.minimum(jnp.sum(pad_end[None, :] <= blk_start[:, None], axis=1),
                             N_EXPERTS - 1).astype(jnp.int32)
    n_valid = jnp.clip((pad_start + counts)[blk_expert] - blk_start, 0, ROW_BLOCK)
    n_valid = jnp.where(blk_start < pad_end[-1], n_valid, 0).astype(jnp.int32)
    n_used = (pad_end[-1:] // ROW_BLOCK).astype(jnp.int32)
    xs, shared = _dispatch(pad_end.astype(jnp.int32), dest, xp, xb, ws_gate.astype(_BF),
                           ws_up.astype(_BF), ws_down.astype(_BF), n_blocks * ROW_BLOCK)
    used = blk_start < pad_end[-1]
    change = jnp.concatenate([jnp.zeros((1,), jnp.int32),
                              (blk_expert[1:] != blk_expert[:-1]).astype(jnp.int32)])
    seg = jnp.cumsum(change)
    later = jnp.where(used[None, :] & (seg[None, :] == seg[:, None] + 1), blk_expert[None, :], -1)
    nxt = jnp.max(later, axis=1).astype(jnp.int32)
    y = _experts(blk_expert, n_valid, n_used, seg.astype(jnp.int32), nxt, xs,
                 w_gate, w_up, w_down, layer)
    return _combine(dest, y, gw.T, x, shared, g.reshape(1, -1), b.reshape(1, -1))


def kernel(x, ln_mix_g, ln_mix_b, ln_ffn_g, ln_ffn_b, gmlp_w_in, gmlp_v_ln_g, gmlp_v_ln_b, gmlp_w_sp, gmlp_b_sp, gmlp_w_out, hgrn_w_in, hgrn_o_norm_g, hgrn_w_out, hgrn_lower_bounds, moe_w_router, moe_e_bias, moe_w_gate, moe_w_up, moe_w_down, moe_ws_gate, moe_ws_up, moe_ws_down):
    bsz, seq, d = x.shape
    t = bsz * seq
    row = lambda a: a.reshape(1, -1)
    xf = x.reshape(t, d)
    xb = xf.astype(_BF)

    z = _proj(xb, gmlp_w_in[0], _epi_gelu, [_BF], 2 * d, name="gmlp_in")[0]
    y = _sgu(z, row(gmlp_v_ln_g[0]), row(gmlp_v_ln_b[0]), gmlp_w_sp[0], gmlp_b_sp[0].T)
    xf, xb, xp = _out_ln(y, gmlp_w_out[0].astype(_BF), xf, row(ln_mix_g[0]), row(ln_mix_b[0]))
    xf, xb = _moe(xf, xb, xp, 0, moe_w_router[0], moe_e_bias[0], moe_w_gate, moe_w_up,
                     moe_w_down, moe_ws_gate[0], moe_ws_up[0], moe_ws_down[0],
                     ln_ffn_g[0], ln_ffn_b[0])

    lb_soft = jax.nn.softmax(hgrn_lower_bounds.astype(_F32), axis=0)
    lb = (jnp.cumsum(lb_soft, axis=0) - lb_soft[0])[1]
    w_in = hgrn_w_in[0]
    q_s = _proj(xb, w_in, _epi_silu, [_BF], d, col0=0, name="hgrn_q")[0]
    log_f, kk = _proj(xb, w_in, _epi_forget, [_F32, _BF], d, col0=d,
                      vecs=(row(jnp.log(lb)), row(jnp.log1p(-lb)), row(1.0 - lb)), name="hgrn_f")
    vv = _proj(xb, w_in, _epi_id, [_BF], d, col0=2 * d, name="hgrn_i")[0]
    g_s = _proj(xb, w_in, _epi_silu, [_BF], d, col0=3 * d, name="hgrn_g")[0]
    o = _recurrence(q_s, log_f, kk, vv, g_s, row(hgrn_o_norm_g[0]), bsz, seq)
    xf, xb, xp = _out_ln(o, hgrn_w_out[0].astype(_BF), xf, row(ln_mix_g[1]), row(ln_mix_b[1]))
    xf, xb = _moe(xf, xb, xp, 1, moe_w_router[1], moe_e_bias[1], moe_w_gate, moe_w_up,
                     moe_w_down, moe_ws_gate[1], moe_ws_up[1], moe_ws_down[1],
                     ln_ffn_g[1], ln_ffn_b[1])
    return xf.reshape(bsz, seq, d)
```

```python
import functools
import math

import numpy as np
import jax
import jax.numpy as jnp
from jax import lax
from jax.experimental import pallas as pl
from jax.experimental.pallas import tpu as pltpu

D_MODEL = 2048
N_HEADS = 16
HEAD_DIM = 128
GMLP_BLOCK = 128
STREAM_CHUNK = 64
N_EXPERTS = 64
TOP_K = 8
N_GROUPS = 8
GROUP_SIZE = N_EXPERTS // N_GROUPS
TOPK_GROUPS = 4
EXPERT_DIM = 512
ROUTED_SCALE = 2.5
LN_EPS = 1e-5
DEPTH = 2
ALPHA = (2 * DEPTH) ** 0.25

ROW_BLOCK = 256
PACK_ROWS = D_MODEL // HEAD_DIM
REC_CHUNK = 128
REC_LEVELS = int(math.log2(REC_CHUNK))
VMEM_LIMIT = 56 * 1024 * 1024

_BF = jnp.bfloat16
_F32 = jnp.float32


def _cparams(sem):
    return pltpu.CompilerParams(dimension_semantics=sem, vmem_limit_bytes=VMEM_LIMIT)


def _sigmoid(x):
    return 1.0 / (1.0 + jnp.exp(-x))


def _silu(x):
    return x * _sigmoid(x)


STAGE_PAD = 8
SLAB_PITCH = PACK_ROWS + 8


def _slab_store(ref, stage, m, val):
    pitch = m + STAGE_PAD
    for j in range(PACK_ROWS):
        stage[j * pitch:j * pitch + m, :] = val[:, j * HEAD_DIM:(j + 1) * HEAD_DIM]

    for r in range(m):
        slab = stage[pl.ds(r, PACK_ROWS, stride=pitch), :]
        ref[r * PACK_ROWS:(r + 1) * PACK_ROWS, :] = slab.astype(_BF)


def _layer_norm(x, g, b):
    mu = jnp.mean(x, axis=-1, keepdims=True)
    xc = x - mu
    var = jnp.mean(xc * xc, axis=-1, keepdims=True)
    return xc * lax.rsqrt(var + LN_EPS) * g + b


CAST_ROWS = 256


def _cast_rows(src, dst):
    def step(i, carry):
        rows = pl.ds(pl.multiple_of(i * CAST_ROWS, CAST_ROWS), CAST_ROWS)
        dst[rows, :] = src[rows, :].astype(_BF)
        return carry
    lax.fori_loop(0, src.shape[0] // CAST_ROWS, step, 0)


def _proj_body(epi, n_vec, n_out, x_ref, w_ref, *refs):
    vecs = [r[...] for r in refs[:n_vec]]
    outs = refs[n_vec:n_vec + n_out]
    wb_ref = refs[n_vec + n_out]

    @pl.when(pl.program_id(1) == 0)
    def _():
        _cast_rows(w_ref, wb_ref)

    acc = jnp.dot(x_ref[...], wb_ref[...], preferred_element_type=_F32)
    res = epi(acc, *vecs)
    for o_ref, r in zip(outs, res):
        o_ref[...] = r.astype(o_ref.dtype)


def _proj(x, w, epi, out_dtypes, n, col0=0, vecs=(), tm=512, tn=1024, name="proj"):
    m, k = x.shape
    grid = (n // tn, m // tm)
    c0 = col0 // tn
    in_specs = [pl.BlockSpec((tm, k), lambda j, i: (i, 0)),
                pl.BlockSpec((k, tn), lambda j, i: (0, c0 + j))]
    in_specs += [pl.BlockSpec((1, tn), lambda j, i: (0, j)) for _ in vecs]
    out_specs = [pl.BlockSpec((tm, tn), lambda j, i: (i, j)) for _ in out_dtypes]
    out_shape = [jax.ShapeDtypeStruct((m, n), dt) for dt in out_dtypes]
    return pl.pallas_call(
        functools.partial(_proj_body, epi, len(vecs), len(out_dtypes)),
        grid=grid, in_specs=in_specs, out_specs=out_specs, out_shape=out_shape,
        scratch_shapes=[pltpu.VMEM((k, tn), _BF)],
        compiler_params=_cparams(("arbitrary", "arbitrary")), name=name,
    )(x, w, *vecs)


def _epi_gelu(acc):
    return (0.5 * acc * (1.0 + lax.erf(acc * (1.0 / math.sqrt(2.0)))),)


def _epi_silu(acc):
    return (_silu(acc),)


def _epi_id(acc):
    return (acc,)


def _epi_forget(acc, log_lb, log1m_lb, one_m_lb):
    e = jnp.exp(-jnp.abs(acc))
    ls = jnp.minimum(acc, 0.0) - jnp.log(1.0 + e)
    c = log1m_lb + ls
    log_f = jnp.maximum(log_lb, c) + jnp.log(1.0 + jnp.exp(-jnp.abs(log_lb - c)))
    inv = 1.0 / (1.0 + e)
    k = one_m_lb * jnp.where(acc > 0.0, e * inv, inv)
    return log_f, k


def _sgu_body(u_ref, v_ref, g_ref, b_ref, wsp_ref, bsp_ref, y_ref, *, n_blk):
    vn = _layer_norm(v_ref[...].astype(_F32), g_ref[...], b_ref[...]).astype(_BF)
    ri = lax.broadcasted_iota(jnp.int32, (GMLP_BLOCK, GMLP_BLOCK), 0) // STREAM_CHUNK
    ci = lax.broadcasted_iota(jnp.int32, (GMLP_BLOCK, GMLP_BLOCK), 1) // STREAM_CHUNK
    causal = ri >= ci
    for h in range(N_HEADS):
        w = jnp.where(causal, wsp_ref[h], 0.0).astype(_BF)
        bias = bsp_ref[:, h:h + 1]
        cs = slice(h * HEAD_DIM, (h + 1) * HEAD_DIM)
        for n in range(n_blk):
            rs = slice(n * GMLP_BLOCK, (n + 1) * GMLP_BLOCK)
            sv = jnp.dot(w, vn[rs, cs], preferred_element_type=_F32) + bias
            y_ref[rs, cs] = (u_ref[rs, cs].astype(_F32) * sv).astype(_BF)


def _sgu(z, g, b, w_sp, bsp_t, tm=256):
    t = z.shape[0]
    return pl.pallas_call(
        functools.partial(_sgu_body, n_blk=tm // GMLP_BLOCK),
        grid=(t // tm,),
        in_specs=[pl.BlockSpec((tm, D_MODEL), lambda i: (i, 0)),
                  pl.BlockSpec((tm, D_MODEL), lambda i: (i, 1)),
                  pl.BlockSpec((1, D_MODEL), lambda i: (0, 0)),
                  pl.BlockSpec((1, D_MODEL), lambda i: (0, 0)),
                  pl.BlockSpec((N_HEADS, GMLP_BLOCK, GMLP_BLOCK), lambda i: (0, 0, 0)),
                  pl.BlockSpec((GMLP_BLOCK, N_HEADS), lambda i: (0, 0))],
        out_specs=pl.BlockSpec((tm, D_MODEL), lambda i: (i, 0)),
        out_shape=jax.ShapeDtypeStruct((t, D_MODEL), _BF),
        compiler_params=_cparams(("parallel",)), name="sgu",
    )(z, z, g, b, w_sp, bsp_t)


def _out_ln_body(y_ref, w_ref, x_ref, g_ref, b_ref, o_ref, ob_ref, op_ref, stage):
    h = jnp.dot(y_ref[...], w_ref[...], preferred_element_type=_F32)
    r = _layer_norm(ALPHA * x_ref[...] + h, g_ref[...], b_ref[...])
    o_ref[...] = r
    ob_ref[...] = r.astype(_BF)
    _slab_store(op_ref, stage, r.shape[0], r)


def _out_ln(y, w, x_res, g, b, tm=256):
    t = y.shape[0]
    row = lambda i: (i, 0)
    fixed = lambda i: (0, 0)
    return pl.pallas_call(
        _out_ln_body,
        grid=(t // tm,),
        in_specs=[pl.BlockSpec((tm, D_MODEL), row),
                  pl.BlockSpec((D_MODEL, D_MODEL), fixed),
                  pl.BlockSpec((tm, D_MODEL), row),
                  pl.BlockSpec((1, D_MODEL), fixed),
                  pl.BlockSpec((1, D_MODEL), fixed)],
        out_specs=[pl.BlockSpec((tm, D_MODEL), row), pl.BlockSpec((tm, D_MODEL), row),
                   pl.BlockSpec((tm * PACK_ROWS, HEAD_DIM), row)],
        out_shape=[jax.ShapeDtypeStruct((t, D_MODEL), _F32),
                   jax.ShapeDtypeStruct((t, D_MODEL), _BF),
                   jax.ShapeDtypeStruct((t * PACK_ROWS, HEAD_DIM), _BF)],
        scratch_shapes=[pltpu.VMEM((PACK_ROWS * (tm + STAGE_PAD), HEAD_DIM), _F32)],
        compiler_params=_cparams(("parallel",)), name="out_ln",
    )(y, w, x_res, g, b)


def _rec_constants():
    c = REC_CHUNK
    t = np.arange(c)[:, None]
    j = np.arange(c)[None, :]
    tri = (j <= t).astype(np.float32)
    pair = []
    for lv in range(REC_LEVELS):
        m = c >> (lv + 1)
        same_block = (t // (2 * m)) == (j // (2 * m))
        pair.append(same_block & (t % (2 * m) >= m) & (j % (2 * m) < m))
    pair = np.concatenate(pair, 0).astype(np.float32)
    return tri, pair


def _mid_rows(b, m, row):
    c = b.shape[0]
    if 2 * m >= 8:
        return jnp.concatenate([jnp.broadcast_to(b[s + m - 1:s + m, :], (2 * m, HEAD_DIM))
                                for s in range(0, c, 2 * m)], axis=0)
    prev1 = pltpu.roll(b, 1, 0)
    if m == 1:
        return jnp.where(row % 2 == 1, prev1, b)
    p = row % 4
    return jnp.where(p == 0, pltpu.roll(b, c - 1, 0),
                     jnp.where(p == 1, b, jnp.where(p == 2, prev1, pltpu.roll(b, 2, 0))))


def _rec_body(tri_ref, pair_ref, q_ref, lf_ref, k_ref, v_ref, gs_ref, gn_ref,
              o_ref, st_ref, *, hpb):
    c = REC_CHUNK
    nt = (((1,), (1,)), ((), ()))

    @pl.when(pl.program_id(2) == 0)
    def _():
        st_ref[...] = jnp.zeros_like(st_ref)

    tri = tri_ref[...]
    eye = (lax.broadcasted_iota(jnp.int32, (c, c), 0) ==
           lax.broadcasted_iota(jnp.int32, (c, c), 1))
    row = lax.broadcasted_iota(jnp.int32, (c, HEAD_DIM), 0)
    heads = range(hpb)
    cols = [slice(h * HEAD_DIM, (h + 1) * HEAD_DIM) for h in heads]
    q = [q_ref[:, cs].astype(_F32) for cs in cols]
    k = [k_ref[:, cs].astype(_F32) for cs in cols]
    v = [v_ref[:, cs] for cs in cols]

    b = []
    for cs in cols:
        lf = lf_ref[:, cs]
        l1 = lf.astype(_BF)
        r1 = lf - l1.astype(_F32)
        l2 = r1.astype(_BF)
        l3 = (r1 - l2.astype(_F32)).astype(_BF)
        b3 = jnp.dot(tri, jnp.concatenate([l1, l2, l3], axis=1), preferred_element_type=_F32)
        b.append(b3[:, :HEAD_DIM] + b3[:, HEAD_DIM:2 * HEAD_DIM] + b3[:, 2 * HEAD_DIM:])
    b_last = [bh[c - 1:c, :] for bh in b]
    st = [st_ref[h] for h in heads]
    o = [lax.dot_general((q[h] * jnp.exp(b[h])).astype(_BF), st[h].astype(_BF), nt,
                         preferred_element_type=_F32) for h in heads]
    scores = [jnp.where(eye, jnp.sum(q[h] * k[h], axis=1, keepdims=True), 0.0) for h in heads]
    for lv in range(REC_LEVELS):
        pair = pair_ref[lv * c:(lv + 1) * c, :]
        for h in heads:
            e = jnp.exp(-jnp.abs(b[h] - _mid_rows(b[h], c >> (lv + 1), row)))
            s = lax.dot_general((q[h] * e).astype(_BF), (k[h] * e).astype(_BF), nt,
                                preferred_element_type=_F32)
            scores[h] = scores[h] + s * pair
    for h in heads:
        o[h] = o[h] + jnp.dot(scores[h].astype(_BF), v[h], preferred_element_type=_F32)
        k_dec = (k[h] * jnp.exp(b_last[h] - b[h])).astype(_BF)
        upd = lax.dot_general(v[h], k_dec, (((0,), (0,)), ((), ())), preferred_element_type=_F32)
        st_ref[h] = jnp.exp(b_last[h]) * st[h] + upd
    for h in heads:
        ms = jnp.mean(o[h] * o[h], axis=1, keepdims=True)
        oh = o[h] * lax.rsqrt(ms + LN_EPS) * gn_ref[...]
        o_ref[:, cols[h]] = (oh * gs_ref[:, cols[h]].astype(_F32)).astype(_BF)


def _recurrence(q_s, log_f, k, v, g_s, g_norm, bsz, seq, hpb=8):
    c = REC_CHUNK
    tri, pair = _rec_constants()
    tri = jnp.asarray(tri, _BF)
    pair = jnp.asarray(pair, _F32)
    w = hpb * HEAD_DIM
    n_c = seq // c
    fixed = lambda b, h, s: (0, 0)
    tile = lambda b, h, s: (b * n_c + s, h)
    return pl.pallas_call(
        functools.partial(_rec_body, hpb=hpb),
        grid=(bsz, N_HEADS // hpb, n_c),
        in_specs=[pl.BlockSpec(tri.shape, fixed), pl.BlockSpec(pair.shape, fixed),
                  pl.BlockSpec((c, w), tile), pl.BlockSpec((c, w), tile),
                  pl.BlockSpec((c, w), tile), pl.BlockSpec((c, w), tile),
                  pl.BlockSpec((c, w), tile), pl.BlockSpec((1, HEAD_DIM), fixed)],
        out_specs=pl.BlockSpec((c, w), tile),
        out_shape=jax.ShapeDtypeStruct((bsz * seq, D_MODEL), _BF),
        scratch_shapes=[pltpu.VMEM((hpb, HEAD_DIM, HEAD_DIM), _F32)],
        compiler_params=_cparams(("parallel", "parallel", "arbitrary")), name="hgrn_rec",
    )(tri, pair, q_s, log_f, k, v, g_s, g_norm)


def _router_body(x_ref, wr_ref, bias_ref, eidx_ref, gw_ref, rank_ref, cnt_ref, carry_ref, *, tm):
    @pl.when(pl.program_id(0) == 0)
    def _():
        carry_ref[...] = jnp.zeros_like(carry_ref)

    neg = -jnp.inf
    logits = lax.dot_general(wr_ref[...], x_ref[...], (((1,), (1,)), ((), ())),
                             precision=lax.Precision.HIGHEST,
                             preferred_element_type=_F32)
    scores = _sigmoid(logits)
    choice = scores + bias_ref[...]
    c3 = choice.reshape(N_GROUPS, GROUP_SIZE, tm)
    i3 = lax.broadcasted_iota(jnp.int32, c3.shape, 1)
    m1 = jnp.max(c3, axis=1, keepdims=True)
    first = jnp.min(jnp.where(c3 == m1, i3, GROUP_SIZE), axis=1, keepdims=True)
    m2 = jnp.max(jnp.where(i3 == first, neg, c3), axis=1, keepdims=True)
    gs = (m1 + m2).reshape(N_GROUPS, tm)
    ig = lax.broadcasted_iota(jnp.int32, gs.shape, 0)
    gsel = jnp.zeros(gs.shape, jnp.bool_)
    for _ in range(TOPK_GROUPS):
        m = jnp.max(gs, axis=0, keepdims=True)
        gi = jnp.min(jnp.where(gs == m, ig, N_GROUPS), axis=0, keepdims=True)
        hit = ig == gi
        gsel = gsel | hit
        gs = jnp.where(hit, neg, gs)
    allowed = jnp.broadcast_to(gsel.reshape(N_GROUPS, 1, tm), c3.shape).reshape(N_EXPERTS, tm)
    masked = jnp.where(allowed, choice, neg)
    ie = lax.broadcasted_iota(jnp.int32, masked.shape, 0)
    picked = jnp.zeros(masked.shape, _F32)
    hits, e_rows, w_rows = [], [], []
    for _ in range(TOP_K):
        m = jnp.max(masked, axis=0, keepdims=True)
        ei = jnp.min(jnp.where(masked == m, ie, N_EXPERTS), axis=0, keepdims=True)
        hit = ie == ei
        hits.append(hit)
        e_rows.append(ei)
        w_rows.append(jnp.sum(jnp.where(hit, scores, 0.0), axis=0, keepdims=True))
        picked = picked + hit.astype(_F32)
        masked = jnp.where(hit, neg, masked)
    gw = jnp.concatenate(w_rows, axis=0)
    gw = gw / jnp.sum(gw, axis=0, keepdims=True) * ROUTED_SCALE
    before = (lax.broadcasted_iota(jnp.int32, (tm, tm), 0) <
              lax.broadcasted_iota(jnp.int32, (tm, tm), 1)).astype(_BF)
    cum = jnp.dot(picked.astype(_BF), before, preferred_element_type=_F32) + carry_ref[...]
    r_rows = [jnp.sum(jnp.where(h, cum, 0.0), axis=0, keepdims=True) for h in hits]
    carry = carry_ref[...] + jnp.sum(picked, axis=1, keepdims=True)
    carry_ref[...] = carry
    eidx_ref[...] = jnp.concatenate(e_rows, axis=0)
    gw_ref[...] = gw
    rank_ref[...] = jnp.concatenate(r_rows, axis=0).astype(jnp.int32)
    cnt_ref[...] = carry.astype(jnp.int32)


def _router(x, wr_t, bias_col, tm=512):
    t = x.shape[0]
    tok = lambda i: (0, i)
    fixed = lambda i: (0, 0)
    return pl.pallas_call(
        functools.partial(_router_body, tm=tm),
        grid=(t // tm,),
        in_specs=[pl.BlockSpec((tm, D_MODEL), lambda i: (i, 0)),
                  pl.BlockSpec((N_EXPERTS, D_MODEL), fixed),
                  pl.BlockSpec((N_EXPERTS, 1), fixed)],
        out_specs=[pl.BlockSpec((TOP_K, tm), tok), pl.BlockSpec((TOP_K, tm), tok),
                   pl.BlockSpec((TOP_K, tm), tok), pl.BlockSpec((N_EXPERTS, 1), fixed)],
        out_shape=[jax.ShapeDtypeStruct((TOP_K, t), jnp.int32),
                   jax.ShapeDtypeStruct((TOP_K, t), _F32),
                   jax.ShapeDtypeStruct((TOP_K, t), jnp.int32),
                   jax.ShapeDtypeStruct((N_EXPERTS, 1), jnp.int32)],
        scratch_shapes=[pltpu.VMEM((N_EXPERTS, 1), _F32)],
        compiler_params=_cparams(("arbitrary",)), name="router",
    )(x, wr_t, bias_col)


BLOCK_WORDS = ROW_BLOCK * PACK_ROWS
EXPERT_SPLIT = 512


def _dispatch_body(pend_ref, dest_ref, xp_ref, xb_ref, wg_ref, wu_ref, wd_ref, xs_hbm, sh_ref,
                   zbuf, sem, *, tm):
    i = pl.program_id(0)

    def block_copy(b):
        start = pl.multiple_of(b * BLOCK_WORDS, BLOCK_WORDS)
        return pltpu.make_async_copy(zbuf, xs_hbm.at[pl.ds(start, BLOCK_WORDS), :], sem.at[1])

    def pad_copy(e):
        return block_copy(pend_ref[e] // ROW_BLOCK - 1)

    def has_rows(e):
        return pend_ref[e] > jnp.where(e > 0, pend_ref[jnp.maximum(e - 1, 0)], 0)

    @pl.when(i == 0)
    def _():
        zbuf[...] = jnp.zeros_like(zbuf)

        def start(e, carry):
            @pl.when(has_rows(e))
            def _():
                pad_copy(e).start()
            return carry
        lax.fori_loop(0, N_EXPERTS, start, 0)

        def wait(e, carry):
            @pl.when(has_rows(e))
            def _():
                pad_copy(e).wait()
            return carry
        lax.fori_loop(0, N_EXPERTS, wait, 0)

        first_unused = pend_ref[N_EXPERTS - 1] // ROW_BLOCK
        n_blocks = xs_hbm.shape[0] // BLOCK_WORDS

        def tail_start(b, carry):
            block_copy(b).start()
            return carry
        lax.fori_loop(first_unused, n_blocks, tail_start, 0)

        def tail_wait(b, carry):
            block_copy(b).wait()
            return carry
        lax.fori_loop(first_unused, n_blocks, tail_wait, 0)

    def push(t, carry):
        src = xp_ref.at[pl.ds(pl.multiple_of(t * PACK_ROWS, PACK_ROWS), PACK_ROWS), :]
        for k in range(TOP_K):
            row = pl.multiple_of(dest_ref[k, t] * PACK_ROWS, PACK_ROWS)
            pltpu.make_async_copy(src, xs_hbm.at[pl.ds(row, PACK_ROWS), :], sem.at[0]).start()
        return carry
    lax.fori_loop(0, tm, push, 0)
    xb = xb_ref[...]
    hg = jnp.dot(xb, wg_ref[...], preferred_element_type=_F32)
    hu = jnp.dot(xb, wu_ref[...], preferred_element_type=_F32)
    sh_ref[...] = jnp.dot((_silu(hg) * hu).astype(_BF), wd_ref[...], preferred_element_type=_F32)
    for k in range(TOP_K):
        pltpu.make_async_copy(xp_ref, xs_hbm.at[pl.ds(0, tm * PACK_ROWS), :], sem.at[0]).wait()


def _dispatch(pad_end, dest, xp, xb, wg, wu, wd, n_rows, tm=256):
    t = xb.shape[0]
    n_tiles = t // tm
    fixed = lambda i, pe: (0, 0)
    grid_spec = pltpu.PrefetchScalarGridSpec(
        num_scalar_prefetch=1, grid=(n_tiles,),
        in_specs=[pl.BlockSpec((TOP_K, tm), lambda i, pe: (0, i), memory_space=pltpu.SMEM),
                  pl.BlockSpec((tm * PACK_ROWS, HEAD_DIM), lambda i, pe: (i, 0)),
                  pl.BlockSpec((tm, D_MODEL), lambda i, pe: (i, 0)),
                  pl.BlockSpec((D_MODEL, EXPERT_DIM), fixed),
                  pl.BlockSpec((D_MODEL, EXPERT_DIM), fixed),
                  pl.BlockSpec((EXPERT_DIM, D_MODEL), fixed)],
        out_specs=[pl.BlockSpec(memory_space=pl.ANY),
                   pl.BlockSpec((tm, D_MODEL), lambda i, pe: (i, 0))],
        scratch_shapes=[pltpu.VMEM((BLOCK_WORDS, HEAD_DIM), _BF),
                        pltpu.SemaphoreType.DMA((2,))])
    return pl.pallas_call(
        functools.partial(_dispatch_body, tm=tm), grid_spec=grid_spec,
        out_shape=[jax.ShapeDtypeStruct((n_rows * PACK_ROWS, HEAD_DIM), _BF),
                   jax.ShapeDtypeStruct((t, D_MODEL), _F32)],
        compiler_params=_cparams(("arbitrary",)), name="dispatch",
    )(pad_end, dest, xp, xb, wg, wu, wd)


def _expert_body(be_ref, nv_ref, nu_ref, seg_ref, nxt_ref, x_ref, wg_hbm, wu_hbm, wd_hbm, y_ref,
                 wg_b, wu_b, wd_b, stage, wg_f, wu_f, wd_f, sem, *, layer):
    i = pl.program_id(0)
    n_valid = nv_ref[i]
    new_expert = (i == 0) | (be_ref[i] != be_ref[jnp.maximum(i - 1, 0)])

    def weight_copies(e, s):
        return [pltpu.make_async_copy(src.at[layer, e], dst.at[s], sem.at[s])
                for src, dst in ((wg_hbm, wg_f), (wu_hbm, wu_f), (wd_hbm, wd_f))]

    @pl.when(new_expert & (n_valid > 0))
    def _():
        s = seg_ref[i] % 2

        @pl.when(i == 0)
        def _():
            for cp in weight_copies(be_ref[0], 0):
                cp.start()

        for cp in weight_copies(be_ref[i], s):
            cp.wait()

        @pl.when(nxt_ref[i] >= 0)
        def _():
            for cp in weight_copies(nxt_ref[i], 1 - s):
                cp.start()

        _cast_rows(wg_f.at[s], wg_b)
        _cast_rows(wu_f.at[s], wu_b)
        _cast_rows(wd_f.at[s], wd_b)

    @pl.when(n_valid > 0)
    def _():
        for r in range(ROW_BLOCK):
            stage[r * SLAB_PITCH:r * SLAB_PITCH + PACK_ROWS, :] = (
                x_ref[r * PACK_ROWS:(r + 1) * PACK_ROWS, :].astype(_F32))
        n_slab = EXPERT_SPLIT // HEAD_DIM
        hg = None
        hu = None
        for c in range(D_MODEL // EXPERT_SPLIT):
            ks = slice(c * EXPERT_SPLIT, (c + 1) * EXPERT_SPLIT)
            xc = jnp.concatenate(
                [stage[pl.ds(c * n_slab + j, ROW_BLOCK, stride=SLAB_PITCH), :]
                 for j in range(n_slab)], axis=1).astype(_BF)
            pg = jnp.dot(xc, wg_b[ks, :], preferred_element_type=_F32)
            pu = jnp.dot(xc, wu_b[ks, :], preferred_element_type=_F32)
            hg = pg if hg is None else hg + pg
            hu = pu if hu is None else hu + pu
        hh = (_silu(hg) * hu).astype(_BF)
        pitch = ROW_BLOCK + STAGE_PAD
        for c in range(D_MODEL // EXPERT_SPLIT):
            yc = jnp.dot(hh, wd_b[:, c * EXPERT_SPLIT:(c + 1) * EXPERT_SPLIT],
                         preferred_element_type=_F32)
            for j in range(n_slab):
                jj = c * n_slab + j
                stage[jj * pitch:jj * pitch + ROW_BLOCK, :] = yc[:, j * HEAD_DIM:(j + 1) * HEAD_DIM]
        for r in range(ROW_BLOCK):
            y_ref[r * PACK_ROWS:(r + 1) * PACK_ROWS, :] = stage[pl.ds(r, PACK_ROWS, stride=pitch), :]

    @pl.when(n_valid == 0)
    def _():
        y_ref[...] = jnp.zeros_like(y_ref)


def _experts(blk_expert, n_valid, n_used, seg, nxt, xs, wg, wu, wd, layer):
    n_blocks = xs.shape[0] // BLOCK_WORDS
    x_map = lambda i, be, nv, nu, sg, nx: (jnp.minimum(i, nu[0] - 1), 0)
    grid_spec = pltpu.PrefetchScalarGridSpec(
        num_scalar_prefetch=5, grid=(n_blocks,),
        in_specs=[pl.BlockSpec((BLOCK_WORDS, HEAD_DIM), x_map),
                  pl.BlockSpec(memory_space=pl.ANY),
                  pl.BlockSpec(memory_space=pl.ANY),
                  pl.BlockSpec(memory_space=pl.ANY)],
        out_specs=pl.BlockSpec((BLOCK_WORDS, HEAD_DIM), lambda i, be, nv, nu, sg, nx: (i, 0)),
        scratch_shapes=[pltpu.VMEM((D_MODEL, EXPERT_DIM), _BF),
                        pltpu.VMEM((D_MODEL, EXPERT_DIM), _BF),
                        pltpu.VMEM((EXPERT_DIM, D_MODEL), _BF),
                        pltpu.VMEM((ROW_BLOCK * SLAB_PITCH, HEAD_DIM), _F32),
                        pltpu.VMEM((2, D_MODEL, EXPERT_DIM), _F32),
                        pltpu.VMEM((2, D_MODEL, EXPERT_DIM), _F32),
                        pltpu.VMEM((2, EXPERT_DIM, D_MODEL), _F32),
                        pltpu.SemaphoreType.DMA((2,))])
    return pl.pallas_call(
        functools.partial(_expert_body, layer=layer), grid_spec=grid_spec,
        out_shape=jax.ShapeDtypeStruct(xs.shape, _F32),
        compiler_params=_cparams(("arbitrary",)), name="experts",
    )(blk_expert, n_valid, n_used, seg, nxt, xs, wg, wu, wd)


COMBINE_ROWS = 32


def _combine_body(dcur_ref, dnxt_ref, y_hbm, gw_ref, x_ref, sh_ref, g_ref, b_ref,
                  o_ref, ob_ref, buf, sem, *, tm, n_tiles):
    i = pl.program_id(0)
    slot = i % 2
    n_rows = TOP_K * tm * PACK_ROWS

    def start_gather(d_ref, s, t):
        for k in range(TOP_K):
            src = pl.multiple_of(d_ref[k, t] * PACK_ROWS, PACK_ROWS)
            dst = pl.multiple_of((k * tm + t) * SLAB_PITCH, 8)
            pltpu.make_async_copy(y_hbm.at[pl.ds(src, PACK_ROWS), :],
                                  buf.at[s, pl.ds(dst, PACK_ROWS), :], sem.at[s]).start()

    def wait_gather(s):
        pltpu.make_async_copy(y_hbm.at[pl.ds(0, n_rows), :], buf.at[s, pl.ds(0, n_rows), :],
                              sem.at[s]).wait()

    @pl.when(i == 0)
    def _():
        def first(t, carry):
            start_gather(dcur_ref, 0, t)
            return carry
        lax.fori_loop(0, tm, first, 0)

    o_ref[...] = ALPHA * x_ref[...] + sh_ref[...]
    wait_gather(slot)
    rows_buf = buf.at[slot]

    def add_routed(c, carry):
        r0 = pl.multiple_of(c * COMBINE_ROWS, COMBINE_ROWS)
        for u in range(COMBINE_ROWS):
            start_gather(dnxt_ref, 1 - slot, r0 + u)
        rows = pl.ds(r0, COMBINE_ROWS)
        gate = gw_ref[rows, :]
        gates = [jnp.broadcast_to(gate[:, k:k + 1], (COMBINE_ROWS, HEAD_DIM)) for k in range(TOP_K)]
        for j in range(PACK_ROWS):
            cols = slice(j * HEAD_DIM, (j + 1) * HEAD_DIM)
            acc = o_ref[rows, cols]
            for k in range(TOP_K):
                start = (k * tm + r0) * SLAB_PITCH + j
                acc = acc + gates[k] * rows_buf[pl.ds(start, COMBINE_ROWS, stride=SLAB_PITCH), :]
            o_ref[rows, cols] = acc
        return carry
    lax.fori_loop(0, tm // COMBINE_ROWS, add_routed, 0)
    r = _layer_norm(o_ref[...], g_ref[...], b_ref[...])
    o_ref[...] = r
    ob_ref[...] = r.astype(_BF)

    @pl.when(i == n_tiles - 1)
    def _():
        wait_gather(1 - slot)


def _combine(dest, y, gw_t, x, shared, g, b, tm=128):
    t = x.shape[0]
    nt = t // tm
    row = lambda i: (i, 0)
    fixed = lambda i: (0, 0)
    return pl.pallas_call(
        functools.partial(_combine_body, tm=tm, n_tiles=nt),
        grid=(nt,),
        in_specs=[pl.BlockSpec((TOP_K, tm), lambda i: (0, i), memory_space=pltpu.SMEM),
                  pl.BlockSpec((TOP_K, tm), lambda i: (0, jnp.minimum(i + 1, nt - 1)),
                               memory_space=pltpu.SMEM),
                  pl.BlockSpec(memory_space=pl.ANY),
                  pl.BlockSpec((tm, TOP_K), row),
                  pl.BlockSpec((tm, D_MODEL), row), pl.BlockSpec((tm, D_MODEL), row),
                  pl.BlockSpec((1, D_MODEL), fixed), pl.BlockSpec((1, D_MODEL), fixed)],
        out_specs=[pl.BlockSpec((tm, D_MODEL), row), pl.BlockSpec((tm, D_MODEL), row)],
        out_shape=[jax.ShapeDtypeStruct((t, D_MODEL), _F32),
                   jax.ShapeDtypeStruct((t, D_MODEL), _BF)],
        scratch_shapes=[pltpu.VMEM((2, TOP_K * tm * SLAB_PITCH, HEAD_DIM), _F32),
                        pltpu.SemaphoreType.DMA((2,))],
        compiler_params=_cparams(("arbitrary",)), name="combine",
    )(dest, dest, y, gw_t, x, shared, g, b)


def _moe(x, xb, xp, layer, w_router, e_bias, w_gate, w_up, w_down, ws_gate, ws_up, ws_down, g, b):
    t = x.shape[0]
    eidx, gw, rank, counts = _router(x, w_router.T, e_bias.reshape(N_EXPERTS, 1))
    counts = counts.reshape(N_EXPERTS)
    padded = (counts + ROW_BLOCK - 1) // ROW_BLOCK * ROW_BLOCK
    pad_end = jnp.cumsum(padded)
    pad_start = pad_end - padded
    n_blocks = t * TOP_K // ROW_BLOCK + N_EXPERTS
    expert_ids = jnp.arange(N_EXPERTS, dtype=jnp.int32)
    dest = rank + jnp.sum(jnp.where(eidx[None] == expert_ids[:, None, None],
                                    pad_start[:, None, None], 0), axis=0)
    blk_start = jnp.arange(n_blocks, dtype=jnp.int32) * ROW_BLOCK
    blk_expert = jnp.minimum(jnp.sum(pad_end[None, :] <= blk_start[:, None], axis=1),
                             N_EXPERTS - 1).astype(jnp.int32)
    n_valid = jnp.clip((pad_start + counts)[blk_expert] - blk_start, 0, ROW_BLOCK)
    n_valid = jnp.where(blk_start < pad_end[-1], n_valid, 0).astype(jnp.int32)
    n_used = (pad_end[-1:] // ROW_BLOCK).astype(jnp.int32)
    xs, shared = _dispatch(pad_end.astype(jnp.int32), dest, xp, xb, ws_gate.astype(_BF),
                           ws_up.astype(_BF), ws_down.astype(_BF), n_blocks * ROW_BLOCK)
    used = blk_start < pad_end[-1]
    change = jnp.concatenate([jnp.zeros((1,), jnp.int32),
                              (blk_expert[1:] != blk_expert[:-1]).astype(jnp.int32)])
    seg = jnp.cumsum(change)
    later = jnp.where(used[None, :] & (seg[None, :] == seg[:, None] + 1), blk_expert[None, :], -1)
    nxt = jnp.max(later, axis=1).astype(jnp.int32)
    y = _experts(blk_expert, n_valid, n_used, seg.astype(jnp.int32), nxt, xs,
                 w_gate, w_up, w_down, layer)
    return _combine(dest, y, gw.T, x, shared, g.reshape(1, -1), b.reshape(1, -1))


def kernel(x, ln_mix_g, ln_mix_b, ln_ffn_g, ln_ffn_b, gmlp_w_in, gmlp_v_ln_g, gmlp_v_ln_b, gmlp_w_sp, gmlp_b_sp, gmlp_w_out, hgrn_w_in, hgrn_o_norm_g, hgrn_w_out, hgrn_lower_bounds, moe_w_router, moe_e_bias, moe_w_gate, moe_w_up, moe_w_down, moe_ws_gate, moe_ws_up, moe_ws_down):
    bsz, seq, d = x.shape
    t = bsz * seq
    row = lambda a: a.reshape(1, -1)
    xf = x.reshape(t, d)
    xb = xf.astype(_BF)

    z = _proj(xb, gmlp_w_in[0], _epi_gelu, [_BF], 2 * d, name="gmlp_in")[0]
    y = _sgu(z, row(gmlp_v_ln_g[0]), row(gmlp_v_ln_b[0]), gmlp_w_sp[0], gmlp_b_sp[0].T)
    xf, xb, xp = _out_ln(y, gmlp_w_out[0].astype(_BF), xf, row(ln_mix_g[0]), row(ln_mix_b[0]))
    xf, xb = _moe(xf, xb, xp, 0, moe_w_router[0], moe_e_bias[0], moe_w_gate, moe_w_up,
                     moe_w_down, moe_ws_gate[0], moe_ws_up[0], moe_ws_down[0],
                     ln_ffn_g[0], ln_ffn_b[0])

    lb_soft = jax.nn.softmax(hgrn_lower_bounds.astype(_F32), axis=0)
    lb = (jnp.cumsum(lb_soft, axis=0) - lb_soft[0])[1]
    w_in = hgrn_w_in[0]
    q_s = _proj(xb, w_in, _epi_silu, [_BF], d, col0=0, name="hgrn_q")[0]
    log_f, kk = _proj(xb, w_in, _epi_forget, [_F32, _BF], d, col0=d,
                      vecs=(row(jnp.log(lb)), row(jnp.log1p(-lb)), row(1.0 - lb)), name="hgrn_f")
    vv = _proj(xb, w_in, _epi_id, [_BF], d, col0=2 * d, name="hgrn_i")[0]
    g_s = _proj(xb, w_in, _epi_silu, [_BF], d, col0=3 * d, name="hgrn_g")[0]
    o = _recurrence(q_s, log_f, kk, vv, g_s, row(hgrn_o_norm_g[0]), bsz, seq)
    xf, xb, xp = _out_ln(o, hgrn_w_out[0].astype(_BF), xf, row(ln_mix_g[1]), row(ln_mix_b[1]))
    xf, xb = _moe(xf, xb, xp, 1, moe_w_router[1], moe_e_bias[1], moe_w_gate, moe_w_up,
                     moe_w_down, moe_ws_gate[1], moe_ws_up[1], moe_ws_down[1],
                     ln_ffn_g[1], ln_ffn_b[1])
    return xf.reshape(bsz, seq, d)
```

```python
import functools
import math

import numpy as np
import jax
import jax.numpy as jnp
from jax import lax
from jax.experimental import pallas as pl
from jax.experimental.pallas import tpu as pltpu

D_MODEL = 2048
N_HEADS = 16
HEAD_DIM = 128
GMLP_BLOCK = 128
STREAM_CHUNK = 64
N_EXPERTS = 64
TOP_K = 8
N_GROUPS = 8
GROUP_SIZE = N_EXPERTS // N_GROUPS
TOPK_GROUPS = 4
EXPERT_DIM = 512
ROUTED_SCALE = 2.5
LN_EPS = 1e-5
DEPTH = 2
ALPHA = (2 * DEPTH) ** 0.25
LOG2_E = math.log2(math.e)

ROW_BLOCK = 256
PACK_ROWS = D_MODEL // HEAD_DIM
REC_CHUNK = 128
REC_LEVELS = int(math.log2(REC_CHUNK))
VMEM_LIMIT = 56 * 1024 * 1024

_BF = jnp.bfloat16
_F32 = jnp.float32


def _cparams(sem):
    return pltpu.CompilerParams(dimension_semantics=sem, vmem_limit_bytes=VMEM_LIMIT)


def _sigmoid(x):
    return 1.0 / (1.0 + jnp.exp(-x))


def _silu(x):
    return x * _sigmoid(x)


STAGE_PAD = 8
SLAB_PITCH = PACK_ROWS + 8


def _slab_store(ref, stage, m, val):
    pitch = m + STAGE_PAD
    for j in range(PACK_ROWS):
        stage[j * pitch:j * pitch + m, :] = val[:, j * HEAD_DIM:(j + 1) * HEAD_DIM]

    for r in range(m):
        slab = stage[pl.ds(r, PACK_ROWS, stride=pitch), :]
        ref[r * PACK_ROWS:(r + 1) * PACK_ROWS, :] = slab.astype(_BF)


def _layer_norm(x, g, b):
    mu = jnp.mean(x, axis=-1, keepdims=True)
    xc = x - mu
    var = jnp.mean(xc * xc, axis=-1, keepdims=True)
    return xc * lax.rsqrt(var + LN_EPS) * g + b


CAST_ROWS = 256


def _cast_rows(src, dst):
    def step(i, carry):
        rows = pl.ds(pl.multiple_of(i * CAST_ROWS, CAST_ROWS), CAST_ROWS)
        dst[rows, :] = src[rows, :].astype(_BF)
        return carry
    lax.fori_loop(0, src.shape[0] // CAST_ROWS, step, 0)


def _proj_body(epi, n_vec, n_out, x_ref, w_ref, *refs):
    vecs = [r[...] for r in refs[:n_vec]]
    outs = refs[n_vec:n_vec + n_out]
    wb_ref = refs[n_vec + n_out]

    @pl.when(pl.program_id(1) == 0)
    def _():
        _cast_rows(w_ref, wb_ref)

    acc = jnp.dot(x_ref[...].astype(_BF), wb_ref[...], preferred_element_type=_F32)
    res = epi(acc, *vecs)
    for o_ref, r in zip(outs, res):
        o_ref[...] = r.astype(o_ref.dtype)


def _proj(x, w, epi, out_dtypes, n, col0=0, vecs=(), tm=512, tn=1024, name="proj"):
    m, k = x.shape
    grid = (n // tn, m // tm)
    c0 = col0 // tn
    in_specs = [pl.BlockSpec((tm, k), lambda j, i: (i, 0)),
                pl.BlockSpec((k, tn), lambda j, i: (0, c0 + j))]
    in_specs += [pl.BlockSpec((1, tn), lambda j, i: (0, j)) for _ in vecs]
    out_specs = [pl.BlockSpec((tm, tn), lambda j, i: (i, j)) for _ in out_dtypes]
    out_shape = [jax.ShapeDtypeStruct((m, n), dt) for dt in out_dtypes]
    return pl.pallas_call(
        functools.partial(_proj_body, epi, len(vecs), len(out_dtypes)),
        grid=grid, in_specs=in_specs, out_specs=out_specs, out_shape=out_shape,
        scratch_shapes=[pltpu.VMEM((k, tn), _BF)],
        compiler_params=_cparams(("arbitrary", "arbitrary")), name=name,
    )(x, w, *vecs)


def _epi_gelu(acc):
    return (0.5 * acc * (1.0 + lax.erf(acc * (1.0 / math.sqrt(2.0)))),)


def _epi_silu(acc):
    return (_silu(acc),)


def _epi_id(acc):
    return (acc,)


def _epi_forget(acc, log_lb, log1m_lb, one_m_lb):
    e = jnp.exp(-jnp.abs(acc))
    ls = jnp.minimum(acc, 0.0) - jnp.log(1.0 + e)
    c = log1m_lb + ls
    log_f = jnp.maximum(log_lb, c) + jnp.log(1.0 + jnp.exp(-jnp.abs(log_lb - c)))
    inv = 1.0 / (1.0 + e)
    k = one_m_lb * jnp.where(acc > 0.0, e * inv, inv)
    return log_f, k


def _sgu_body(u_ref, v_ref, g_ref, b_ref, wsp_ref, bsp_ref, y_ref, *, n_blk):
    vn = _layer_norm(v_ref[...].astype(_F32), g_ref[...], b_ref[...]).astype(_BF)
    ri = lax.broadcasted_iota(jnp.int32, (GMLP_BLOCK, GMLP_BLOCK), 0) // STREAM_CHUNK
    ci = lax.broadcasted_iota(jnp.int32, (GMLP_BLOCK, GMLP_BLOCK), 1) // STREAM_CHUNK
    causal = ri >= ci
    for h in range(N_HEADS):
        w = jnp.where(causal, wsp_ref[h], 0.0).astype(_BF)
        bias = bsp_ref[:, h:h + 1]
        cs = slice(h * HEAD_DIM, (h + 1) * HEAD_DIM)
        for n in range(n_blk):
            rs = slice(n * GMLP_BLOCK, (n + 1) * GMLP_BLOCK)
            sv = jnp.dot(w, vn[rs, cs], preferred_element_type=_F32) + bias
            y_ref[rs, cs] = (u_ref[rs, cs].astype(_F32) * sv).astype(_BF)


def _sgu(z, g, b, w_sp, bsp_t, tm=256):
    t = z.shape[0]
    return pl.pallas_call(
        functools.partial(_sgu_body, n_blk=tm // GMLP_BLOCK),
        grid=(t // tm,),
        in_specs=[pl.BlockSpec((tm, D_MODEL), lambda i: (i, 0)),
                  pl.BlockSpec((tm, D_MODEL), lambda i: (i, 1)),
                  pl.BlockSpec((1, D_MODEL), lambda i: (0, 0)),
                  pl.BlockSpec((1, D_MODEL), lambda i: (0, 0)),
                  pl.BlockSpec((N_HEADS, GMLP_BLOCK, GMLP_BLOCK), lambda i: (0, 0, 0)),
                  pl.BlockSpec((GMLP_BLOCK, N_HEADS), lambda i: (0, 0))],
        out_specs=pl.BlockSpec((tm, D_MODEL), lambda i: (i, 0)),
        out_shape=jax.ShapeDtypeStruct((t, D_MODEL), _BF),
        compiler_params=_cparams(("parallel",)), name="sgu",
    )(z, z, g, b, w_sp, bsp_t)


def _out_ln_body(y_ref, w_ref, x_ref, g_ref, b_ref, o_ref, ob_ref, op_ref, stage):
    h = jnp.dot(y_ref[...], w_ref[...], preferred_element_type=_F32)
    r = _layer_norm(ALPHA * x_ref[...] + h, g_ref[...], b_ref[...])
    o_ref[...] = r
    ob_ref[...] = r.astype(_BF)
    _slab_store(op_ref, stage, r.shape[0], r)


def _out_ln(y, w, x_res, g, b, tm=256):
    t = y.shape[0]
    row = lambda i: (i, 0)
    fixed = lambda i: (0, 0)
    return pl.pallas_call(
        _out_ln_body,
        grid=(t // tm,),
        in_specs=[pl.BlockSpec((tm, D_MODEL), row),
                  pl.BlockSpec((D_MODEL, D_MODEL), fixed),
                  pl.BlockSpec((tm, D_MODEL), row),
                  pl.BlockSpec((1, D_MODEL), fixed),
                  pl.BlockSpec((1, D_MODEL), fixed)],
        out_specs=[pl.BlockSpec((tm, D_MODEL), row), pl.BlockSpec((tm, D_MODEL), row),
                   pl.BlockSpec((tm * PACK_ROWS, HEAD_DIM), row)],
        out_shape=[jax.ShapeDtypeStruct((t, D_MODEL), _F32),
                   jax.ShapeDtypeStruct((t, D_MODEL), _BF),
                   jax.ShapeDtypeStruct((t * PACK_ROWS, HEAD_DIM), _BF)],
        scratch_shapes=[pltpu.VMEM((PACK_ROWS * (tm + STAGE_PAD), HEAD_DIM), _F32)],
        compiler_params=_cparams(("parallel",)), name="out_ln",
    )(y, w, x_res, g, b)


def _rec_constants():
    c = REC_CHUNK
    t = np.arange(c)[:, None]
    j = np.arange(c)[None, :]
    tri = (j <= t).astype(np.float32)
    pair = []
    for lv in range(REC_LEVELS):
        m = c >> (lv + 1)
        same_block = (t // (2 * m)) == (j // (2 * m))
        pair.append(same_block & (t % (2 * m) >= m) & (j % (2 * m) < m))
    pair = np.concatenate(pair, 0).astype(np.float32)
    return tri, pair


def _mid_rows(b, m, row):
    c = b.shape[0]
    if 2 * m >= 8:
        return jnp.concatenate([jnp.broadcast_to(b[s + m - 1:s + m, :], (2 * m, HEAD_DIM))
                                for s in range(0, c, 2 * m)], axis=0)
    prev1 = pltpu.roll(b, 1, 0)
    if m == 1:
        return jnp.where(row % 2 == 1, prev1, b)
    p = row % 4
    return jnp.where(p == 0, pltpu.roll(b, c - 1, 0),
                     jnp.where(p == 1, b, jnp.where(p == 2, prev1, pltpu.roll(b, 2, 0))))


def _rec_body(tri_ref, pair_ref, q_ref, lf_ref, k_ref, v_ref, gs_ref, gn_ref,
              o_ref, st_ref, *, hpb):
    c = REC_CHUNK
    nt = (((1,), (1,)), ((), ()))

    @pl.when(pl.program_id(2) == 0)
    def _():
        st_ref[...] = jnp.zeros_like(st_ref)

    tri = tri_ref[...]
    eye = (lax.broadcasted_iota(jnp.int32, (c, c), 0) ==
           lax.broadcasted_iota(jnp.int32, (c, c), 1))
    row = lax.broadcasted_iota(jnp.int32, (c, HEAD_DIM), 0)
    heads = range(hpb)
    cols = [slice(h * HEAD_DIM, (h + 1) * HEAD_DIM) for h in heads]
    q = [q_ref[:, cs].astype(_F32) for cs in cols]
    k = [k_ref[:, cs].astype(_F32) for cs in cols]
    v = [v_ref[:, cs] for cs in cols]

    b = []
    for cs in cols:
        lf = lf_ref[:, cs]
        l1 = lf.astype(_BF)
        r1 = lf - l1.astype(_F32)
        l2 = r1.astype(_BF)
        l3 = (r1 - l2.astype(_F32)).astype(_BF)
        b3 = jnp.dot(tri, jnp.concatenate([l1, l2, l3], axis=1), preferred_element_type=_F32)
        b.append((b3[:, :HEAD_DIM] + b3[:, HEAD_DIM:2 * HEAD_DIM] + b3[:, 2 * HEAD_DIM:]) * LOG2_E)
    b_last = [bh[c - 1:c, :] for bh in b]
    st = [st_ref[h] for h in heads]
    o = [lax.dot_general((q[h] * jnp.exp2(b[h])).astype(_BF), st[h].astype(_BF), nt,
                         preferred_element_type=_F32) for h in heads]
    scores = [jnp.where(eye, jnp.sum(q[h] * k[h], axis=1, keepdims=True), 0.0) for h in heads]
    for lv in range(REC_LEVELS):
        pair = pair_ref[lv * c:(lv + 1) * c, :]
        for h in heads:
            e = jnp.exp2(-jnp.abs(b[h] - _mid_rows(b[h], c >> (lv + 1), row)))
            s = lax.dot_general((q[h] * e).astype(_BF), (k[h] * e).astype(_BF), nt,
                                preferred_element_type=_F32)
            scores[h] = scores[h] + s * pair
    for h in heads:
        o[h] = o[h] + jnp.dot(scores[h].astype(_BF), v[h], preferred_element_type=_F32)
        k_dec = (k[h] * jnp.exp2(b_last[h] - b[h])).astype(_BF)
        upd = lax.dot_general(v[h], k_dec, (((0,), (0,)), ((), ())), preferred_element_type=_F32)
        st_ref[h] = jnp.exp2(b_last[h]) * st[h] + upd
    for h in heads:
        ms = jnp.mean(o[h] * o[h], axis=1, keepdims=True)
        oh = o[h] * lax.rsqrt(ms + LN_EPS) * gn_ref[...]
        o_ref[:, cols[h]] = (oh * gs_ref[:, cols[h]].astype(_F32)).astype(_BF)


def _recurrence(q_s, log_f, k, v, g_s, g_norm, bsz, seq, hpb=8):
    c = REC_CHUNK
    tri, pair = _rec_constants()
    tri = jnp.asarray(tri, _BF)
    pair = jnp.asarray(pair, _F32)
    w = hpb * HEAD_DIM
    n_c = seq // c
    fixed = lambda b, h, s: (0, 0)
    tile = lambda b, h, s: (b * n_c + s, h)
    return pl.pallas_call(
        functools.partial(_rec_body, hpb=hpb),
        grid=(bsz, N_HEADS // hpb, n_c),
        in_specs=[pl.BlockSpec(tri.shape, fixed), pl.BlockSpec(pair.shape, fixed),
                  pl.BlockSpec((c, w), tile), pl.BlockSpec((c, w), tile),
                  pl.BlockSpec((c, w), tile), pl.BlockSpec((c, w), tile),
                  pl.BlockSpec((c, w), tile), pl.BlockSpec((1, HEAD_DIM), fixed)],
        out_specs=pl.BlockSpec((c, w), tile),
        out_shape=jax.ShapeDtypeStruct((bsz * seq, D_MODEL), _BF),
        scratch_shapes=[pltpu.VMEM((hpb, HEAD_DIM, HEAD_DIM), _F32)],
        compiler_params=_cparams(("parallel", "parallel", "arbitrary")), name="hgrn_rec",
    )(tri, pair, q_s, log_f, k, v, g_s, g_norm)


def _router_body(x_ref, wr_ref, bias_ref, eidx_ref, gw_ref, rank_ref, cnt_ref, carry_ref, *, tm):
    @pl.when(pl.program_id(0) == 0)
    def _():
        carry_ref[...] = jnp.zeros_like(carry_ref)

    neg = -jnp.inf
    logits = lax.dot_general(wr_ref[...], x_ref[...], (((1,), (1,)), ((), ())),
                             precision=lax.Precision.HIGHEST,
                             preferred_element_type=_F32)
    scores = _sigmoid(logits)
    choice = scores + bias_ref[...]
    c3 = choice.reshape(N_GROUPS, GROUP_SIZE, tm)
    i3 = lax.broadcasted_iota(jnp.int32, c3.shape, 1)
    m1 = jnp.max(c3, axis=1, keepdims=True)
    first = jnp.min(jnp.where(c3 == m1, i3, GROUP_SIZE), axis=1, keepdims=True)
    m2 = jnp.max(jnp.where(i3 == first, neg, c3), axis=1, keepdims=True)
    gs = (m1 + m2).reshape(N_GROUPS, tm)
    ig = lax.broadcasted_iota(jnp.int32, gs.shape, 0)
    gsel = jnp.zeros(gs.shape, jnp.bool_)
    for _ in range(TOPK_GROUPS):
        m = jnp.max(gs, axis=0, keepdims=True)
        gi = jnp.min(jnp.where(gs == m, ig, N_GROUPS), axis=0, keepdims=True)
        hit = ig == gi
        gsel = gsel | hit
        gs = jnp.where(hit, neg, gs)
    allowed = jnp.broadcast_to(gsel.reshape(N_GROUPS, 1, tm), c3.shape).reshape(N_EXPERTS, tm)
    masked = jnp.where(allowed, choice, neg)
    ie = lax.broadcasted_iota(jnp.int32, masked.shape, 0)
    picked = jnp.zeros(masked.shape, _F32)
    hits, e_rows, w_rows = [], [], []
    for _ in range(TOP_K):
        m = jnp.max(masked, axis=0, keepdims=True)
        ei = jnp.min(jnp.where(masked == m, ie, N_EXPERTS), axis=0, keepdims=True)
        hit = ie == ei
        hits.append(hit)
        e_rows.append(ei)
        w_rows.append(jnp.sum(jnp.where(hit, scores, 0.0), axis=0, keepdims=True))
        picked = picked + hit.astype(_F32)
        masked = jnp.where(hit, neg, masked)
    gw = jnp.concatenate(w_rows, axis=0)
    gw = gw / jnp.sum(gw, axis=0, keepdims=True) * ROUTED_SCALE
    before = (lax.broadcasted_iota(jnp.int32, (tm, tm), 0) <
              lax.broadcasted_iota(jnp.int32, (tm, tm), 1)).astype(_BF)
    cum = jnp.dot(picked.astype(_BF), before, preferred_element_type=_F32) + carry_ref[...]
    r_rows = [jnp.sum(jnp.where(h, cum, 0.0), axis=0, keepdims=True) for h in hits]
    carry = carry_ref[...] + jnp.sum(picked, axis=1, keepdims=True)
    carry_ref[...] = carry
    eidx_ref[...] = jnp.concatenate(e_rows, axis=0)
    gw_ref[...] = gw
    rank_ref[...] = jnp.concatenate(r_rows, axis=0).astype(jnp.int32)
    cnt_ref[...] = carry.astype(jnp.int32)


def _router(x, wr_t, bias_col, tm=512):
    t = x.shape[0]
    tok = lambda i: (0, i)
    fixed = lambda i: (0, 0)
    return pl.pallas_call(
        functools.partial(_router_body, tm=tm),
        grid=(t // tm,),
        in_specs=[pl.BlockSpec((tm, D_MODEL), lambda i: (i, 0)),
                  pl.BlockSpec((N_EXPERTS, D_MODEL), fixed),
                  pl.BlockSpec((N_EXPERTS, 1), fixed)],
        out_specs=[pl.BlockSpec((TOP_K, tm), tok), pl.BlockSpec((TOP_K, tm), tok),
                   pl.BlockSpec((TOP_K, tm), tok), pl.BlockSpec((N_EXPERTS, 1), fixed)],
        out_shape=[jax.ShapeDtypeStruct((TOP_K, t), jnp.int32),
                   jax.ShapeDtypeStruct((TOP_K, t), _F32),
                   jax.ShapeDtypeStruct((TOP_K, t), jnp.int32),
                   jax.ShapeDtypeStruct((N_EXPERTS, 1), jnp.int32)],
        scratch_shapes=[pltpu.VMEM((N_EXPERTS, 1), _F32)],
        compiler_params=_cparams(("arbitrary",)), name="router",
    )(x, wr_t, bias_col)


BLOCK_WORDS = ROW_BLOCK * PACK_ROWS
EXPERT_SPLIT = 512


def _dispatch_body(pend_ref, dest_ref, xp_ref, xb_ref, wg_ref, wu_ref, wd_ref, xs_hbm, sh_ref,
                   zbuf, sem, *, tm):
    i = pl.program_id(0)

    def block_copy(b):
        start = pl.multiple_of(b * BLOCK_WORDS, BLOCK_WORDS)
        return pltpu.make_async_copy(zbuf, xs_hbm.at[pl.ds(start, BLOCK_WORDS), :], sem.at[1])

    def pad_copy(e):
        return block_copy(pend_ref[e] // ROW_BLOCK - 1)

    def has_rows(e):
        return pend_ref[e] > jnp.where(e > 0, pend_ref[jnp.maximum(e - 1, 0)], 0)

    @pl.when(i == 0)
    def _():
        zbuf[...] = jnp.zeros_like(zbuf)

        def start(e, carry):
            @pl.when(has_rows(e))
            def _():
                pad_copy(e).start()
            return carry
        lax.fori_loop(0, N_EXPERTS, start, 0)

        def wait(e, carry):
            @pl.when(has_rows(e))
            def _():
                pad_copy(e).wait()
            return carry
        lax.fori_loop(0, N_EXPERTS, wait, 0)

        first_unused = pend_ref[N_EXPERTS - 1] // ROW_BLOCK
        n_blocks = xs_hbm.shape[0] // BLOCK_WORDS

        def tail_start(b, carry):
            block_copy(b).start()
            return carry
        lax.fori_loop(first_unused, n_blocks, tail_start, 0)

        def tail_wait(b, carry):
            block_copy(b).wait()
            return carry
        lax.fori_loop(first_unused, n_blocks, tail_wait, 0)

    def push(t, carry):
        src = xp_ref.at[pl.ds(pl.multiple_of(t * PACK_ROWS, PACK_ROWS), PACK_ROWS), :]
        for k in range(TOP_K):
            row = pl.multiple_of(dest_ref[k, t] * PACK_ROWS, PACK_ROWS)
            pltpu.make_async_copy(src, xs_hbm.at[pl.ds(row, PACK_ROWS), :], sem.at[0]).start()
        return carry
    lax.fori_loop(0, tm, push, 0)
    xb = xb_ref[...]
    hg = jnp.dot(xb, wg_ref[...], preferred_element_type=_F32)
    hu = jnp.dot(xb, wu_ref[...], preferred_element_type=_F32)
    sh_ref[...] = jnp.dot((_silu(hg) * hu).astype(_BF), wd_ref[...], preferred_element_type=_F32)
    for k in range(TOP_K):
        pltpu.make_async_copy(xp_ref, xs_hbm.at[pl.ds(0, tm * PACK_ROWS), :], sem.at[0]).wait()


def _dispatch(pad_end, dest, xp, xb, wg, wu, wd, n_rows, tm=256):
    t = xb.shape[0]
    n_tiles = t // tm
    fixed = lambda i, pe: (0, 0)
    grid_spec = pltpu.PrefetchScalarGridSpec(
        num_scalar_prefetch=1, grid=(n_tiles,),
        in_specs=[pl.BlockSpec((TOP_K, tm), lambda i, pe: (0, i), memory_space=pltpu.SMEM),
                  pl.BlockSpec((tm * PACK_ROWS, HEAD_DIM), lambda i, pe: (i, 0)),
                  pl.BlockSpec((tm, D_MODEL), lambda i, pe: (i, 0)),
                  pl.BlockSpec((D_MODEL, EXPERT_DIM), fixed),
                  pl.BlockSpec((D_MODEL, EXPERT_DIM), fixed),
                  pl.BlockSpec((EXPERT_DIM, D_MODEL), fixed)],
        out_specs=[pl.BlockSpec(memory_space=pl.ANY),
                   pl.BlockSpec((tm, D_MODEL), lambda i, pe: (i, 0))],
        scratch_shapes=[pltpu.VMEM((BLOCK_WORDS, HEAD_DIM), _BF),
                        pltpu.SemaphoreType.DMA((2,))])
    return pl.pallas_call(
        functools.partial(_dispatch_body, tm=tm), grid_spec=grid_spec,
        out_shape=[jax.ShapeDtypeStruct((n_rows * PACK_ROWS, HEAD_DIM), _BF),
                   jax.ShapeDtypeStruct((t, D_MODEL), _F32)],
        compiler_params=_cparams(("arbitrary",)), name="dispatch",
    )(pad_end, dest, xp, xb, wg, wu, wd)


def _expert_body(be_ref, nv_ref, nu_ref, seg_ref, nxt_ref, x_ref, wg_hbm, wu_hbm, wd_hbm, y_ref,
                 wg_b, wu_b, wd_b, stage, wg_f, wu_f, wd_f, sem, *, layer):
    i = pl.program_id(0)
    n_valid = nv_ref[i]
    new_expert = (i == 0) | (be_ref[i] != be_ref[jnp.maximum(i - 1, 0)])

    def weight_copies(e, s):
        return [pltpu.make_async_copy(src.at[layer, e], dst.at[s], sem.at[s])
                for src, dst in ((wg_hbm, wg_f), (wu_hbm, wu_f), (wd_hbm, wd_f))]

    @pl.when(new_expert & (n_valid > 0))
    def _():
        s = seg_ref[i] % 2

        @pl.when(i == 0)
        def _():
            for cp in weight_copies(be_ref[0], 0):
                cp.start()

        for cp in weight_copies(be_ref[i], s):
            cp.wait()

        @pl.when(nxt_ref[i] >= 0)
        def _():
            for cp in weight_copies(nxt_ref[i], 1 - s):
                cp.start()

        _cast_rows(wg_f.at[s], wg_b)
        _cast_rows(wu_f.at[s], wu_b)
        _cast_rows(wd_f.at[s], wd_b)

    @pl.when(n_valid > 0)
    def _():
        for r in range(ROW_BLOCK):
            stage[r * SLAB_PITCH:r * SLAB_PITCH + PACK_ROWS, :] = (
                x_ref[r * PACK_ROWS:(r + 1) * PACK_ROWS, :].astype(_F32))
        n_slab = EXPERT_SPLIT // HEAD_DIM
        hg = None
        hu = None
        for c in range(D_MODEL // EXPERT_SPLIT):
            ks = slice(c * EXPERT_SPLIT, (c + 1) * EXPERT_SPLIT)
            xc = jnp.concatenate(
                [stage[pl.ds(c * n_slab + j, ROW_BLOCK, stride=SLAB_PITCH), :]
                 for j in range(n_slab)], axis=1).astype(_BF)
            pg = jnp.dot(xc, wg_b[ks, :], preferred_element_type=_F32)
            pu = jnp.dot(xc, wu_b[ks, :], preferred_element_type=_F32)
            hg = pg if hg is None else hg + pg
            hu = pu if hu is None else hu + pu
        hh = (_silu(hg) * hu).astype(_BF)
        pitch = ROW_BLOCK + STAGE_PAD
        for c in range(D_MODEL // EXPERT_SPLIT):
            yc = jnp.dot(hh, wd_b[:, c * EXPERT_SPLIT:(c + 1) * EXPERT_SPLIT],
                         preferred_element_type=_F32)
            for j in range(n_slab):
                jj = c * n_slab + j
                stage[jj * pitch:jj * pitch + ROW_BLOCK, :] = yc[:, j * HEAD_DIM:(j + 1) * HEAD_DIM]
        for r in range(ROW_BLOCK):
            y_ref[r * PACK_ROWS:(r + 1) * PACK_ROWS, :] = stage[pl.ds(r, PACK_ROWS, stride=pitch), :]

    @pl.when(n_valid == 0)
    def _():
        y_ref[...] = jnp.zeros_like(y_ref)


def _experts(blk_expert, n_valid, n_used, seg, nxt, xs, wg, wu, wd, layer):
    n_blocks = xs.shape[0] // BLOCK_WORDS
    x_map = lambda i, be, nv, nu, sg, nx: (jnp.minimum(i, nu[0] - 1), 0)
    grid_spec = pltpu.PrefetchScalarGridSpec(
        num_scalar_prefetch=5, grid=(n_blocks,),
        in_specs=[pl.BlockSpec((BLOCK_WORDS, HEAD_DIM), x_map),
                  pl.BlockSpec(memory_space=pl.ANY),
                  pl.BlockSpec(memory_space=pl.ANY),
                  pl.BlockSpec(memory_space=pl.ANY)],
        out_specs=pl.BlockSpec((BLOCK_WORDS, HEAD_DIM), lambda i, be, nv, nu, sg, nx: (i, 0)),
        scratch_shapes=[pltpu.VMEM((D_MODEL, EXPERT_DIM), _BF),
                        pltpu.VMEM((D_MODEL, EXPERT_DIM), _BF),
                        pltpu.VMEM((EXPERT_DIM, D_MODEL), _BF),
                        pltpu.VMEM((ROW_BLOCK * SLAB_PITCH, HEAD_DIM), _F32),
                        pltpu.VMEM((2, D_MODEL, EXPERT_DIM), _F32),
                        pltpu.VMEM((2, D_MODEL, EXPERT_DIM), _F32),
                        pltpu.VMEM((2, EXPERT_DIM, D_MODEL), _F32),
                        pltpu.SemaphoreType.DMA((2,))])
    return pl.pallas_call(
        functools.partial(_expert_body, layer=layer), grid_spec=grid_spec,
        out_shape=jax.ShapeDtypeStruct(xs.shape, _F32),
        compiler_params=_cparams(("arbitrary",)), name="experts",
    )(blk_expert, n_valid, n_used, seg, nxt, xs, wg, wu, wd)


COMBINE_ROWS = 32


def _combine_body(dcur_ref, dnxt_ref, y_hbm, gw_ref, x_ref, sh_ref, g_ref, b_ref,
                  o_ref, ob_ref, buf, sem, *, tm, n_tiles):
    i = pl.program_id(0)
    slot = i % 2
    n_rows = TOP_K * tm * PACK_ROWS

    def start_gather(d_ref, s, t):
        for k in range(TOP_K):
            src = pl.multiple_of(d_ref[k, t] * PACK_ROWS, PACK_ROWS)
            dst = pl.multiple_of((k * tm + t) * SLAB_PITCH, 8)
            pltpu.make_async_copy(y_hbm.at[pl.ds(src, PACK_ROWS), :],
                                  buf.at[s, pl.ds(dst, PACK_ROWS), :], sem.at[s]).start()

    def wait_gather(s):
        pltpu.make_async_copy(y_hbm.at[pl.ds(0, n_rows), :], buf.at[s, pl.ds(0, n_rows), :],
                              sem.at[s]).wait()

    @pl.when(i == 0)
    def _():
        def first(t, carry):
            start_gather(dcur_ref, 0, t)
            return carry
        lax.fori_loop(0, tm, first, 0)

    o_ref[...] = ALPHA * x_ref[...] + sh_ref[...]
    wait_gather(slot)
    rows_buf = buf.at[slot]

    def add_routed(c, carry):
        r0 = pl.multiple_of(c * COMBINE_ROWS, COMBINE_ROWS)
        for u in range(COMBINE_ROWS):
            start_gather(dnxt_ref, 1 - slot, r0 + u)
        rows = pl.ds(r0, COMBINE_ROWS)
        gate = gw_ref[rows, :]
        gates = [jnp.broadcast_to(gate[:, k:k + 1], (COMBINE_ROWS, HEAD_DIM)) for k in range(TOP_K)]
        for j in range(PACK_ROWS):
            cols = slice(j * HEAD_DIM, (j + 1) * HEAD_DIM)
            acc = o_ref[rows, cols]
            for k in range(TOP_K):
                start = (k * tm + r0) * SLAB_PITCH + j
                acc = acc + gates[k] * rows_buf[pl.ds(start, COMBINE_ROWS, stride=SLAB_PITCH), :]
            o_ref[rows, cols] = acc
        return carry
    lax.fori_loop(0, tm // COMBINE_ROWS, add_routed, 0)
    r = _layer_norm(o_ref[...], g_ref[...], b_ref[...])
    o_ref[...] = r
    ob_ref[...] = r.astype(_BF)

    @pl.when(i == n_tiles - 1)
    def _():
        wait_gather(1 - slot)


def _combine(dest, y, gw_t, x, shared, g, b, tm=128):
    t = x.shape[0]
    nt = t // tm
    row = lambda i: (i, 0)
    fixed = lambda i: (0, 0)
    return pl.pallas_call(
        functools.partial(_combine_body, tm=tm, n_tiles=nt),
        grid=(nt,),
        in_specs=[pl.BlockSpec((TOP_K, tm), lambda i: (0, i), memory_space=pltpu.SMEM),
                  pl.BlockSpec((TOP_K, tm), lambda i: (0, jnp.minimum(i + 1, nt - 1)),
                               memory_space=pltpu.SMEM),
                  pl.BlockSpec(memory_space=pl.ANY),
                  pl.BlockSpec((tm, TOP_K), row),
                  pl.BlockSpec((tm, D_MODEL), row), pl.BlockSpec((tm, D_MODEL), row),
                  pl.BlockSpec((1, D_MODEL), fixed), pl.BlockSpec((1, D_MODEL), fixed)],
        out_specs=[pl.BlockSpec((tm, D_MODEL), row), pl.BlockSpec((tm, D_MODEL), row)],
        out_shape=[jax.ShapeDtypeStruct((t, D_MODEL), _F32),
                   jax.ShapeDtypeStruct((t, D_MODEL), _BF)],
        scratch_shapes=[pltpu.VMEM((2, TOP_K * tm * SLAB_PITCH, HEAD_DIM), _F32),
                        pltpu.SemaphoreType.DMA((2,))],
        compiler_params=_cparams(("arbitrary",)), name="combine",
    )(dest, dest, y, gw_t, x, shared, g, b)


def _moe(x, xb, xp, layer, w_router, e_bias, w_gate, w_up, w_down, ws_gate, ws_up, ws_down, g, b):
    t = x.shape[0]
    eidx, gw, rank, counts = _router(x, w_router.T, e_bias.reshape(N_EXPERTS, 1))
    counts = counts.reshape(N_EXPERTS)
    padded = (counts + ROW_BLOCK - 1) // ROW_BLOCK * ROW_BLOCK
    pad_end = jnp.cumsum(padded)
    pad_start = pad_end - padded
    n_blocks = t * TOP_K // ROW_BLOCK + N_EXPERTS
    expert_ids = jnp.arange(N_EXPERTS, dtype=jnp.int32)
    dest = rank + jnp.sum(jnp.where(eidx[None] == expert_ids[:, None, None],
                                    pad_start[:, None, None], 0), axis=0)
    blk_start = jnp.arange(n_blocks, dtype=jnp.int32) * ROW_BLOCK
    blk_expert = jnp.minimum(jnp.sum(pad_end[None, :] <= blk_start[:, None], axis=1),
                             N_EXPERTS - 1).astype(jnp.int32)
    n_valid = jnp.clip((pad_start + counts)[blk_expert] - blk_start, 0, ROW_BLOCK)
    n_valid = jnp.where(blk_start < pad_end[-1], n_valid, 0).astype(jnp.int32)
    n_used = (pad_end[-1:] // ROW_BLOCK).astype(jnp.int32)
    xs, shared = _dispatch(pad_end.astype(jnp.int32), dest, xp, xb, ws_gate.astype(_BF),
                           ws_up.astype(_BF), ws_down.astype(_BF), n_blocks * ROW_BLOCK)
    used = blk_start < pad_end[-1]
    change = jnp.concatenate([jnp.zeros((1,), jnp.int32),
                              (blk_expert[1:] != blk_expert[:-1]).astype(jnp.int32)])
    seg = jnp.cumsum(change)
    later = jnp.where(used[None, :] & (seg[None, :] == seg[:, None] + 1), blk_expert[None, :], -1)
    nxt = jnp.max(later, axis=1).astype(jnp.int32)
    y = _experts(blk_expert, n_valid, n_used, seg.astype(jnp.int32), nxt, xs,
                 w_gate, w_up, w_down, layer)
    return _combine(dest, y, gw.T, x, shared, g.reshape(1, -1), b.reshape(1, -1))


def kernel(x, ln_mix_g, ln_mix_b, ln_ffn_g, ln_ffn_b, gmlp_w_in, gmlp_v_ln_g, gmlp_v_ln_b, gmlp_w_sp, gmlp_b_sp, gmlp_w_out, hgrn_w_in, hgrn_o_norm_g, hgrn_w_out, hgrn_lower_bounds, moe_w_router, moe_e_bias, moe_w_gate, moe_w_up, moe_w_down, moe_ws_gate, moe_ws_up, moe_ws_down):
    bsz, seq, d = x.shape
    t = bsz * seq
    row = lambda a: a.reshape(1, -1)
    xf = x.reshape(t, d)

    z = _proj(xf, gmlp_w_in[0], _epi_gelu, [_BF], 2 * d, name="gmlp_in")[0]
    y = _sgu(z, row(gmlp_v_ln_g[0]), row(gmlp_v_ln_b[0]), gmlp_w_sp[0], gmlp_b_sp[0].T)
    xf, xb, xp = _out_ln(y, gmlp_w_out[0].astype(_BF), xf, row(ln_mix_g[0]), row(ln_mix_b[0]))
    xf, xb = _moe(xf, xb, xp, 0, moe_w_router[0], moe_e_bias[0], moe_w_gate, moe_w_up,
                     moe_w_down, moe_ws_gate[0], moe_ws_up[0], moe_ws_down[0],
                     ln_ffn_g[0], ln_ffn_b[0])

    lb_soft = jax.nn.softmax(hgrn_lower_bounds.astype(_F32), axis=0)
    lb = (jnp.cumsum(lb_soft, axis=0) - lb_soft[0])[1]
    w_in = hgrn_w_in[0]
    q_s = _proj(xb, w_in, _epi_silu, [_BF], d, col0=0, name="hgrn_q")[0]
    log_f, kk = _proj(xb, w_in, _epi_forget, [_F32, _BF], d, col0=d,
                      vecs=(row(jnp.log(lb)), row(jnp.log1p(-lb)), row(1.0 - lb)), name="hgrn_f")
    vv = _proj(xb, w_in, _epi_id, [_BF], d, col0=2 * d, name="hgrn_i")[0]
    g_s = _proj(xb, w_in, _epi_silu, [_BF], d, col0=3 * d, name="hgrn_g")[0]
    o = _recurrence(q_s, log_f, kk, vv, g_s, row(hgrn_o_norm_g[0]), bsz, seq)
    xf, xb, xp = _out_ln(o, hgrn_w_out[0].astype(_BF), xf, row(ln_mix_g[1]), row(ln_mix_b[1]))
    xf, xb = _moe(xf, xb, xp, 1, moe_w_router[1], moe_e_bias[1], moe_w_gate, moe_w_up,
                     moe_w_down, moe_ws_gate[1], moe_ws_up[1], moe_ws_down[1],
                     ln_ffn_g[1], ln_ffn_b[1])
    return xf.reshape(bsz, seq, d)
```

```python
import functools
import math

import numpy as np
import jax
import jax.numpy as jnp
from jax import lax
from jax.experimental import pallas as pl
from jax.experimental.pallas import tpu as pltpu

D_MODEL = 2048
N_HEADS = 16
HEAD_DIM = 128
GMLP_BLOCK = 128
STREAM_CHUNK = 64
N_EXPERTS = 64
TOP_K = 8
N_GROUPS = 8
GROUP_SIZE = N_EXPERTS // N_GROUPS
TOPK_GROUPS = 4
EXPERT_DIM = 512
ROUTED_SCALE = 2.5
LN_EPS = 1e-5
DEPTH = 2
ALPHA = (2 * DEPTH) ** 0.25
LOG2_E = math.log2(math.e)

ROW_BLOCK = 256
PACK_ROWS = D_MODEL // HEAD_DIM
REC_CHUNK = 128
REC_LEVELS = int(math.log2(REC_CHUNK))
VMEM_LIMIT = 56 * 1024 * 1024

_BF = jnp.bfloat16
_F32 = jnp.float32


def _cparams(sem):
    return pltpu.CompilerParams(dimension_semantics=sem, vmem_limit_bytes=VMEM_LIMIT)


def _sigmoid(x):
    return 1.0 / (1.0 + jnp.exp(-x))


def _silu(x):
    return x * _sigmoid(x)


STAGE_PAD = 8
SLAB_PITCH = PACK_ROWS + 8


def _slab_store(ref, stage, m, val):
    pitch = m + STAGE_PAD
    for j in range(PACK_ROWS):
        stage[j * pitch:j * pitch + m, :] = val[:, j * HEAD_DIM:(j + 1) * HEAD_DIM]

    for r in range(m):
        slab = stage[pl.ds(r, PACK_ROWS, stride=pitch), :]
        ref[r * PACK_ROWS:(r + 1) * PACK_ROWS, :] = slab.astype(_BF)


def _layer_norm(x, g, b):
    mu = jnp.mean(x, axis=-1, keepdims=True)
    xc = x - mu
    var = jnp.mean(xc * xc, axis=-1, keepdims=True)
    return xc * lax.rsqrt(var + LN_EPS) * g + b


CAST_ROWS = 256


def _cast_rows(src, dst):
    def step(i, carry):
        rows = pl.ds(pl.multiple_of(i * CAST_ROWS, CAST_ROWS), CAST_ROWS)
        dst[rows, :] = src[rows, :].astype(_BF)
        return carry
    lax.fori_loop(0, src.shape[0] // CAST_ROWS, step, 0)


def _proj_body(epi, n_vec, n_out, x_ref, w_ref, *refs):
    vecs = [r[...] for r in refs[:n_vec]]
    outs = refs[n_vec:n_vec + n_out]
    wb_ref = refs[n_vec + n_out]

    @pl.when(pl.program_id(1) == 0)
    def _():
        _cast_rows(w_ref, wb_ref)

    acc = jnp.dot(x_ref[...].astype(_BF), wb_ref[...], preferred_element_type=_F32)
    res = epi(acc, *vecs)
    for o_ref, r in zip(outs, res):
        o_ref[...] = r.astype(o_ref.dtype)


def _proj(x, w, epi, out_dtypes, n, col0=0, vecs=(), tm=512, tn=1024, name="proj"):
    m, k = x.shape
    grid = (n // tn, m // tm)
    c0 = col0 // tn
    in_specs = [pl.BlockSpec((tm, k), lambda j, i: (i, 0)),
                pl.BlockSpec((k, tn), lambda j, i: (0, c0 + j))]
    in_specs += [pl.BlockSpec((1, tn), lambda j, i: (0, j)) for _ in vecs]
    out_specs = [pl.BlockSpec((tm, tn), lambda j, i: (i, j)) for _ in out_dtypes]
    out_shape = [jax.ShapeDtypeStruct((m, n), dt) for dt in out_dtypes]
    return pl.pallas_call(
        functools.partial(_proj_body, epi, len(vecs), len(out_dtypes)),
        grid=grid, in_specs=in_specs, out_specs=out_specs, out_shape=out_shape,
        scratch_shapes=[pltpu.VMEM((k, tn), _BF)],
        compiler_params=_cparams(("arbitrary", "arbitrary")), name=name,
    )(x, w, *vecs)


def _epi_gelu(acc):
    return (0.5 * acc * (1.0 + lax.erf(acc * (1.0 / math.sqrt(2.0)))),)


def _epi_silu(acc):
    return (_silu(acc),)


def _epi_id(acc):
    return (acc,)


def _epi_forget(acc, log_lb, log1m_lb, one_m_lb):
    e = jnp.exp(-jnp.abs(acc))
    ls = jnp.minimum(acc, 0.0) - jnp.log(1.0 + e)
    c = log1m_lb + ls
    log_f = jnp.maximum(log_lb, c) + jnp.log(1.0 + jnp.exp(-jnp.abs(log_lb - c)))
    inv = 1.0 / (1.0 + e)
    k = one_m_lb * jnp.where(acc > 0.0, e * inv, inv)
    return log_f, k


def _sgu_body(u_ref, v_ref, g_ref, b_ref, wsp_ref, bsp_ref, y_ref, *, n_blk):
    vn = _layer_norm(v_ref[...].astype(_F32), g_ref[...], b_ref[...]).astype(_BF)
    ri = lax.broadcasted_iota(jnp.int32, (GMLP_BLOCK, GMLP_BLOCK), 0) // STREAM_CHUNK
    ci = lax.broadcasted_iota(jnp.int32, (GMLP_BLOCK, GMLP_BLOCK), 1) // STREAM_CHUNK
    causal = ri >= ci
    for h in range(N_HEADS):
        w = jnp.where(causal, wsp_ref[h], 0.0).astype(_BF)
        bias = bsp_ref[:, h:h + 1]
        cs = slice(h * HEAD_DIM, (h + 1) * HEAD_DIM)
        for n in range(n_blk):
            rs = slice(n * GMLP_BLOCK, (n + 1) * GMLP_BLOCK)
            sv = jnp.dot(w, vn[rs, cs], preferred_element_type=_F32) + bias
            y_ref[rs, cs] = (u_ref[rs, cs].astype(_F32) * sv).astype(_BF)


def _sgu(z, g, b, w_sp, bsp_t, tm=256):
    t = z.shape[0]
    return pl.pallas_call(
        functools.partial(_sgu_body, n_blk=tm // GMLP_BLOCK),
        grid=(t // tm,),
        in_specs=[pl.BlockSpec((tm, D_MODEL), lambda i: (i, 0)),
                  pl.BlockSpec((tm, D_MODEL), lambda i: (i, 1)),
                  pl.BlockSpec((1, D_MODEL), lambda i: (0, 0)),
                  pl.BlockSpec((1, D_MODEL), lambda i: (0, 0)),
                  pl.BlockSpec((N_HEADS, GMLP_BLOCK, GMLP_BLOCK), lambda i: (0, 0, 0)),
                  pl.BlockSpec((GMLP_BLOCK, N_HEADS), lambda i: (0, 0))],
        out_specs=pl.BlockSpec((tm, D_MODEL), lambda i: (i, 0)),
        out_shape=jax.ShapeDtypeStruct((t, D_MODEL), _BF),
        compiler_params=_cparams(("parallel",)), name="sgu",
    )(z, z, g, b, w_sp, bsp_t)


def _out_ln_body(y_ref, w_ref, x_ref, g_ref, b_ref, o_ref, ob_ref, op_ref, stage):
    h = jnp.dot(y_ref[...], w_ref[...], preferred_element_type=_F32)
    r = _layer_norm(ALPHA * x_ref[...] + h, g_ref[...], b_ref[...])
    o_ref[...] = r
    ob_ref[...] = r.astype(_BF)
    _slab_store(op_ref, stage, r.shape[0], r)


def _out_ln(y, w, x_res, g, b, tm=256):
    t = y.shape[0]
    row = lambda i: (i, 0)
    fixed = lambda i: (0, 0)
    return pl.pallas_call(
        _out_ln_body,
        grid=(t // tm,),
        in_specs=[pl.BlockSpec((tm, D_MODEL), row),
                  pl.BlockSpec((D_MODEL, D_MODEL), fixed),
                  pl.BlockSpec((tm, D_MODEL), row),
                  pl.BlockSpec((1, D_MODEL), fixed),
                  pl.BlockSpec((1, D_MODEL), fixed)],
        out_specs=[pl.BlockSpec((tm, D_MODEL), row), pl.BlockSpec((tm, D_MODEL), row),
                   pl.BlockSpec((tm * PACK_ROWS, HEAD_DIM), row)],
        out_shape=[jax.ShapeDtypeStruct((t, D_MODEL), _F32),
                   jax.ShapeDtypeStruct((t, D_MODEL), _BF),
                   jax.ShapeDtypeStruct((t * PACK_ROWS, HEAD_DIM), _BF)],
        scratch_shapes=[pltpu.VMEM((PACK_ROWS * (tm + STAGE_PAD), HEAD_DIM), _F32)],
        compiler_params=_cparams(("parallel",)), name="out_ln",
    )(y, w, x_res, g, b)


def _rec_constants():
    c = REC_CHUNK
    t = np.arange(c)[:, None]
    j = np.arange(c)[None, :]
    tri = (j <= t).astype(np.float32)
    pair = []
    for lv in range(REC_LEVELS):
        m = c >> (lv + 1)
        same_block = (t // (2 * m)) == (j // (2 * m))
        pair.append(same_block & (t % (2 * m) >= m) & (j % (2 * m) < m))
    pair = np.concatenate(pair, 0).astype(np.float32)
    return tri, pair


def _mid_rows(b, m, row):
    c = b.shape[0]
    if 2 * m >= 8:
        return jnp.concatenate([jnp.broadcast_to(b[s + m - 1:s + m, :], (2 * m, HEAD_DIM))
                                for s in range(0, c, 2 * m)], axis=0)
    prev1 = pltpu.roll(b, 1, 0)
    if m == 1:
        return jnp.where(row % 2 == 1, prev1, b)
    p = row % 4
    return jnp.where(p == 0, pltpu.roll(b, c - 1, 0),
                     jnp.where(p == 1, b, jnp.where(p == 2, prev1, pltpu.roll(b, 2, 0))))


def _rec_body(tri_ref, pair_ref, q_ref, lf_ref, k_ref, v_ref, gs_ref, gn_ref,
              o_ref, st_ref, *, hpb):
    c = REC_CHUNK
    nt = (((1,), (1,)), ((), ()))

    @pl.when(pl.program_id(2) == 0)
    def _():
        st_ref[...] = jnp.zeros_like(st_ref)

    tri = tri_ref[...]
    eye = (lax.broadcasted_iota(jnp.int32, (c, c), 0) ==
           lax.broadcasted_iota(jnp.int32, (c, c), 1))
    row = lax.broadcasted_iota(jnp.int32, (c, HEAD_DIM), 0)
    heads = range(hpb)
    cols = [slice(h * HEAD_DIM, (h + 1) * HEAD_DIM) for h in heads]
    q = [q_ref[:, cs].astype(_F32) for cs in cols]
    k = [k_ref[:, cs].astype(_F32) for cs in cols]
    v = [v_ref[:, cs] for cs in cols]

    b = []
    for cs in cols:
        lf = lf_ref[:, cs]
        l1 = lf.astype(_BF)
        r1 = lf - l1.astype(_F32)
        l2 = r1.astype(_BF)
        l3 = (r1 - l2.astype(_F32)).astype(_BF)
        b3 = jnp.dot(tri, jnp.concatenate([l1, l2, l3], axis=1), preferred_element_type=_F32)
        b.append((b3[:, :HEAD_DIM] + b3[:, HEAD_DIM:2 * HEAD_DIM] + b3[:, 2 * HEAD_DIM:]) * LOG2_E)
    b_last = [bh[c - 1:c, :] for bh in b]
    st = [st_ref[h] for h in heads]
    o = [lax.dot_general((q[h] * jnp.exp2(b[h])).astype(_BF), st[h].astype(_BF), nt,
                         preferred_element_type=_F32) for h in heads]
    scores = [jnp.where(eye, jnp.sum(q[h] * k[h], axis=1, keepdims=True), 0.0) for h in heads]
    for lv in range(REC_LEVELS):
        pair = pair_ref[lv * c:(lv + 1) * c, :]
        for h in heads:
            e = jnp.exp2(-jnp.abs(b[h] - _mid_rows(b[h], c >> (lv + 1), row)))
            s = lax.dot_general((q[h] * e).astype(_BF), (k[h] * e).astype(_BF), nt,
                                preferred_element_type=_F32)
            scores[h] = scores[h] + s * pair
    for h in heads:
        o[h] = o[h] + jnp.dot(scores[h].astype(_BF), v[h], preferred_element_type=_F32)
        k_dec = (k[h] * jnp.exp2(b_last[h] - b[h])).astype(_BF)
        upd = lax.dot_general(v[h], k_dec, (((0,), (0,)), ((), ())), preferred_element_type=_F32)
        st_ref[h] = jnp.exp2(b_last[h]) * st[h] + upd
    for h in heads:
        ms = jnp.mean(o[h] * o[h], axis=1, keepdims=True)
        oh = o[h] * lax.rsqrt(ms + LN_EPS) * gn_ref[...]
        o_ref[:, cols[h]] = (oh * gs_ref[:, cols[h]].astype(_F32)).astype(_BF)


def _recurrence(q_s, log_f, k, v, g_s, g_norm, bsz, seq, hpb=8):
    c = REC_CHUNK
    tri, pair = _rec_constants()
    tri = jnp.asarray(tri, _BF)
    pair = jnp.asarray(pair, _F32)
    w = hpb * HEAD_DIM
    n_c = seq // c
    fixed = lambda b, h, s: (0, 0)
    tile = lambda b, h, s: (b * n_c + s, h)
    return pl.pallas_call(
        functools.partial(_rec_body, hpb=hpb),
        grid=(bsz, N_HEADS // hpb, n_c),
        in_specs=[pl.BlockSpec(tri.shape, fixed), pl.BlockSpec(pair.shape, fixed),
                  pl.BlockSpec((c, w), tile), pl.BlockSpec((c, w), tile),
                  pl.BlockSpec((c, w), tile), pl.BlockSpec((c, w), tile),
                  pl.BlockSpec((c, w), tile), pl.BlockSpec((1, HEAD_DIM), fixed)],
        out_specs=pl.BlockSpec((c, w), tile),
        out_shape=jax.ShapeDtypeStruct((bsz * seq, D_MODEL), _BF),
        scratch_shapes=[pltpu.VMEM((hpb, HEAD_DIM, HEAD_DIM), _F32)],
        compiler_params=_cparams(("parallel", "parallel", "arbitrary")), name="hgrn_rec",
    )(tri, pair, q_s, log_f, k, v, g_s, g_norm)


def _router_body(x_ref, wr_ref, bias_ref, eidx_ref, gw_ref, rank_ref, cnt_ref, carry_ref, *, tm):
    @pl.when(pl.program_id(0) == 0)
    def _():
        carry_ref[...] = jnp.zeros_like(carry_ref)

    neg = -jnp.inf
    nt = (((1,), (1,)), ((), ()))
    x = x_ref[...]
    x_hi = x.astype(_BF)
    x_lo = (x - x_hi.astype(_F32)).astype(_BF)
    wr = wr_ref[...]
    w_hi = wr.astype(_BF)
    w_lo = (wr - w_hi.astype(_F32)).astype(_BF)
    logits = (lax.dot_general(w_hi, x_hi, nt, preferred_element_type=_F32) +
              lax.dot_general(w_hi, x_lo, nt, preferred_element_type=_F32) +
              lax.dot_general(w_lo, x_hi, nt, preferred_element_type=_F32))
    scores = _sigmoid(logits)
    choice = scores + bias_ref[...]
    c3 = choice.reshape(N_GROUPS, GROUP_SIZE, tm)
    i3 = lax.broadcasted_iota(jnp.int32, c3.shape, 1)
    m1 = jnp.max(c3, axis=1, keepdims=True)
    first = jnp.min(jnp.where(c3 == m1, i3, GROUP_SIZE), axis=1, keepdims=True)
    m2 = jnp.max(jnp.where(i3 == first, neg, c3), axis=1, keepdims=True)
    gs = (m1 + m2).reshape(N_GROUPS, tm)
    ig = lax.broadcasted_iota(jnp.int32, gs.shape, 0)
    gsel = jnp.zeros(gs.shape, jnp.bool_)
    for _ in range(TOPK_GROUPS):
        m = jnp.max(gs, axis=0, keepdims=True)
        gi = jnp.min(jnp.where(gs == m, ig, N_GROUPS), axis=0, keepdims=True)
        hit = ig == gi
        gsel = gsel | hit
        gs = jnp.where(hit, neg, gs)
    allowed = jnp.broadcast_to(gsel.reshape(N_GROUPS, 1, tm), c3.shape).reshape(N_EXPERTS, tm)
    masked = jnp.where(allowed, choice, neg)
    ie = lax.broadcasted_iota(jnp.int32, masked.shape, 0)
    picked = jnp.zeros(masked.shape, _F32)
    hits, e_rows, w_rows = [], [], []
    for _ in range(TOP_K):
        m = jnp.max(masked, axis=0, keepdims=True)
        ei = jnp.min(jnp.where(masked == m, ie, N_EXPERTS), axis=0, keepdims=True)
        hit = ie == ei
        hits.append(hit)
        e_rows.append(ei)
        w_rows.append(jnp.sum(jnp.where(hit, scores, 0.0), axis=0, keepdims=True))
        picked = picked + hit.astype(_F32)
        masked = jnp.where(hit, neg, masked)
    gw = jnp.concatenate(w_rows, axis=0)
    gw = gw / jnp.sum(gw, axis=0, keepdims=True) * ROUTED_SCALE
    before = (lax.broadcasted_iota(jnp.int32, (tm, tm), 0) <
              lax.broadcasted_iota(jnp.int32, (tm, tm), 1)).astype(_BF)
    cum = jnp.dot(picked.astype(_BF), before, preferred_element_type=_F32) + carry_ref[...]
    r_rows = [jnp.sum(jnp.where(h, cum, 0.0), axis=0, keepdims=True) for h in hits]
    carry = carry_ref[...] + jnp.sum(picked, axis=1, keepdims=True)
    carry_ref[...] = carry
    eidx_ref[...] = jnp.concatenate(e_rows, axis=0)
    gw_ref[...] = gw
    rank_ref[...] = jnp.concatenate(r_rows, axis=0).astype(jnp.int32)
    cnt_ref[...] = carry.astype(jnp.int32)


def _router(x, wr_t, bias_col, tm=512):
    t = x.shape[0]
    tok = lambda i: (0, i)
    fixed = lambda i: (0, 0)
    return pl.pallas_call(
        functools.partial(_router_body, tm=tm),
        grid=(t // tm,),
        in_specs=[pl.BlockSpec((tm, D_MODEL), lambda i: (i, 0)),
                  pl.BlockSpec((N_EXPERTS, D_MODEL), fixed),
                  pl.BlockSpec((N_EXPERTS, 1), fixed)],
        out_specs=[pl.BlockSpec((TOP_K, tm), tok), pl.BlockSpec((TOP_K, tm), tok),
                   pl.BlockSpec((TOP_K, tm), tok), pl.BlockSpec((N_EXPERTS, 1), fixed)],
        out_shape=[jax.ShapeDtypeStruct((TOP_K, t), jnp.int32),
                   jax.ShapeDtypeStruct((TOP_K, t), _F32),
                   jax.ShapeDtypeStruct((TOP_K, t), jnp.int32),
                   jax.ShapeDtypeStruct((N_EXPERTS, 1), jnp.int32)],
        scratch_shapes=[pltpu.VMEM((N_EXPERTS, 1), _F32)],
        compiler_params=_cparams(("arbitrary",)), name="router",
    )(x, wr_t, bias_col)


BLOCK_WORDS = ROW_BLOCK * PACK_ROWS
EXPERT_SPLIT = 512


def _dispatch_body(pend_ref, dest_ref, xp_ref, xb_ref, wg_ref, wu_ref, wd_ref, xs_hbm, sh_ref,
                   zbuf, sem, *, tm):
    i = pl.program_id(0)

    def block_copy(b):
        start = pl.multiple_of(b * BLOCK_WORDS, BLOCK_WORDS)
        return pltpu.make_async_copy(zbuf, xs_hbm.at[pl.ds(start, BLOCK_WORDS), :], sem.at[1])

    def pad_copy(e):
        return block_copy(pend_ref[e] // ROW_BLOCK - 1)

    def has_rows(e):
        return pend_ref[e] > jnp.where(e > 0, pend_ref[jnp.maximum(e - 1, 0)], 0)

    @pl.when(i == 0)
    def _():
        zbuf[...] = jnp.zeros_like(zbuf)

        def start(e, carry):
            @pl.when(has_rows(e))
            def _():
                pad_copy(e).start()
            return carry
        lax.fori_loop(0, N_EXPERTS, start, 0)

        def wait(e, carry):
            @pl.when(has_rows(e))
            def _():
                pad_copy(e).wait()
            return carry
        lax.fori_loop(0, N_EXPERTS, wait, 0)

        first_unused = pend_ref[N_EXPERTS - 1] // ROW_BLOCK
        n_blocks = xs_hbm.shape[0] // BLOCK_WORDS

        def tail_start(b, carry):
            block_copy(b).start()
            return carry
        lax.fori_loop(first_unused, n_blocks, tail_start, 0)

        def tail_wait(b, carry):
            block_copy(b).wait()
            return carry
        lax.fori_loop(first_unused, n_blocks, tail_wait, 0)

    def push(t, carry):
        src = xp_ref.at[pl.ds(pl.multiple_of(t * PACK_ROWS, PACK_ROWS), PACK_ROWS), :]
        for k in range(TOP_K):
            row = pl.multiple_of(dest_ref[k, t] * PACK_ROWS, PACK_ROWS)
            pltpu.make_async_copy(src, xs_hbm.at[pl.ds(row, PACK_ROWS), :], sem.at[0]).start()
        return carry
    lax.fori_loop(0, tm, push, 0)
    xb = xb_ref[...]
    hg = jnp.dot(xb, wg_ref[...], preferred_element_type=_F32)
    hu = jnp.dot(xb, wu_ref[...], preferred_element_type=_F32)
    sh_ref[...] = jnp.dot((_silu(hg) * hu).astype(_BF), wd_ref[...], preferred_element_type=_F32)
    for k in range(TOP_K):
        pltpu.make_async_copy(xp_ref, xs_hbm.at[pl.ds(0, tm * PACK_ROWS), :], sem.at[0]).wait()


def _dispatch(pad_end, dest, xp, xb, wg, wu, wd, n_rows, tm=256):
    t = xb.shape[0]
    n_tiles = t // tm
    fixed = lambda i, pe: (0, 0)
    grid_spec = pltpu.PrefetchScalarGridSpec(
        num_scalar_prefetch=1, grid=(n_tiles,),
        in_specs=[pl.BlockSpec((TOP_K, tm), lambda i, pe: (0, i), memory_space=pltpu.SMEM),
                  pl.BlockSpec((tm * PACK_ROWS, HEAD_DIM), lambda i, pe: (i, 0)),
                  pl.BlockSpec((tm, D_MODEL), lambda i, pe: (i, 0)),
                  pl.BlockSpec((D_MODEL, EXPERT_DIM), fixed),
                  pl.BlockSpec((D_MODEL, EXPERT_DIM), fixed),
                  pl.BlockSpec((EXPERT_DIM, D_MODEL), fixed)],
        out_specs=[pl.BlockSpec(memory_space=pl.ANY),
                   pl.BlockSpec((tm, D_MODEL), lambda i, pe: (i, 0))],
        scratch_shapes=[pltpu.VMEM((BLOCK_WORDS, HEAD_DIM), _BF),
                        pltpu.SemaphoreType.DMA((2,))])
    return pl.pallas_call(
        functools.partial(_dispatch_body, tm=tm), grid_spec=grid_spec,
        out_shape=[jax.ShapeDtypeStruct((n_rows * PACK_ROWS, HEAD_DIM), _BF),
                   jax.ShapeDtypeStruct((t, D_MODEL), _F32)],
        compiler_params=_cparams(("arbitrary",)), name="dispatch",
    )(pad_end, dest, xp, xb, wg, wu, wd)


def _expert_body(be_ref, nv_ref, nu_ref, seg_ref, nxt_ref, x_ref, wg_hbm, wu_hbm, wd_hbm, y_ref,
                 wg_b, wu_b, wd_b, stage, wg_f, wu_f, wd_f, sem, *, layer):
    i = pl.program_id(0)
    n_valid = nv_ref[i]
    new_expert = (i == 0) | (be_ref[i] != be_ref[jnp.maximum(i - 1, 0)])

    def weight_copies(e, s):
        return [pltpu.make_async_copy(src.at[layer, e], dst.at[s], sem.at[s])
                for src, dst in ((wg_hbm, wg_f), (wu_hbm, wu_f), (wd_hbm, wd_f))]

    @pl.when(new_expert & (n_valid > 0))
    def _():
        s = seg_ref[i] % 2

        @pl.when(i == 0)
        def _():
            for cp in weight_copies(be_ref[0], 0):
                cp.start()

        for cp in weight_copies(be_ref[i], s):
            cp.wait()

        @pl.when(nxt_ref[i] >= 0)
        def _():
            for cp in weight_copies(nxt_ref[i], 1 - s):
                cp.start()

        _cast_rows(wg_f.at[s], wg_b)
        _cast_rows(wu_f.at[s], wu_b)
        _cast_rows(wd_f.at[s], wd_b)

    @pl.when(n_valid > 0)
    def _():
        for r in range(ROW_BLOCK):
            stage[r * SLAB_PITCH:r * SLAB_PITCH + PACK_ROWS, :] = (
                x_ref[r * PACK_ROWS:(r + 1) * PACK_ROWS, :].astype(_F32))
        n_slab = EXPERT_SPLIT // HEAD_DIM
        hg = None
        hu = None
        for c in range(D_MODEL // EXPERT_SPLIT):
            ks = slice(c * EXPERT_SPLIT, (c + 1) * EXPERT_SPLIT)
            xc = jnp.concatenate(
                [stage[pl.ds(c * n_slab + j, ROW_BLOCK, stride=SLAB_PITCH), :]
                 for j in range(n_slab)], axis=1).astype(_BF)
            pg = jnp.dot(xc, wg_b[ks, :], preferred_element_type=_F32)
            pu = jnp.dot(xc, wu_b[ks, :], preferred_element_type=_F32)
            hg = pg if hg is None else hg + pg
            hu = pu if hu is None else hu + pu
        hh = (_silu(hg) * hu).astype(_BF)
        pitch = ROW_BLOCK + STAGE_PAD
        for c in range(D_MODEL // EXPERT_SPLIT):
            yc = jnp.dot(hh, wd_b[:, c * EXPERT_SPLIT:(c + 1) * EXPERT_SPLIT],
                         preferred_element_type=_F32)
            for j in range(n_slab):
                jj = c * n_slab + j
                stage[jj * pitch:jj * pitch + ROW_BLOCK, :] = yc[:, j * HEAD_DIM:(j + 1) * HEAD_DIM]
        for r in range(ROW_BLOCK):
            y_ref[r * PACK_ROWS:(r + 1) * PACK_ROWS, :] = stage[pl.ds(r, PACK_ROWS, stride=pitch), :]

    @pl.when(n_valid == 0)
    def _():
        y_ref[...] = jnp.zeros_like(y_ref)


def _experts(blk_expert, n_valid, n_used, seg, nxt, xs, wg, wu, wd, layer):
    n_blocks = xs.shape[0] // BLOCK_WORDS
    x_map = lambda i, be, nv, nu, sg, nx: (jnp.minimum(i, nu[0] - 1), 0)
    grid_spec = pltpu.PrefetchScalarGridSpec(
        num_scalar_prefetch=5, grid=(n_blocks,),
        in_specs=[pl.BlockSpec((BLOCK_WORDS, HEAD_DIM), x_map),
                  pl.BlockSpec(memory_space=pl.ANY),
                  pl.BlockSpec(memory_space=pl.ANY),
                  pl.BlockSpec(memory_space=pl.ANY)],
        out_specs=pl.BlockSpec((BLOCK_WORDS, HEAD_DIM), lambda i, be, nv, nu, sg, nx: (i, 0)),
        scratch_shapes=[pltpu.VMEM((D_MODEL, EXPERT_DIM), _BF),
                        pltpu.VMEM((D_MODEL, EXPERT_DIM), _BF),
                        pltpu.VMEM((EXPERT_DIM, D_MODEL), _BF),
                        pltpu.VMEM((ROW_BLOCK * SLAB_PITCH, HEAD_DIM), _F32),
                        pltpu.VMEM((2, D_MODEL, EXPERT_DIM), _F32),
                        pltpu.VMEM((2, D_MODEL, EXPERT_DIM), _F32),
                        pltpu.VMEM((2, EXPERT_DIM, D_MODEL), _F32),
                        pltpu.SemaphoreType.DMA((2,))])
    return pl.pallas_call(
        functools.partial(_expert_body, layer=layer), grid_spec=grid_spec,
        out_shape=jax.ShapeDtypeStruct(xs.shape, _F32),
        compiler_params=_cparams(("arbitrary",)), name="experts",
    )(blk_expert, n_valid, n_used, seg, nxt, xs, wg, wu, wd)


COMBINE_ROWS = 32


def _combine_body(dcur_ref, dnxt_ref, y_hbm, gw_ref, x_ref, sh_ref, g_ref, b_ref,
                  o_ref, ob_ref, buf, sem, *, tm, n_tiles):
    i = pl.program_id(0)
    slot = i % 2
    n_rows = TOP_K * tm * PACK_ROWS

    def start_gather(d_ref, s, t):
        for k in range(TOP_K):
            src = pl.multiple_of(d_ref[k, t] * PACK_ROWS, PACK_ROWS)
            dst = pl.multiple_of((k * tm + t) * SLAB_PITCH, 8)
            pltpu.make_async_copy(y_hbm.at[pl.ds(src, PACK_ROWS), :],
                                  buf.at[s, pl.ds(dst, PACK_ROWS), :], sem.at[s]).start()

    def wait_gather(s):
        pltpu.make_async_copy(y_hbm.at[pl.ds(0, n_rows), :], buf.at[s, pl.ds(0, n_rows), :],
                              sem.at[s]).wait()

    @pl.when(i == 0)
    def _():
        def first(t, carry):
            start_gather(dcur_ref, 0, t)
            return carry
        lax.fori_loop(0, tm, first, 0)

    o_ref[...] = ALPHA * x_ref[...] + sh_ref[...]
    wait_gather(slot)
    rows_buf = buf.at[slot]

    def add_routed(c, carry):
        r0 = pl.multiple_of(c * COMBINE_ROWS, COMBINE_ROWS)
        for u in range(COMBINE_ROWS):
            start_gather(dnxt_ref, 1 - slot, r0 + u)
        rows = pl.ds(r0, COMBINE_ROWS)
        gate = gw_ref[rows, :]
        gates = [jnp.broadcast_to(gate[:, k:k + 1], (COMBINE_ROWS, HEAD_DIM)) for k in range(TOP_K)]
        for j in range(PACK_ROWS):
            cols = slice(j * HEAD_DIM, (j + 1) * HEAD_DIM)
            acc = o_ref[rows, cols]
            for k in range(TOP_K):
                start = (k * tm + r0) * SLAB_PITCH + j
                acc = acc + gates[k] * rows_buf[pl.ds(start, COMBINE_ROWS, stride=SLAB_PITCH), :]
            o_ref[rows, cols] = acc
        return carry
    lax.fori_loop(0, tm // COMBINE_ROWS, add_routed, 0)
    r = _layer_norm(o_ref[...], g_ref[...], b_ref[...])
    o_ref[...] = r
    ob_ref[...] = r.astype(_BF)

    @pl.when(i == n_tiles - 1)
    def _():
        wait_gather(1 - slot)


def _combine(dest, y, gw_t, x, shared, g, b, tm=128):
    t = x.shape[0]
    nt = t // tm
    row = lambda i: (i, 0)
    fixed = lambda i: (0, 0)
    return pl.pallas_call(
        functools.partial(_combine_body, tm=tm, n_tiles=nt),
        grid=(nt,),
        in_specs=[pl.BlockSpec((TOP_K, tm), lambda i: (0, i), memory_space=pltpu.SMEM),
                  pl.BlockSpec((TOP_K, tm), lambda i: (0, jnp.minimum(i + 1, nt - 1)),
                               memory_space=pltpu.SMEM),
                  pl.BlockSpec(memory_space=pl.ANY),
                  pl.BlockSpec((tm, TOP_K), row),
                  pl.BlockSpec((tm, D_MODEL), row), pl.BlockSpec((tm, D_MODEL), row),
                  pl.BlockSpec((1, D_MODEL), fixed), pl.BlockSpec((1, D_MODEL), fixed)],
        out_specs=[pl.BlockSpec((tm, D_MODEL), row), pl.BlockSpec((tm, D_MODEL), row)],
        out_shape=[jax.ShapeDtypeStruct((t, D_MODEL), _F32),
                   jax.ShapeDtypeStruct((t, D_MODEL), _BF)],
        scratch_shapes=[pltpu.VMEM((2, TOP_K * tm * SLAB_PITCH, HEAD_DIM), _F32),
                        pltpu.SemaphoreType.DMA((2,))],
        compiler_params=_cparams(("arbitrary",)), name="combine",
    )(dest, dest, y, gw_t, x, shared, g, b)


def _moe(x, xb, xp, layer, w_router, e_bias, w_gate, w_up, w_down, ws_gate, ws_up, ws_down, g, b):
    t = x.shape[0]
    eidx, gw, rank, counts = _router(x, w_router.T, e_bias.reshape(N_EXPERTS, 1))
    counts = counts.reshape(N_EXPERTS)
    padded = (counts + ROW_BLOCK - 1) // ROW_BLOCK * ROW_BLOCK
    pad_end = jnp.cumsum(padded)
    pad_start = pad_end - padded
    n_blocks = t * TOP_K // ROW_BLOCK + N_EXPERTS
    expert_ids = jnp.arange(N_EXPERTS, dtype=jnp.int32)
    dest = rank + jnp.sum(jnp.where(eidx[None] == expert_ids[:, None, None],
                                    pad_start[:, None, None], 0), axis=0)
    blk_start = jnp.arange(n_blocks, dtype=jnp.int32) * ROW_BLOCK
    blk_expert = jnp.minimum(jnp.sum(pad_end[None, :] <= blk_start[:, None], axis=1),
                             N_EXPERTS - 1).astype(jnp.int32)
    n_valid = jnp.clip((pad_start + counts)[blk_expert] - blk_start, 0, ROW_BLOCK)
    n_valid = jnp.where(blk_start < pad_end[-1], n_valid, 0).astype(jnp.int32)
    n_used = (pad_end[-1:] // ROW_BLOCK).astype(jnp.int32)
    xs, shared = _dispatch(pad_end.astype(jnp.int32), dest, xp, xb, ws_gate.astype(_BF),
                           ws_up.astype(_BF), ws_down.astype(_BF), n_blocks * ROW_BLOCK)
    used = blk_start < pad_end[-1]
    change = jnp.concatenate([jnp.zeros((1,), jnp.int32),
                              (blk_expert[1:] != blk_expert[:-1]).astype(jnp.int32)])
    seg = jnp.cumsum(change)
    later = jnp.where(used[None, :] & (seg[None, :] == seg[:, None] + 1), blk_expert[None, :], -1)
    nxt = jnp.max(later, axis=1).astype(jnp.int32)
    y = _experts(blk_expert, n_valid, n_used, seg.astype(jnp.int32), nxt, xs,
                 w_gate, w_up, w_down, layer)
    return _combine(dest, y, gw.T, x, shared, g.reshape(1, -1), b.reshape(1, -1))


def kernel(x, ln_mix_g, ln_mix_b, ln_ffn_g, ln_ffn_b, gmlp_w_in, gmlp_v_ln_g, gmlp_v_ln_b, gmlp_w_sp, gmlp_b_sp, gmlp_w_out, hgrn_w_in, hgrn_o_norm_g, hgrn_w_out, hgrn_lower_bounds, moe_w_router, moe_e_bias, moe_w_gate, moe_w_up, moe_w_down, moe_ws_gate, moe_ws_up, moe_ws_down):
    bsz, seq, d = x.shape
    t = bsz * seq
    row = lambda a: a.reshape(1, -1)
    xf = x.reshape(t, d)

    z = _proj(xf, gmlp_w_in[0], _epi_gelu, [_BF], 2 * d, name="gmlp_in")[0]
    y = _sgu(z, row(gmlp_v_ln_g[0]), row(gmlp_v_ln_b[0]), gmlp_w_sp[0], gmlp_b_sp[0].T)
    xf, xb, xp = _out_ln(y, gmlp_w_out[0].astype(_BF), xf, row(ln_mix_g[0]), row(ln_mix_b[0]))
    xf, xb = _moe(xf, xb, xp, 0, moe_w_router[0], moe_e_bias[0], moe_w_gate, moe_w_up,
                     moe_w_down, moe_ws_gate[0], moe_ws_up[0], moe_ws_down[0],
                     ln_ffn_g[0], ln_ffn_b[0])

    lb_soft = jax.nn.softmax(hgrn_lower_bounds.astype(_F32), axis=0)
    lb = (jnp.cumsum(lb_soft, axis=0) - lb_soft[0])[1]
    w_in = hgrn_w_in[0]
    q_s = _proj(xb, w_in, _epi_silu, [_BF], d, col0=0, name="hgrn_q")[0]
    log_f, kk = _proj(xb, w_in, _epi_forget, [_F32, _BF], d, col0=d,
                      vecs=(row(jnp.log(lb)), row(jnp.log1p(-lb)), row(1.0 - lb)), name="hgrn_f")
    vv = _proj(xb, w_in, _epi_id, [_BF], d, col0=2 * d, name="hgrn_i")[0]
    g_s = _proj(xb, w_in, _epi_silu, [_BF], d, col0=3 * d, name="hgrn_g")[0]
    o = _recurrence(q_s, log_f, kk, vv, g_s, row(hgrn_o_norm_g[0]), bsz, seq)
    xf, xb, xp = _out_ln(o, hgrn_w_out[0].astype(_BF), xf, row(ln_mix_g[1]), row(ln_mix_b[1]))
    xf, xb = _moe(xf, xb, xp, 1, moe_w_router[1], moe_e_bias[1], moe_w_gate, moe_w_up,
                     moe_w_down, moe_ws_gate[1], moe_ws_up[1], moe_ws_down[1],
                     ln_ffn_g[1], ln_ffn_b[1])
    return xf.reshape(bsz, seq, d)
```

```python
import functools
import math

import numpy as np
import jax
import jax.numpy as jnp
from jax import lax
from jax.experimental import pallas as pl
from jax.experimental.pallas import tpu as pltpu

D_MODEL = 2048
N_HEADS = 16
HEAD_DIM = 128
GMLP_BLOCK = 128
STREAM_CHUNK = 64
N_EXPERTS = 64
TOP_K = 8
N_GROUPS = 8
GROUP_SIZE = N_EXPERTS // N_GROUPS
TOPK_GROUPS = 4
EXPERT_DIM = 512
ROUTED_SCALE = 2.5
LN_EPS = 1e-5
DEPTH = 2
ALPHA = (2 * DEPTH) ** 0.25
LOG2_E = math.log2(math.e)

ROW_BLOCK = 256
PACK_ROWS = D_MODEL // HEAD_DIM
REC_CHUNK = 128
REC_LEVELS = int(math.log2(REC_CHUNK))
VMEM_LIMIT = 56 * 1024 * 1024

_BF = jnp.bfloat16
_F32 = jnp.float32


def _cparams(sem):
    return pltpu.CompilerParams(dimension_semantics=sem, vmem_limit_bytes=VMEM_LIMIT)


def _sigmoid(x):
    return 1.0 / (1.0 + jnp.exp(-x))


def _silu(x):
    return x * _sigmoid(x)


STAGE_PAD = 8
SLAB_PITCH = PACK_ROWS + 8


def _slab_store(ref, stage, m, val):
    pitch = m + STAGE_PAD
    for j in range(PACK_ROWS):
        stage[j * pitch:j * pitch + m, :] = val[:, j * HEAD_DIM:(j + 1) * HEAD_DIM]

    for r in range(m):
        slab = stage[pl.ds(r, PACK_ROWS, stride=pitch), :]
        ref[r * PACK_ROWS:(r + 1) * PACK_ROWS, :] = slab.astype(_BF)


def _layer_norm(x, g, b):
    mu = jnp.mean(x, axis=-1, keepdims=True)
    xc = x - mu
    var = jnp.mean(xc * xc, axis=-1, keepdims=True)
    return xc * lax.rsqrt(var + LN_EPS) * g + b


CAST_ROWS = 256


def _cast_rows(src, dst):
    def step(i, carry):
        rows = pl.ds(pl.multiple_of(i * CAST_ROWS, CAST_ROWS), CAST_ROWS)
        dst[rows, :] = src[rows, :].astype(_BF)
        return carry
    lax.fori_loop(0, src.shape[0] // CAST_ROWS, step, 0)


def _proj_body(epi, n_vec, n_out, x_ref, w_ref, *refs):
    vecs = [r[...] for r in refs[:n_vec]]
    outs = refs[n_vec:n_vec + n_out]
    wb_ref = refs[n_vec + n_out]

    @pl.when(pl.program_id(1) == 0)
    def _():
        _cast_rows(w_ref, wb_ref)

    acc = jnp.dot(x_ref[...].astype(_BF), wb_ref[...], preferred_element_type=_F32)
    res = epi(acc, *vecs)
    for o_ref, r in zip(outs, res):
        o_ref[...] = r.astype(o_ref.dtype)


def _proj(x, w, epi, out_dtypes, n, col0=0, vecs=(), tm=512, tn=1024, name="proj"):
    m, k = x.shape
    grid = (n // tn, m // tm)
    c0 = col0 // tn
    in_specs = [pl.BlockSpec((tm, k), lambda j, i: (i, 0)),
                pl.BlockSpec((k, tn), lambda j, i: (0, c0 + j))]
    in_specs += [pl.BlockSpec((1, tn), lambda j, i: (0, j)) for _ in vecs]
    out_specs = [pl.BlockSpec((tm, tn), lambda j, i: (i, j)) for _ in out_dtypes]
    out_shape = [jax.ShapeDtypeStruct((m, n), dt) for dt in out_dtypes]
    return pl.pallas_call(
        functools.partial(_proj_body, epi, len(vecs), len(out_dtypes)),
        grid=grid, in_specs=in_specs, out_specs=out_specs, out_shape=out_shape,
        scratch_shapes=[pltpu.VMEM((k, tn), _BF)],
        compiler_params=_cparams(("arbitrary", "arbitrary")), name=name,
    )(x, w, *vecs)


def _epi_gelu(acc):
    return (0.5 * acc * (1.0 + lax.erf(acc * (1.0 / math.sqrt(2.0)))),)


def _epi_silu(acc):
    return (_silu(acc),)


def _epi_id(acc):
    return (acc,)


def _epi_forget(acc, log_lb, log1m_lb, one_m_lb):
    e = jnp.exp(-jnp.abs(acc))
    ls = jnp.minimum(acc, 0.0) - jnp.log(1.0 + e)
    c = log1m_lb + ls
    log_f = jnp.maximum(log_lb, c) + jnp.log(1.0 + jnp.exp(-jnp.abs(log_lb - c)))
    inv = 1.0 / (1.0 + e)
    k = one_m_lb * jnp.where(acc > 0.0, e * inv, inv)
    return log_f, k


def _sgu_body(u_ref, v_ref, g_ref, b_ref, wsp_ref, bsp_ref, y_ref, *, n_blk):
    vn = _layer_norm(v_ref[...].astype(_F32), g_ref[...], b_ref[...]).astype(_BF)
    ri = lax.broadcasted_iota(jnp.int32, (GMLP_BLOCK, GMLP_BLOCK), 0) // STREAM_CHUNK
    ci = lax.broadcasted_iota(jnp.int32, (GMLP_BLOCK, GMLP_BLOCK), 1) // STREAM_CHUNK
    causal = ri >= ci
    for h in range(N_HEADS):
        w = jnp.where(causal, wsp_ref[h], 0.0).astype(_BF)
        bias = bsp_ref[:, h:h + 1]
        cs = slice(h * HEAD_DIM, (h + 1) * HEAD_DIM)
        for n in range(n_blk):
            rs = slice(n * GMLP_BLOCK, (n + 1) * GMLP_BLOCK)
            sv = jnp.dot(w, vn[rs, cs], preferred_element_type=_F32) + bias
            y_ref[rs, cs] = (u_ref[rs, cs].astype(_F32) * sv).astype(_BF)


def _sgu(z, g, b, w_sp, bsp_t, tm=256):
    t = z.shape[0]
    return pl.pallas_call(
        functools.partial(_sgu_body, n_blk=tm // GMLP_BLOCK),
        grid=(t // tm,),
        in_specs=[pl.BlockSpec((tm, D_MODEL), lambda i: (i, 0)),
                  pl.BlockSpec((tm, D_MODEL), lambda i: (i, 1)),
                  pl.BlockSpec((1, D_MODEL), lambda i: (0, 0)),
                  pl.BlockSpec((1, D_MODEL), lambda i: (0, 0)),
                  pl.BlockSpec((N_HEADS, GMLP_BLOCK, GMLP_BLOCK), lambda i: (0, 0, 0)),
                  pl.BlockSpec((GMLP_BLOCK, N_HEADS), lambda i: (0, 0))],
        out_specs=pl.BlockSpec((tm, D_MODEL), lambda i: (i, 0)),
        out_shape=jax.ShapeDtypeStruct((t, D_MODEL), _BF),
        compiler_params=_cparams(("parallel",)), name="sgu",
    )(z, z, g, b, w_sp, bsp_t)


def _out_ln_body(y_ref, w_ref, x_ref, g_ref, b_ref, o_ref, ob_ref, op_ref, stage):
    h = jnp.dot(y_ref[...], w_ref[...], preferred_element_type=_F32)
    r = _layer_norm(ALPHA * x_ref[...] + h, g_ref[...], b_ref[...])
    o_ref[...] = r
    ob_ref[...] = r.astype(_BF)
    _slab_store(op_ref, stage, r.shape[0], r)


def _out_ln(y, w, x_res, g, b, tm=256):
    t = y.shape[0]
    row = lambda i: (i, 0)
    fixed = lambda i: (0, 0)
    return pl.pallas_call(
        _out_ln_body,
        grid=(t // tm,),
        in_specs=[pl.BlockSpec((tm, D_MODEL), row),
                  pl.BlockSpec((D_MODEL, D_MODEL), fixed),
                  pl.BlockSpec((tm, D_MODEL), row),
                  pl.BlockSpec((1, D_MODEL), fixed),
                  pl.BlockSpec((1, D_MODEL), fixed)],
        out_specs=[pl.BlockSpec((tm, D_MODEL), row), pl.BlockSpec((tm, D_MODEL), row),
                   pl.BlockSpec((tm * PACK_ROWS, HEAD_DIM), row)],
        out_shape=[jax.ShapeDtypeStruct((t, D_MODEL), _F32),
                   jax.ShapeDtypeStruct((t, D_MODEL), _BF),
                   jax.ShapeDtypeStruct((t * PACK_ROWS, HEAD_DIM), _BF)],
        scratch_shapes=[pltpu.VMEM((PACK_ROWS * (tm + STAGE_PAD), HEAD_DIM), _F32)],
        compiler_params=_cparams(("parallel",)), name="out_ln",
    )(y, w, x_res, g, b)


def _rec_constants():
    c = REC_CHUNK
    t = np.arange(c)[:, None]
    j = np.arange(c)[None, :]
    tri = (j <= t).astype(np.float32)
    pair = []
    for lv in range(REC_LEVELS):
        m = c >> (lv + 1)
        same_block = (t // (2 * m)) == (j // (2 * m))
        pair.append(same_block & (t % (2 * m) >= m) & (j % (2 * m) < m))
    pair = np.concatenate(pair, 0).astype(np.float32)
    return tri, pair


def _mid_rows(b, m, row):
    c = b.shape[0]
    if 2 * m >= 8:
        return jnp.concatenate([jnp.broadcast_to(b[s + m - 1:s + m, :], (2 * m, HEAD_DIM))
                                for s in range(0, c, 2 * m)], axis=0)
    prev1 = pltpu.roll(b, 1, 0)
    if m == 1:
        return jnp.where(row % 2 == 1, prev1, b)
    p = row % 4
    return jnp.where(p == 0, pltpu.roll(b, c - 1, 0),
                     jnp.where(p == 1, b, jnp.where(p == 2, prev1, pltpu.roll(b, 2, 0))))


def _rec_body(tri_ref, pair_ref, q_ref, lf_ref, k_ref, v_ref, gs_ref, gn_ref,
              o_ref, st_ref, *, hpb):
    c = REC_CHUNK
    nt = (((1,), (1,)), ((), ()))

    @pl.when(pl.program_id(2) == 0)
    def _():
        st_ref[...] = jnp.zeros_like(st_ref)

    tri = tri_ref[...]
    eye = (lax.broadcasted_iota(jnp.int32, (c, c), 0) ==
           lax.broadcasted_iota(jnp.int32, (c, c), 1))
    row = lax.broadcasted_iota(jnp.int32, (c, HEAD_DIM), 0)
    heads = range(hpb)
    cols = [slice(h * HEAD_DIM, (h + 1) * HEAD_DIM) for h in heads]
    q = [q_ref[:, cs].astype(_F32) for cs in cols]
    k = [k_ref[:, cs].astype(_F32) for cs in cols]
    v = [v_ref[:, cs] for cs in cols]

    b = []
    for cs in cols:
        lf = lf_ref[:, cs]
        l1 = lf.astype(_BF)
        r1 = lf - l1.astype(_F32)
        l2 = r1.astype(_BF)
        l3 = (r1 - l2.astype(_F32)).astype(_BF)
        b3 = jnp.dot(tri, jnp.concatenate([l1, l2, l3], axis=1), preferred_element_type=_F32)
        b.append((b3[:, :HEAD_DIM] + b3[:, HEAD_DIM:2 * HEAD_DIM] + b3[:, 2 * HEAD_DIM:]) * LOG2_E)
    b_last = [bh[c - 1:c, :] for bh in b]
    st = [st_ref[h] for h in heads]
    o = [lax.dot_general((q[h] * jnp.exp2(b[h])).astype(_BF), st[h].astype(_BF), nt,
                         preferred_element_type=_F32) for h in heads]
    scores = [jnp.where(eye, jnp.sum(q[h] * k[h], axis=1, keepdims=True), 0.0) for h in heads]
    for lv in range(REC_LEVELS):
        pair = pair_ref[lv * c:(lv + 1) * c, :]
        for h in heads:
            e = jnp.exp2(-jnp.abs(b[h] - _mid_rows(b[h], c >> (lv + 1), row)))
            s = lax.dot_general((q[h] * e).astype(_BF), (k[h] * e).astype(_BF), nt,
                                preferred_element_type=_F32)
            scores[h] = scores[h] + s * pair
    for h in heads:
        o[h] = o[h] + jnp.dot(scores[h].astype(_BF), v[h], preferred_element_type=_F32)
        k_dec = (k[h] * jnp.exp2(b_last[h] - b[h])).astype(_BF)
        upd = lax.dot_general(v[h], k_dec, (((0,), (0,)), ((), ())), preferred_element_type=_F32)
        st_ref[h] = jnp.exp2(b_last[h]) * st[h] + upd
    for h in heads:
        ms = jnp.mean(o[h] * o[h], axis=1, keepdims=True)
        oh = o[h] * lax.rsqrt(ms + LN_EPS) * gn_ref[...]
        o_ref[:, cols[h]] = (oh * gs_ref[:, cols[h]].astype(_F32)).astype(_BF)


def _recurrence(q_s, log_f, k, v, g_s, g_norm, bsz, seq, hpb=8):
    c = REC_CHUNK
    tri, pair = _rec_constants()
    tri = jnp.asarray(tri, _BF)
    pair = jnp.asarray(pair, _F32)
    w = hpb * HEAD_DIM
    n_c = seq // c
    fixed = lambda b, h, s: (0, 0)
    tile = lambda b, h, s: (b * n_c + s, h)
    return pl.pallas_call(
        functools.partial(_rec_body, hpb=hpb),
        grid=(bsz, N_HEADS // hpb, n_c),
        in_specs=[pl.BlockSpec(tri.shape, fixed), pl.BlockSpec(pair.shape, fixed),
                  pl.BlockSpec((c, w), tile), pl.BlockSpec((c, w), tile),
                  pl.BlockSpec((c, w), tile), pl.BlockSpec((c, w), tile),
                  pl.BlockSpec((c, w), tile), pl.BlockSpec((1, HEAD_DIM), fixed)],
        out_specs=pl.BlockSpec((c, w), tile),
        out_shape=jax.ShapeDtypeStruct((bsz * seq, D_MODEL), _BF),
        scratch_shapes=[pltpu.VMEM((hpb, HEAD_DIM, HEAD_DIM), _F32)],
        compiler_params=_cparams(("parallel", "parallel", "arbitrary")), name="hgrn_rec",
    )(tri, pair, q_s, log_f, k, v, g_s, g_norm)


def _router_body(x_ref, wr_ref, bias_ref, eidx_ref, gw_ref, rank_ref, cnt_ref, carry_ref, *, tm):
    @pl.when(pl.program_id(0) == 0)
    def _():
        carry_ref[...] = jnp.zeros_like(carry_ref)

    neg = -jnp.inf
    nt = (((1,), (1,)), ((), ()))
    x = x_ref[...]
    x_hi = x.astype(_BF)
    x_lo = (x - x_hi.astype(_F32)).astype(_BF)
    wr = wr_ref[...]
    w_hi = wr.astype(_BF)
    w_lo = (wr - w_hi.astype(_F32)).astype(_BF)
    logits = (lax.dot_general(w_hi, x_hi, nt, preferred_element_type=_F32) +
              lax.dot_general(w_hi, x_lo, nt, preferred_element_type=_F32) +
              lax.dot_general(w_lo, x_hi, nt, preferred_element_type=_F32))
    scores = _sigmoid(logits)
    choice = scores + bias_ref[...]
    c3 = choice.reshape(N_GROUPS, GROUP_SIZE, tm)
    i3 = lax.broadcasted_iota(jnp.int32, c3.shape, 1)
    m1 = jnp.max(c3, axis=1, keepdims=True)
    first = jnp.min(jnp.where(c3 == m1, i3, GROUP_SIZE), axis=1, keepdims=True)
    m2 = jnp.max(jnp.where(i3 == first, neg, c3), axis=1, keepdims=True)
    gs = (m1 + m2).reshape(N_GROUPS, tm)
    ig = lax.broadcasted_iota(jnp.int32, gs.shape, 0)
    gsel = jnp.zeros(gs.shape, jnp.bool_)
    for _ in range(TOPK_GROUPS):
        m = jnp.max(gs, axis=0, keepdims=True)
        gi = jnp.min(jnp.where(gs == m, ig, N_GROUPS), axis=0, keepdims=True)
        hit = ig == gi
        gsel = gsel | hit
        gs = jnp.where(hit, neg, gs)
    allowed = jnp.broadcast_to(gsel.reshape(N_GROUPS, 1, tm), c3.shape).reshape(N_EXPERTS, tm)
    masked = jnp.where(allowed, choice, neg)
    ie = lax.broadcasted_iota(jnp.int32, masked.shape, 0)
    picked = jnp.zeros(masked.shape, _F32)
    hits, e_rows, w_rows = [], [], []
    for _ in range(TOP_K):
        m = jnp.max(masked, axis=0, keepdims=True)
        ei = jnp.min(jnp.where(masked == m, ie, N_EXPERTS), axis=0, keepdims=True)
        hit = ie == ei
        hits.append(hit)
        e_rows.append(ei)
        w_rows.append(jnp.sum(jnp.where(hit, scores, 0.0), axis=0, keepdims=True))
        picked = picked + hit.astype(_F32)
        masked = jnp.where(hit, neg, masked)
    gw = jnp.concatenate(w_rows, axis=0)
    gw = gw / jnp.sum(gw, axis=0, keepdims=True) * ROUTED_SCALE
    before = (lax.broadcasted_iota(jnp.int32, (tm, tm), 0) <
              lax.broadcasted_iota(jnp.int32, (tm, tm), 1)).astype(_BF)
    cum = jnp.dot(picked.astype(_BF), before, preferred_element_type=_F32) + carry_ref[...]
    r_rows = [jnp.sum(jnp.where(h, cum, 0.0), axis=0, keepdims=True) for h in hits]
    carry = carry_ref[...] + jnp.sum(picked, axis=1, keepdims=True)
    carry_ref[...] = carry
    eidx_ref[...] = jnp.concatenate(e_rows, axis=0)
    gw_ref[...] = gw
    rank_ref[...] = jnp.concatenate(r_rows, axis=0).astype(jnp.int32)
    cnt_ref[...] = carry.astype(jnp.int32)


def _router(x, wr_t, bias_col, tm=512):
    t = x.shape[0]
    tok = lambda i: (0, i)
    fixed = lambda i: (0, 0)
    return pl.pallas_call(
        functools.partial(_router_body, tm=tm),
        grid=(t // tm,),
        in_specs=[pl.BlockSpec((tm, D_MODEL), lambda i: (i, 0)),
                  pl.BlockSpec((N_EXPERTS, D_MODEL), fixed),
                  pl.BlockSpec((N_EXPERTS, 1), fixed)],
        out_specs=[pl.BlockSpec((TOP_K, tm), tok), pl.BlockSpec((TOP_K, tm), tok),
                   pl.BlockSpec((TOP_K, tm), tok), pl.BlockSpec((N_EXPERTS, 1), fixed)],
        out_shape=[jax.ShapeDtypeStruct((TOP_K, t), jnp.int32),
                   jax.ShapeDtypeStruct((TOP_K, t), _F32),
                   jax.ShapeDtypeStruct((TOP_K, t), jnp.int32),
                   jax.ShapeDtypeStruct((N_EXPERTS, 1), jnp.int32)],
        scratch_shapes=[pltpu.VMEM((N_EXPERTS, 1), _F32)],
        compiler_params=_cparams(("arbitrary",)), name="router",
    )(x, wr_t, bias_col)


BLOCK_WORDS = ROW_BLOCK * PACK_ROWS
EXPERT_SPLIT = 512


def _dispatch_body(pend_ref, dest_ref, xp_ref, xb_ref, wg_ref, wu_ref, wd_ref, xs_hbm, sh_ref,
                   zbuf, sem, *, tm):
    i = pl.program_id(0)

    def block_copy(b):
        start = pl.multiple_of(b * BLOCK_WORDS, BLOCK_WORDS)
        return pltpu.make_async_copy(zbuf, xs_hbm.at[pl.ds(start, BLOCK_WORDS), :], sem.at[1])

    def pad_copy(e):
        return block_copy(pend_ref[e] // ROW_BLOCK - 1)

    def has_rows(e):
        return pend_ref[e] > jnp.where(e > 0, pend_ref[jnp.maximum(e - 1, 0)], 0)

    @pl.when(i == 0)
    def _():
        zbuf[...] = jnp.zeros_like(zbuf)

        def start(e, carry):
            @pl.when(has_rows(e))
            def _():
                pad_copy(e).start()
            return carry
        lax.fori_loop(0, N_EXPERTS, start, 0)

        def wait(e, carry):
            @pl.when(has_rows(e))
            def _():
                pad_copy(e).wait()
            return carry
        lax.fori_loop(0, N_EXPERTS, wait, 0)

        first_unused = pend_ref[N_EXPERTS - 1] // ROW_BLOCK
        n_blocks = xs_hbm.shape[0] // BLOCK_WORDS

        def tail_start(b, carry):
            block_copy(b).start()
            return carry
        lax.fori_loop(first_unused, n_blocks, tail_start, 0)

        def tail_wait(b, carry):
            block_copy(b).wait()
            return carry
        lax.fori_loop(first_unused, n_blocks, tail_wait, 0)

    def push(t, carry):
        src = xp_ref.at[pl.ds(pl.multiple_of(t * PACK_ROWS, PACK_ROWS), PACK_ROWS), :]
        for k in range(TOP_K):
            row = pl.multiple_of(dest_ref[k, t] * PACK_ROWS, PACK_ROWS)
            pltpu.make_async_copy(src, xs_hbm.at[pl.ds(row, PACK_ROWS), :],
                                  sem.at[0]).start(priority=k % 2)
        return carry
    lax.fori_loop(0, tm, push, 0)
    xb = xb_ref[...]
    hg = jnp.dot(xb, wg_ref[...], preferred_element_type=_F32)
    hu = jnp.dot(xb, wu_ref[...], preferred_element_type=_F32)
    sh_ref[...] = jnp.dot((_silu(hg) * hu).astype(_BF), wd_ref[...], preferred_element_type=_F32)
    for k in range(TOP_K):
        pltpu.make_async_copy(xp_ref, xs_hbm.at[pl.ds(0, tm * PACK_ROWS), :], sem.at[0]).wait()


def _dispatch(pad_end, dest, xp, xb, wg, wu, wd, n_rows, tm=256):
    t = xb.shape[0]
    n_tiles = t // tm
    fixed = lambda i, pe: (0, 0)
    grid_spec = pltpu.PrefetchScalarGridSpec(
        num_scalar_prefetch=1, grid=(n_tiles,),
        in_specs=[pl.BlockSpec((TOP_K, tm), lambda i, pe: (0, i), memory_space=pltpu.SMEM),
                  pl.BlockSpec((tm * PACK_ROWS, HEAD_DIM), lambda i, pe: (i, 0)),
                  pl.BlockSpec((tm, D_MODEL), lambda i, pe: (i, 0)),
                  pl.BlockSpec((D_MODEL, EXPERT_DIM), fixed),
                  pl.BlockSpec((D_MODEL, EXPERT_DIM), fixed),
                  pl.BlockSpec((EXPERT_DIM, D_MODEL), fixed)],
        out_specs=[pl.BlockSpec(memory_space=pl.ANY),
                   pl.BlockSpec((tm, D_MODEL), lambda i, pe: (i, 0))],
        scratch_shapes=[pltpu.VMEM((BLOCK_WORDS, HEAD_DIM), _BF),
                        pltpu.SemaphoreType.DMA((2,))])
    return pl.pallas_call(
        functools.partial(_dispatch_body, tm=tm), grid_spec=grid_spec,
        out_shape=[jax.ShapeDtypeStruct((n_rows * PACK_ROWS, HEAD_DIM), _BF),
                   jax.ShapeDtypeStruct((t, D_MODEL), _F32)],
        compiler_params=_cparams(("arbitrary",)), name="dispatch",
    )(pad_end, dest, xp, xb, wg, wu, wd)


def _expert_body(be_ref, nv_ref, nu_ref, seg_ref, nxt_ref, x_ref, wg_hbm, wu_hbm, wd_hbm, y_ref,
                 wg_b, wu_b, wd_b, stage, wg_f, wu_f, wd_f, sem, *, layer):
    i = pl.program_id(0)
    n_valid = nv_ref[i]
    new_expert = (i == 0) | (be_ref[i] != be_ref[jnp.maximum(i - 1, 0)])

    def weight_copies(e, s):
        return [pltpu.make_async_copy(src.at[layer, e], dst.at[s], sem.at[s])
                for src, dst in ((wg_hbm, wg_f), (wu_hbm, wu_f), (wd_hbm, wd_f))]

    @pl.when(new_expert & (n_valid > 0))
    def _():
        s = seg_ref[i] % 2

        @pl.when(i == 0)
        def _():
            for cp in weight_copies(be_ref[0], 0):
                cp.start()

        for cp in weight_copies(be_ref[i], s):
            cp.wait()

        @pl.when(nxt_ref[i] >= 0)
        def _():
            for cp in weight_copies(nxt_ref[i], 1 - s):
                cp.start()

        _cast_rows(wg_f.at[s], wg_b)
        _cast_rows(wu_f.at[s], wu_b)
        _cast_rows(wd_f.at[s], wd_b)

    @pl.when(n_valid > 0)
    def _():
        for r in range(ROW_BLOCK):
            stage[r * SLAB_PITCH:r * SLAB_PITCH + PACK_ROWS, :] = (
                x_ref[r * PACK_ROWS:(r + 1) * PACK_ROWS, :].astype(_F32))
        n_slab = EXPERT_SPLIT // HEAD_DIM
        hg = None
        hu = None
        for c in range(D_MODEL // EXPERT_SPLIT):
            ks = slice(c * EXPERT_SPLIT, (c + 1) * EXPERT_SPLIT)
            xc = jnp.concatenate(
                [stage[pl.ds(c * n_slab + j, ROW_BLOCK, stride=SLAB_PITCH), :]
                 for j in range(n_slab)], axis=1).astype(_BF)
            pg = jnp.dot(xc, wg_b[ks, :], preferred_element_type=_F32)
            pu = jnp.dot(xc, wu_b[ks, :], preferred_element_type=_F32)
            hg = pg if hg is None else hg + pg
            hu = pu if hu is None else hu + pu
        hh = (_silu(hg) * hu).astype(_BF)
        pitch = ROW_BLOCK + STAGE_PAD
        for c in range(D_MODEL // EXPERT_SPLIT):
            yc = jnp.dot(hh, wd_b[:, c * EXPERT_SPLIT:(c + 1) * EXPERT_SPLIT],
                         preferred_element_type=_F32)
            for j in range(n_slab):
                jj = c * n_slab + j
                stage[jj * pitch:jj * pitch + ROW_BLOCK, :] = yc[:, j * HEAD_DIM:(j + 1) * HEAD_DIM]
        for r in range(ROW_BLOCK):
            y_ref[r * PACK_ROWS:(r + 1) * PACK_ROWS, :] = stage[pl.ds(r, PACK_ROWS, stride=pitch), :]

    @pl.when(n_valid == 0)
    def _():
        y_ref[...] = jnp.zeros_like(y_ref)


def _experts(blk_expert, n_valid, n_used, seg, nxt, xs, wg, wu, wd, layer):
    n_blocks = xs.shape[0] // BLOCK_WORDS
    x_map = lambda i, be, nv, nu, sg, nx: (jnp.minimum(i, nu[0] - 1), 0)
    grid_spec = pltpu.PrefetchScalarGridSpec(
        num_scalar_prefetch=5, grid=(n_blocks,),
        in_specs=[pl.BlockSpec((BLOCK_WORDS, HEAD_DIM), x_map),
                  pl.BlockSpec(memory_space=pl.ANY),
                  pl.BlockSpec(memory_space=pl.ANY),
                  pl.BlockSpec(memory_space=pl.ANY)],
        out_specs=pl.BlockSpec((BLOCK_WORDS, HEAD_DIM), lambda i, be, nv, nu, sg, nx: (i, 0)),
        scratch_shapes=[pltpu.VMEM((D_MODEL, EXPERT_DIM), _BF),
                        pltpu.VMEM((D_MODEL, EXPERT_DIM), _BF),
                        pltpu.VMEM((EXPERT_DIM, D_MODEL), _BF),
                        pltpu.VMEM((ROW_BLOCK * SLAB_PITCH, HEAD_DIM), _F32),
                        pltpu.VMEM((2, D_MODEL, EXPERT_DIM), _F32),
                        pltpu.VMEM((2, D_MODEL, EXPERT_DIM), _F32),
                        pltpu.VMEM((2, EXPERT_DIM, D_MODEL), _F32),
                        pltpu.SemaphoreType.DMA((2,))])
    return pl.pallas_call(
        functools.partial(_expert_body, layer=layer), grid_spec=grid_spec,
        out_shape=jax.ShapeDtypeStruct(xs.shape, _F32),
        compiler_params=_cparams(("arbitrary",)), name="experts",
    )(blk_expert, n_valid, n_used, seg, nxt, xs, wg, wu, wd)


COMBINE_ROWS = 32


def _combine_body(dcur_ref, dnxt_ref, y_hbm, gw_ref, x_ref, sh_ref, g_ref, b_ref,
                  o_ref, ob_ref, buf, sem, *, tm, n_tiles):
    i = pl.program_id(0)
    slot = i % 2
    n_rows = TOP_K * tm * PACK_ROWS

    def start_gather(d_ref, s, t):
        for k in range(TOP_K):
            src = pl.multiple_of(d_ref[k, t] * PACK_ROWS, PACK_ROWS)
            dst = pl.multiple_of((k * tm + t) * SLAB_PITCH, 8)
            pltpu.make_async_copy(y_hbm.at[pl.ds(src, PACK_ROWS), :],
                                  buf.at[s, pl.ds(dst, PACK_ROWS), :],
                                  sem.at[s]).start(priority=k % 2)

    def wait_gather(s):
        pltpu.make_async_copy(y_hbm.at[pl.ds(0, n_rows), :], buf.at[s, pl.ds(0, n_rows), :],
                              sem.at[s]).wait()

    @pl.when(i == 0)
    def _():
        def first(t, carry):
            start_gather(dcur_ref, 0, t)
            return carry
        lax.fori_loop(0, tm, first, 0)

    o_ref[...] = ALPHA * x_ref[...] + sh_ref[...]
    wait_gather(slot)
    rows_buf = buf.at[slot]

    def add_routed(c, carry):
        r0 = pl.multiple_of(c * COMBINE_ROWS, COMBINE_ROWS)
        for u in range(COMBINE_ROWS):
            start_gather(dnxt_ref, 1 - slot, r0 + u)
        rows = pl.ds(r0, COMBINE_ROWS)
        gate = gw_ref[rows, :]
        gates = [jnp.broadcast_to(gate[:, k:k + 1], (COMBINE_ROWS, HEAD_DIM)) for k in range(TOP_K)]
        for j in range(PACK_ROWS):
            cols = slice(j * HEAD_DIM, (j + 1) * HEAD_DIM)
            acc = o_ref[rows, cols]
            for k in range(TOP_K):
                start = (k * tm + r0) * SLAB_PITCH + j
                acc = acc + gates[k] * rows_buf[pl.ds(start, COMBINE_ROWS, stride=SLAB_PITCH), :]
            o_ref[rows, cols] = acc
        return carry
    lax.fori_loop(0, tm // COMBINE_ROWS, add_routed, 0)
    r = _layer_norm(o_ref[...], g_ref[...], b_ref[...])
    o_ref[...] = r
    ob_ref[...] = r.astype(_BF)

    @pl.when(i == n_tiles - 1)
    def _():
        wait_gather(1 - slot)


def _combine(dest, y, gw_t, x, shared, g, b, tm=128):
    t = x.shape[0]
    nt = t // tm
    row = lambda i: (i, 0)
    fixed = lambda i: (0, 0)
    return pl.pallas_call(
        functools.partial(_combine_body, tm=tm, n_tiles=nt),
        grid=(nt,),
        in_specs=[pl.BlockSpec((TOP_K, tm), lambda i: (0, i), memory_space=pltpu.SMEM),
                  pl.BlockSpec((TOP_K, tm), lambda i: (0, jnp.minimum(i + 1, nt - 1)),
                               memory_space=pltpu.SMEM),
                  pl.BlockSpec(memory_space=pl.ANY),
                  pl.BlockSpec((tm, TOP_K), row),
                  pl.BlockSpec((tm, D_MODEL), row), pl.BlockSpec((tm, D_MODEL), row),
                  pl.BlockSpec((1, D_MODEL), fixed), pl.BlockSpec((1, D_MODEL), fixed)],
        out_specs=[pl.BlockSpec((tm, D_MODEL), row), pl.BlockSpec((tm, D_MODEL), row)],
        out_shape=[jax.ShapeDtypeStruct((t, D_MODEL), _F32),
                   jax.ShapeDtypeStruct((t, D_MODEL), _BF)],
        scratch_shapes=[pltpu.VMEM((2, TOP_K * tm * SLAB_PITCH, HEAD_DIM), _F32),
                        pltpu.SemaphoreType.DMA((2,))],
        compiler_params=_cparams(("arbitrary",)), name="combine",
    )(dest, dest, y, gw_t, x, shared, g, b)


def _moe(x, xb, xp, layer, w_router, e_bias, w_gate, w_up, w_down, ws_gate, ws_up, ws_down, g, b):
    t = x.shape[0]
    eidx, gw, rank, counts = _router(x, w_router.T, e_bias.reshape(N_EXPERTS, 1))
    counts = counts.reshape(N_EXPERTS)
    padded = (counts + ROW_BLOCK - 1) // ROW_BLOCK * ROW_BLOCK
    pad_end = jnp.cumsum(padded)
    pad_start = pad_end - padded
    n_blocks = t * TOP_K // ROW_BLOCK + N_EXPERTS
    expert_ids = jnp.arange(N_EXPERTS, dtype=jnp.int32)
    dest = rank + jnp.sum(jnp.where(eidx[None] == expert_ids[:, None, None],
                                    pad_start[:, None, None], 0), axis=0)
    blk_start = jnp.arange(n_blocks, dtype=jnp.int32) * ROW_BLOCK
    blk_expert = jnp.minimum(jnp.sum(pad_end[None, :] <= blk_start[:, None], axis=1),
                             N_EXPERTS - 1).astype(jnp.int32)
    n_valid = jnp.clip((pad_start + counts)[blk_expert] - blk_start, 0, ROW_BLOCK)
    n_valid = jnp.where(blk_start < pad_end[-1], n_valid, 0).astype(jnp.int32)
    n_used = (pad_end[-1:] // ROW_BLOCK).astype(jnp.int32)
    xs, shared = _dispatch(pad_end.astype(jnp.int32), dest, xp, xb, ws_gate.astype(_BF),
                           ws_up.astype(_BF), ws_down.astype(_BF), n_blocks * ROW_BLOCK)
    used = blk_start < pad_end[-1]
    change = jnp.concatenate([jnp.zeros((1,), jnp.int32),
                              (blk_expert[1:] != blk_expert[:-1]).astype(jnp.int32)])
    seg = jnp.cumsum(change)
    later = jnp.where(used[None, :] & (seg[None, :] == seg[:, None] + 1), blk_expert[None, :], -1)
    nxt = jnp.max(later, axis=1).astype(jnp.int32)
    y = _experts(blk_expert, n_valid, n_used, seg.astype(jnp.int32), nxt, xs,
                 w_gate, w_up, w_down, layer)
    return _combine(dest, y, gw.T, x, shared, g.reshape(1, -1), b.reshape(1, -1))


def kernel(x, ln_mix_g, ln_mix_b, ln_ffn_g, ln_ffn_b, gmlp_w_in, gmlp_v_ln_g, gmlp_v_ln_b, gmlp_w_sp, gmlp_b_sp, gmlp_w_out, hgrn_w_in, hgrn_o_norm_g, hgrn_w_out, hgrn_lower_bounds, moe_w_router, moe_e_bias, moe_w_gate, moe_w_up, moe_w_down, moe_ws_gate, moe_ws_up, moe_ws_down):
    bsz, seq, d = x.shape
    t = bsz * seq
    row = lambda a: a.reshape(1, -1)
    xf = x.reshape(t, d)

    z = _proj(xf, gmlp_w_in[0], _epi_gelu, [_BF], 2 * d, name="gmlp_in")[0]
    y = _sgu(z, row(gmlp_v_ln_g[0]), row(gmlp_v_ln_b[0]), gmlp_w_sp[0], gmlp_b_sp[0].T)
    xf, xb, xp = _out_ln(y, gmlp_w_out[0].astype(_BF), xf, row(ln_mix_g[0]), row(ln_mix_b[0]))
    xf, xb = _moe(xf, xb, xp, 0, moe_w_router[0], moe_e_bias[0], moe_w_gate, moe_w_up,
                     moe_w_down, moe_ws_gate[0], moe_ws_up[0], moe_ws_down[0],
                     ln_ffn_g[0], ln_ffn_b[0])

    lb_soft = jax.nn.softmax(hgrn_lower_bounds.astype(_F32), axis=0)
    lb = (jnp.cumsum(lb_soft, axis=0) - lb_soft[0])[1]
    w_in = hgrn_w_in[0]
    q_s = _proj(xb, w_in, _epi_silu, [_BF], d, col0=0, name="hgrn_q")[0]
    log_f, kk = _proj(xb, w_in, _epi_forget, [_F32, _BF], d, col0=d,
                      vecs=(row(jnp.log(lb)), row(jnp.log1p(-lb)), row(1.0 - lb)), name="hgrn_f")
    vv = _proj(xb, w_in, _epi_id, [_BF], d, col0=2 * d, name="hgrn_i")[0]
    g_s = _proj(xb, w_in, _epi_silu, [_BF], d, col0=3 * d, name="hgrn_g")[0]
    o = _recurrence(q_s, log_f, kk, vv, g_s, row(hgrn_o_norm_g[0]), bsz, seq)
    xf, xb, xp = _out_ln(o, hgrn_w_out[0].astype(_BF), xf, row(ln_mix_g[1]), row(ln_mix_b[1]))
    xf, xb = _moe(xf, xb, xp, 1, moe_w_router[1], moe_e_bias[1], moe_w_gate, moe_w_up,
                     moe_w_down, moe_ws_gate[1], moe_ws_up[1], moe_ws_down[1],
                     ln_ffn_g[1], ln_ffn_b[1])
    return xf.reshape(bsz, seq, d)
```
